```python
import math
import jax
import jax.numpy as jnp
from jax import lax
import numpy as np

D_MODEL = 1024
BATCH = 2
SEQ = 16384
DEPTH = 4

GRID_W = 64
CTX_LEN = 256
N_MIXERS = 4
EPS = 1e-6

GMLP_CHUNK = 128
GMLP_WIDTH = 2 * D_MODEL
GMLP_GROUPS = 8

LRU_WIDTH = D_MODEL
LRU_HEADS = 8
LRU_HEAD_DIM = LRU_WIDTH // LRU_HEADS
LRU_CONV = 4
LRU_C = 8.0

HYENA_WIDTH = D_MODEL
HYENA_ORDER = 2
HYENA_CONV = 3
HYENA_BANDS = 16
HYENA_EMB = 2 * HYENA_BANDS + 1
HYENA_FILTER_HIDDEN = 64
HYENA_FAST_DECAY = 0.3
HYENA_SLOW_DECAY = 1.5
HYENA_DECAY_TARGET = 1e-2

POOL_WIDTH = D_MODEL
POOL_WINDOWS = (2, 4, 8, 16)
POOL_GROUP = POOL_WIDTH // len(POOL_WINDOWS)

D_FF = 2816
N_EXPERTS = 8
TOP_K = 2
EXPERT_FF = 3584

kernel_name = 'hybrid_interleaved_diffusion_block'


def _n_uses(first, period):
    return len(range(first, DEPTH, period))


def rmsnorm(x, g):
    x32 = x.astype(jnp.float32)
    y = x32 * lax.rsqrt(jnp.mean(x32 * x32, axis=-1, keepdims=True) + EPS)
    return (y * g.astype(jnp.float32)).astype(x.dtype)


def layernorm(x, g):
    x32 = x.astype(jnp.float32)
    mu = jnp.mean(x32, axis=-1, keepdims=True)
    xc = x32 - mu
    y = xc * lax.rsqrt(jnp.mean(xc * xc, axis=-1, keepdims=True) + EPS)
    return (y * g.astype(jnp.float32)).astype(x.dtype)


def modulate(x, shift, scale):
    return x * (1.0 + scale) + shift


def grid_pos_embed(rows, dtype):
    r, col = jnp.meshgrid(jnp.arange(rows, dtype=jnp.float32),
                          jnp.arange(GRID_W, dtype=jnp.float32), indexing='ij')
    quarter = D_MODEL // 4
    omega = 1.0 / (10000.0 ** (jnp.arange(quarter, dtype=jnp.float32) / quarter))

    def sincos(p):
        ang = p.reshape(-1, 1) * omega[None, :]
        return jnp.concatenate([jnp.sin(ang), jnp.cos(ang)], axis=-1)

    return jnp.concatenate([sincos(r), sincos(col)], axis=-1).astype(dtype)


def depthwise_conv_centred(x, w, b):
    K = w.shape[0]
    L = x.shape[1]
    left = K // 2
    xp = jnp.pad(x, ((0, 0), (left, K - 1 - left), (0, 0)))
    y = b
    for k in range(K):
        y = y + xp[:, k:k + L] * w[k]
    return y


def gmlp_mixer(h, w_in, g_v, w_s, b_s, w_out):
    B, L, _ = h.shape
    u, v = jnp.split(jax.nn.gelu(h @ w_in), 2, axis=-1)
    v = layernorm(v, g_v)
    v = v.reshape(B, L // GMLP_CHUNK, GMLP_CHUNK, GMLP_GROUPS, GMLP_WIDTH // GMLP_GROUPS)
    v = jnp.einsum('gts,bnsgc->bntgc', w_s, v) + b_s.T[:, :, None]
    return (u * v.reshape(B, L, GMLP_WIDTH)) @ w_out


def rglru_scan(x, w_a, b_a, w_x, b_x, lam, h0, reverse):
    B, L, W = x.shape
    xh = x.reshape(B, L, LRU_HEADS, LRU_HEAD_DIM)
    r = jax.nn.sigmoid(jnp.einsum('blhi,hij->blhj', xh, w_a).reshape(B, L, W).astype(jnp.float32) + b_a)
    i = jax.nn.sigmoid(jnp.einsum('blhi,hij->blhj', xh, w_x).reshape(B, L, W).astype(jnp.float32) + b_x)
    log_a = -LRU_C * r * jax.nn.softplus(-lam.astype(jnp.float32))
    a = jnp.exp(log_a)
    b = jnp.sqrt(-jnp.expm1(2.0 * log_a)) * (i * x.astype(jnp.float32))
    if h0 is not None:
        first = L - 1 if reverse else 0
        b = b.at[:, first].add(a[:, first] * h0)

    def combine(e1, e2):
        a1, b1 = e1
        a2, b2 = e2
        return a1 * a2, a2 * b1 + b2

    _, hs = lax.associative_scan(combine, (a, b), axis=1, reverse=reverse)
    return hs


def lru_mixer(h_lat, h_ctx, w_in, conv_w, conv_b, w_a, b_a, w_x, b_x, lam, w_out, ctx_out):
    def scans(xb, s_f, s_b):
        hf = rglru_scan(xb, w_a[0], b_a[0], w_x[0], b_x[0], lam[0], s_f, False)
        hb = rglru_scan(xb, w_a[1], b_a[1], w_x[1], b_x[1], lam[1], s_b, True)
        return hf, hb

    gate_c, xb_c = jnp.split(h_ctx @ w_in, 2, axis=-1)
    hf_c, hb_c = scans(depthwise_conv_centred(xb_c, conv_w, conv_b), None, None)
    gate_l, xb_l = jnp.split(h_lat @ w_in, 2, axis=-1)
    hf_l, hb_l = scans(depthwise_conv_centred(xb_l, conv_w, conv_b), hf_c[:, -1], hb_c[:, 0])
    g_l = jax.nn.gelu(gate_l)
    y_lat = (g_l * (hf_l + hb_l).astype(g_l.dtype)) @ w_out
    y_ctx = None
    if ctx_out:
        g_c = jax.nn.gelu(gate_c)
        y_ctx = (g_c * (hf_c + hb_c).astype(g_c.dtype)) @ w_out
    return y_lat, y_ctx


def hyena_filters(L, f_w1, f_b1, f_w2, f_b2, f_w3, f_b3, f_freq, f_wout):
    f32 = jnp.float32
    t = jnp.linspace(0.0, 1.0, L, dtype=f32)[:, None]
    w = 2.0 * math.pi * jnp.arange(L, dtype=f32)[:, None] / L
    bands = jnp.linspace(1e-4, HYENA_BANDS - 1, HYENA_BANDS, dtype=f32)[None, :]
    z = jnp.concatenate([t, jnp.cos(bands * w), jnp.sin(-bands * w)], axis=-1)
    fr = f_freq.astype(f32)
    hdn = jnp.sin(fr[0] * (z @ f_w1.astype(f32) + f_b1.astype(f32)))
    hdn = jnp.sin(fr[1] * (hdn @ f_w2.astype(f32) + f_b2.astype(f32)))
    hdn = jnp.sin(fr[2] * (hdn @ f_w3.astype(f32) + f_b3.astype(f32)))
    filt = (hdn @ f_wout.astype(f32)).reshape(L, HYENA_ORDER, 2, HYENA_WIDTH)
    max_decay = math.log(HYENA_DECAY_TARGET) / HYENA_FAST_DECAY
    min_decay = math.log(HYENA_DECAY_TARGET) / HYENA_SLOW_DECAY
    deltas = jnp.abs(jnp.linspace(min_decay, max_decay, HYENA_WIDTH, dtype=f32))
    window = jnp.exp(-t * deltas[None, :])
    return filt * window[:, None, None, :]


def two_sided_long_conv(v, h_fwd, h_bwd, bias):
    B, L, C = v.shape
    v32 = v.astype(jnp.float32)
    k = jnp.concatenate([h_fwd, jnp.zeros((1, C), jnp.float32), h_bwd[:0:-1]], axis=0)
    n = 2 * L
    y = jnp.fft.irfft(jnp.fft.rfft(v32, n=n, axis=1) * jnp.fft.rfft(k, n=n, axis=0)[None], n=n, axis=1)[:, :L]
    return y + v32 * bias.astype(jnp.float32)


def hyena_mixer(h, w_in, conv_w, conv_b, f_w1, f_b1, f_w2, f_b2, f_w3, f_b3, f_freq, f_wout, bias, w_out):
    B, L, _ = h.shape
    z = depthwise_conv_centred(h @ w_in, conv_w, conv_b).astype(jnp.float32)
    x1, x2, v = jnp.split(z, 3, axis=-1)
    filt = hyena_filters(L, f_w1, f_b1, f_w2, f_b2, f_w3, f_b3, f_freq, f_wout)
    y = v
    for o, gate in enumerate((x1, x2)):
        y = gate * two_sided_long_conv(y, filt[:, o, 0], filt[:, o, 1], bias[o])
    return y.astype(h.dtype) @ w_out


def pool_mixer(h, w_in, w_g, b_g, scale, w_out):
    B, L, _ = h.shape
    G = len(POOL_WINDOWS)
    p = (h @ w_in).astype(jnp.float32).reshape(B, L, G, POOL_GROUP)
    S = jnp.concatenate([jnp.zeros((B, 1, G, POOL_GROUP), jnp.float32), jnp.cumsum(p, axis=1)], axis=1)
    t = jnp.arange(L)
    pooled = []
    for g, win in enumerate(POOL_WINDOWS):
        lo = jnp.clip(t - win // 2, 0, L)
        hi = jnp.clip(t + win // 2, 0, L)
        cnt = (hi - lo).astype(jnp.float32)[None, :, None]
        Sg = S[:, :, g]
        pooled.append((Sg[:, hi] - Sg[:, lo]) / cnt - p[:, :, g])
    q = jnp.stack(pooled, axis=2)
    y = jnp.einsum('blgc,gcd->blgd', q, w_g.astype(jnp.float32)) + b_g.astype(jnp.float32)
    y = y.reshape(B, L, POOL_WIDTH) * scale.astype(jnp.float32)
    return y.astype(h.dtype) @ w_out


def swiglu(h, w_gate, w_up, w_down):
    return (jax.nn.silu(h @ w_gate) * (h @ w_up)) @ w_down


def moe_ffn(h, w_router, w_gate, w_up, w_down):
    shape = h.shape
    hf = h.reshape(-1, shape[-1])
    logits = (hf @ w_router).astype(jnp.float32)
    top_val, top_idx = lax.top_k(logits, TOP_K)
    probs = jax.nn.softmax(top_val, axis=-1)
    combine = jnp.sum(jax.nn.one_hot(top_idx, N_EXPERTS, dtype=jnp.float32) * probs[..., None], axis=1)
    out = jnp.zeros_like(hf)
    for e in range(N_EXPERTS):
        out = out + combine[:, e:e + 1].astype(hf.dtype) * swiglu(hf, w_gate[e], w_up[e], w_down[e])
    return out.reshape(shape)


def setup_inputs(seed: int = 0) -> dict:
    key = jax.random.key(seed)
    keys = iter(jax.random.split(key, 64))
    f32 = jnp.float32

    def nrm(shape, scale):
        return jax.random.normal(next(keys), shape, f32) * scale

    D = D_MODEL
    nA, nB, nC, nD = (_n_uses(k, N_MIXERS) for k in range(N_MIXERS))
    nF, nM = _n_uses(0, 2), _n_uses(1, 2)
    inp = {}
    inp['x'] = nrm((BATCH, SEQ, D), 1.0)
    inp['c'] = nrm((BATCH, D), 1.0)
    inp['ctx'] = nrm((BATCH, CTX_LEN, D), 1.0)
    inp['c_ctx'] = nrm((D,), 1.0)
    inp['ada_w'] = nrm((DEPTH, D, 6 * D), 0.5 * D ** -0.5)
    inp['ada_b'] = nrm((DEPTH, 6 * D), 0.02)
    inp['norm_g'] = 1.0 + nrm((DEPTH, 4, D), 0.05)
    inp['gmlp_w_in'] = nrm((nA, D, 2 * GMLP_WIDTH), D ** -0.5)
    inp['gmlp_g_v'] = 1.0 + nrm((nA, GMLP_WIDTH), 0.05)
    inp['gmlp_w_s'] = nrm((nA, GMLP_GROUPS, GMLP_CHUNK, GMLP_CHUNK), GMLP_CHUNK ** -0.5)
    inp['gmlp_b_s'] = 1.0 + nrm((nA, GMLP_GROUPS, GMLP_CHUNK), 0.02)
    inp['gmlp_w_out'] = nrm((nA, GMLP_WIDTH, D), GMLP_WIDTH ** -0.5)
    inp['lru_w_in'] = nrm((nB, D, 2 * LRU_WIDTH), D ** -0.5)
    inp['lru_conv_w'] = nrm((nB, LRU_CONV, LRU_WIDTH), LRU_CONV ** -0.5)
    inp['lru_conv_b'] = nrm((nB, LRU_WIDTH), 0.02)
    inp['lru_w_a'] = nrm((nB, 2, LRU_HEADS, LRU_HEAD_DIM, LRU_HEAD_DIM), LRU_HEAD_DIM ** -0.5)
    inp['lru_b_a'] = nrm((nB, 2, LRU_WIDTH), 0.02)
    inp['lru_w_x'] = nrm((nB, 2, LRU_HEADS, LRU_HEAD_DIM, LRU_HEAD_DIM), LRU_HEAD_DIM ** -0.5)
    inp['lru_b_x'] = nrm((nB, 2, LRU_WIDTH), 0.02)
    u = jax.random.uniform(next(keys), (nB, 2, LRU_WIDTH), f32, minval=0.9, maxval=0.999)
    a = u ** (1.0 / LRU_C)
    inp['lru_lam'] = jnp.log(a) - jnp.log1p(-a)
    inp['lru_w_out'] = nrm((nB, LRU_WIDTH, D), LRU_WIDTH ** -0.5)
    inp['hyena_w_in'] = nrm((nC, D, 3 * HYENA_WIDTH), D ** -0.5)
    inp['hyena_conv_w'] = nrm((nC, HYENA_CONV, 3 * HYENA_WIDTH), HYENA_CONV ** -0.5)
    inp['hyena_conv_b'] = nrm((nC, 3 * HYENA_WIDTH), 0.02)
    inp['hyena_f_w1'] = nrm((nC, HYENA_EMB, HYENA_FILTER_HIDDEN), HYENA_EMB ** -0.5)
    inp['hyena_f_b1'] = nrm((nC, HYENA_FILTER_HIDDEN), 0.02)
    inp['hyena_f_w2'] = nrm((nC, HYENA_FILTER_HIDDEN, HYENA_FILTER_HIDDEN), HYENA_FILTER_HIDDEN ** -0.5)
    inp['hyena_f_b2'] = nrm((nC, HYENA_FILTER_HIDDEN), 0.02)
    inp['hyena_f_w3'] = nrm((nC, HYENA_FILTER_HIDDEN, HYENA_FILTER_HIDDEN), HYENA_FILTER_HIDDEN ** -0.5)
    inp['hyena_f_b3'] = nrm((nC, HYENA_FILTER_HIDDEN), 0.02)
    inp['hyena_f_freq'] = 1.0 + nrm((nC, 3, HYENA_FILTER_HIDDEN), 0.05)
    inp['hyena_f_wout'] = nrm((nC, HYENA_FILTER_HIDDEN, HYENA_ORDER * 2 * HYENA_WIDTH), 0.1 * HYENA_FILTER_HIDDEN ** -0.5)
    inp['hyena_bias'] = nrm((nC, HYENA_ORDER, HYENA_WIDTH), 0.1)
    inp['hyena_w_out'] = nrm((nC, HYENA_WIDTH, D), HYENA_WIDTH ** -0.5)
    inp['pool_w_in'] = nrm((nD, D, POOL_WIDTH), D ** -0.5)
    inp['pool_w_g'] = nrm((nD, len(POOL_WINDOWS), POOL_GROUP, POOL_GROUP), POOL_GROUP ** -0.5)
    inp['pool_b_g'] = nrm((nD, len(POOL_WINDOWS), POOL_GROUP), 0.02)
    inp['pool_scale'] = 1.0 + nrm((nD, POOL_WIDTH), 0.05)
    inp['pool_w_out'] = nrm((nD, POOL_WIDTH, D), POOL_WIDTH ** -0.5)
    inp['ffn_w_gate'] = nrm((nF, D, D_FF), D ** -0.5)
    inp['ffn_w_up'] = nrm((nF, D, D_FF), D ** -0.5)
    inp['ffn_w_down'] = nrm((nF, D_FF, D), D_FF ** -0.5)
    inp['moe_w_router'] = nrm((nM, D, N_EXPERTS), D ** -0.5)
    inp['moe_w_gate'] = nrm((nM, N_EXPERTS, D, EXPERT_FF), D ** -0.5)
    inp['moe_w_up'] = nrm((nM, N_EXPERTS, D, EXPERT_FF), D ** -0.5)
    inp['moe_w_down'] = nrm((nM, N_EXPERTS, EXPERT_FF, D), EXPERT_FF ** -0.5)
    return inp


def reference(x, c, ctx, c_ctx, ada_w, ada_b, norm_g,
              gmlp_w_in, gmlp_g_v, gmlp_w_s, gmlp_b_s, gmlp_w_out,
              lru_w_in, lru_conv_w, lru_conv_b, lru_w_a, lru_b_a, lru_w_x, lru_b_x, lru_lam, lru_w_out,
              hyena_w_in, hyena_conv_w, hyena_conv_b, hyena_f_w1, hyena_f_b1, hyena_f_w2, hyena_f_b2,
              hyena_f_w3, hyena_f_b3, hyena_f_freq, hyena_f_wout, hyena_bias, hyena_w_out,
              pool_w_in, pool_w_g, pool_b_g, pool_scale, pool_w_out,
              ffn_w_gate, ffn_w_up, ffn_w_down,
              moe_w_router, moe_w_gate, moe_w_up, moe_w_down):
    B, L, _ = x.shape
    rows = L // GRID_W
    x = x + grid_pos_embed(rows, x.dtype)[None]
    xc = ctx
    last_ctx = max([i for i in range(DEPTH) if i % N_MIXERS == 1], default=-1)

    for i in range(DEPTH):
        kind, j = i % N_MIXERS, i // N_MIXERS
        update_ctx = i < last_ctx
        read_ctx = i <= last_ctx
        sh1, sc1, gt1, sh2, sc2, gt2 = jnp.split((jax.nn.silu(c) @ ada_w[i] + ada_b[i])[:, None, :], 6, axis=-1)
        h = modulate(rmsnorm(x, norm_g[i, 0]), sh1, sc1)
        hc = None
        if read_ctx:
            csh1, csc1, cgt1, csh2, csc2, cgt2 = jnp.split(jax.nn.silu(c_ctx) @ ada_w[i] + ada_b[i], 6, axis=-1)
            hc = modulate(rmsnorm(xc, norm_g[i, 0]), csh1, csc1)

        y_c = None
        if kind == 0:
            p = (gmlp_w_in[j], gmlp_g_v[j], gmlp_w_s[j], gmlp_b_s[j], gmlp_w_out[j])
            y = gmlp_mixer(h, *p)
            if update_ctx:
                y_c = gmlp_mixer(hc, *p)
        elif kind == 1:
            y, y_c = lru_mixer(h, hc, lru_w_in[j], lru_conv_w[j], lru_conv_b[j], lru_w_a[j], lru_b_a[j],
                               lru_w_x[j], lru_b_x[j], lru_lam[j], lru_w_out[j], update_ctx)
        elif kind == 2:
            p = (hyena_w_in[j], hyena_conv_w[j], hyena_conv_b[j], hyena_f_w1[j], hyena_f_b1[j],
                 hyena_f_w2[j], hyena_f_b2[j], hyena_f_w3[j], hyena_f_b3[j], hyena_f_freq[j],
                 hyena_f_wout[j], hyena_bias[j], hyena_w_out[j])
            y = hyena_mixer(h, *p)
            if update_ctx:
                y_c = hyena_mixer(hc, *p)
        else:
            p = (pool_w_in[j], pool_w_g[j], pool_b_g[j], pool_scale[j], pool_w_out[j])
            y = pool_mixer(h, *p)
            if update_ctx:
                y_c = pool_mixer(hc, *p)

        x = x + gt1 * rmsnorm(y, norm_g[i, 1])
        if update_ctx:
            xc = xc + cgt1 * rmsnorm(y_c, norm_g[i, 1])

        k = i // 2

        def channel_mixer(t):
            if i % 2 == 0:
                return swiglu(t, ffn_w_gate[k], ffn_w_up[k], ffn_w_down[k])
            return moe_ffn(t, moe_w_router[k], moe_w_gate[k], moe_w_up[k], moe_w_down[k])

        x = x + gt2 * rmsnorm(channel_mixer(modulate(rmsnorm(x, norm_g[i, 2]), sh2, sc2)), norm_g[i, 3])
        if update_ctx:
            xc = xc + cgt2 * rmsnorm(channel_mixer(modulate(rmsnorm(xc, norm_g[i, 2]), csh2, csc2)), norm_g[i, 3])
    return x
```

```python
import functools
import math

import jax
import jax.numpy as jnp
import numpy as np
from jax import lax
from jax.experimental import pallas as pl
from jax.experimental.pallas import tpu as pltpu

F32 = jnp.float32
BF16 = jnp.bfloat16
EPS = 1e-6

VMEM_LIMIT_BYTES = 52 * 1024 * 1024
LANES = 128
SUBLANES = 8

GRID_W = 64
GMLP_CHUNK = 128
GMLP_GROUPS = 8
LRU_HEADS = 8
LRU_C = 8.0
POOL_WINDOWS = (2, 4, 8, 16)
HYENA_BANDS = 16
HYENA_FAST_DECAY = 0.3
HYENA_SLOW_DECAY = 1.5
HYENA_DECAY_TARGET = 1e-2
N_EXPERTS = 8
TOP_K = 2
FFT_P = 128
HALO = SUBLANES


def _cparams(*sem):
    return pltpu.CompilerParams(dimension_semantics=sem, vmem_limit_bytes=VMEM_LIMIT_BYTES)


def _rms(x, g):
    return x * lax.rsqrt(jnp.mean(x * x, axis=-1, keepdims=True) + EPS) * g


def _gelu(x):
    return 0.5 * x * (1.0 + jnp.tanh(math.sqrt(2.0 / math.pi) * (x + 0.044715 * (x * x * x))))


def _silu(x):
    return x * (1.0 / (1.0 + jnp.exp(-x)))


def _sigmoid(x):
    return 1.0 / (1.0 + jnp.exp(-x))


def _bdot(a, b):
    return jnp.dot(a, b, preferred_element_type=F32)


def _hdot(a, b):
    return jnp.dot(a, b, preferred_element_type=F32, precision=lax.Precision.HIGHEST)


def _row_tile(rows, want):
    t = min(rows, want)
    assert rows % t == 0, (rows, t)
    return t


def _ada_kernel(c_ref, w_ref, b_ref, o_ref):
    o_ref[0] = _hdot(_silu(c_ref[...]), w_ref[0]) + b_ref[0]


def ada_all(cc, ada_w, ada_b):
    depth, d, d6 = ada_w.shape
    nchunk = d6 // d
    return pl.pallas_call(
        _ada_kernel,
        grid=(depth, nchunk),
        in_specs=[
            pl.BlockSpec((SUBLANES, d), lambda i, j: (0, 0)),
            pl.BlockSpec((1, d, d), lambda i, j: (i, 0, j)),
            pl.BlockSpec((1, 1, d), lambda i, j: (i, 0, j)),
        ],
        out_specs=pl.BlockSpec((1, SUBLANES, d), lambda i, j: (i, 0, j)),
        out_shape=jax.ShapeDtypeStruct((depth, SUBLANES, d6), F32),
        compiler_params=_cparams("parallel", "parallel"),
        name="ada",
    )(cc, ada_w, ada_b.reshape(depth, 1, d6))


def _pos_kernel(x_ref, rt_ref, ct_ref, o_ref):
    half = rt_ref.shape[-1]
    x = x_ref[...]
    o_ref[:, :, :half] = x[:, :, :half] + rt_ref[...]
    o_ref[:, :, half:] = x[:, :, half:] + ct_ref[...][None]


def add_pos(x):
    b, l, d = x.shape
    rows = l // GRID_W
    quarter = d // 4
    omega = 1.0 / (10000.0 ** (jnp.arange(quarter, dtype=F32) / quarter))

    def sincos(p):
        ang = p.reshape(-1, 1) * omega[None, :]
        return jnp.concatenate([jnp.sin(ang), jnp.cos(ang)], axis=-1)

    rtab = sincos(jnp.arange(rows, dtype=F32)).reshape(rows, 1, 2 * quarter)
    ctab = sincos(jnp.arange(GRID_W, dtype=F32))
    x3 = x.reshape(b * rows, GRID_W, d)
    tr = _row_tile(rows, 16)
    nrt = rows // tr
    out = pl.pallas_call(
        _pos_kernel,
        grid=(b * nrt,),
        in_specs=[
            pl.BlockSpec((tr, GRID_W, d), lambda i: (i, 0, 0)),
            pl.BlockSpec((tr, 1, 2 * quarter), lambda i: (i % nrt, 0, 0)),
            pl.BlockSpec((GRID_W, 2 * quarter), lambda i: (0, 0)),
        ],
        out_specs=pl.BlockSpec((tr, GRID_W, d), lambda i: (i, 0, 0)),
        out_shape=jax.ShapeDtypeStruct(x3.shape, F32),
        compiler_params=_cparams("parallel"),
        name="add_pos",
    )(x3, rtab, ctab)
    return out.reshape(b * l, d)


def _nmm_kernel(x_ref, g_ref, sh_ref, sc_ref, w_ref, o_ref, h_scr, *, act):
    @pl.when(pl.program_id(1) == 0)
    def _():
        h = _rms(x_ref[...], g_ref[...]) * (1.0 + sc_ref[0]) + sh_ref[0]
        h_scr[...] = h.astype(BF16)

    y = _bdot(h_scr[...], w_ref[...])
    if act == "gelu":
        y = _gelu(y)
    o_ref[...] = y.astype(o_ref.dtype)


def norm_mod_matmul(x, g, shift, scale, w, rows_per_mod, out_dtype, act=None, tm=512, tn=2048):
    r, d = x.shape
    n = w.shape[1]
    tm = _row_tile(min(r, rows_per_mod), tm)
    tn = min(tn, n)
    assert n % tn == 0
    mod_map = lambda i, j: ((i * tm) // rows_per_mod, 0, 0)
    return pl.pallas_call(
        functools.partial(_nmm_kernel, act=act),
        grid=(r // tm, n // tn),
        in_specs=[
            pl.BlockSpec((tm, d), lambda i, j: (i, 0)),
            pl.BlockSpec((1, d), lambda i, j: (0, 0)),
            pl.BlockSpec((1, 1, d), mod_map),
            pl.BlockSpec((1, 1, d), mod_map),
            pl.BlockSpec((d, tn), lambda i, j: (0, j)),
        ],
        out_specs=pl.BlockSpec((tm, tn), lambda i, j: (i, j)),
        out_shape=jax.ShapeDtypeStruct((r, n), out_dtype),
        scratch_shapes=[pltpu.VMEM((tm, d), BF16)],
        compiler_params=_cparams("parallel", "arbitrary"),
        name="norm_mod_matmul",
    )(x, g.reshape(1, d), shift, scale, w.astype(BF16))


def _mnr_kernel(*refs, n_in, prologue):
    in_refs = refs[:n_in]
    w_ref, g_ref, gate_ref, x_ref, o_ref = refs[n_in:]
    a = prologue(*[r[...] for r in in_refs])
    y = _bdot(a, w_ref[...])
    o_ref[...] = x_ref[...] + gate_ref[0] * _rms(y, g_ref[...])


def matmul_norm_res(inputs, prologue, w, g, gate, x, rows_per_mod, tm=512):
    r, d = x.shape
    k = w.shape[0]
    tm = _row_tile(min(r, rows_per_mod), tm)
    mod_map = lambda i: ((i * tm) // rows_per_mod, 0, 0)
    in_specs = [pl.BlockSpec((tm, wd), functools.partial(lambda i, cb: (i, cb), cb=cb))
                for (_, wd, cb) in inputs]
    in_specs += [
        pl.BlockSpec((k, d), lambda i: (0, 0)),
        pl.BlockSpec((1, d), lambda i: (0, 0)),
        pl.BlockSpec((1, 1, d), mod_map),
        pl.BlockSpec((tm, d), lambda i: (i, 0)),
    ]
    return pl.pallas_call(
        functools.partial(_mnr_kernel, n_in=len(inputs), prologue=prologue),
        grid=(r // tm,),
        in_specs=in_specs,
        out_specs=pl.BlockSpec((tm, d), lambda i: (i, 0)),
        out_shape=jax.ShapeDtypeStruct((r, d), F32),
        compiler_params=_cparams("parallel"),
        name="matmul_norm_res",
    )(*[a for (a, _, _) in inputs], w.astype(BF16), g.reshape(1, d), gate, x)


def _identity_bf16(a):
    return a.astype(BF16)


def _swiglu_step(h_scr, wg_ref, wu_ref, wd_ref, acc_scr, j):
    h = h_scr[...]
    t = (_silu(_bdot(h, wg_ref[...])) * _bdot(h, wu_ref[...])).astype(BF16)
    part = _bdot(t, wd_ref[...])

    @pl.when(j == 0)
    def _():
        acc_scr[...] = part

    @pl.when(j > 0)
    def _():
        acc_scr[...] += part


def _ffn_dense_kernel(x_ref, g1_ref, sh_ref, sc_ref, wg_ref, wu_ref, wd_ref, g2_ref, gate_ref,
                      o_ref, h_scr, acc_scr):
    j = pl.program_id(1)

    @pl.when(j == 0)
    def _():
        h = _rms(x_ref[...], g1_ref[...]) * (1.0 + sc_ref[0]) + sh_ref[0]
        h_scr[...] = h.astype(BF16)

    _swiglu_step(h_scr, wg_ref, wu_ref, wd_ref, acc_scr, j)

    @pl.when(j == pl.num_programs(1) - 1)
    def _():
        o_ref[...] = x_ref[...] + gate_ref[0] * _rms(acc_scr[...], g2_ref[...])


def ffn_dense(x, g1, shift, scale, wg, wu, wd, g2, gate, rows_per_mod, tm=1024, tf=256):
    r, d = x.shape
    ff = wg.shape[1]
    tm = _row_tile(min(r, rows_per_mod), tm)
    assert ff % tf == 0
    mod_map = lambda i, j: ((i * tm) // rows_per_mod, 0, 0)
    return pl.pallas_call(
        _ffn_dense_kernel,
        grid=(r // tm, ff // tf),
        in_specs=[
            pl.BlockSpec((tm, d), lambda i, j: (i, 0)),
            pl.BlockSpec((1, d), lambda i, j: (0, 0)),
            pl.BlockSpec((1, 1, d), mod_map),
            pl.BlockSpec((1, 1, d), mod_map),
            pl.BlockSpec((d, tf), lambda i, j: (0, j)),
            pl.BlockSpec((d, tf), lambda i, j: (0, j)),
            pl.BlockSpec((tf, d), lambda i, j: (j, 0)),
            pl.BlockSpec((1, d), lambda i, j: (0, 0)),
            pl.BlockSpec((1, 1, d), mod_map),
        ],
        out_specs=pl.BlockSpec((tm, d), lambda i, j: (i, 0)),
        out_shape=jax.ShapeDtypeStruct((r, d), F32),
        scratch_shapes=[pltpu.VMEM((tm, d), BF16), pltpu.VMEM((tm, d), F32)],
        compiler_params=_cparams("parallel", "arbitrary"),
        name="ffn_dense",
    )(x, g1.reshape(1, d), shift, scale, wg.astype(BF16), wu.astype(BF16), wd.astype(BF16),
      g2.reshape(1, d), gate)


def _ffn_routed_kernel(te_ref, nu_ref, x_ref, wg_ref, wu_ref, wd_ref, o_ref, h_scr, acc_scr):
    i = pl.program_id(0)
    j = pl.program_id(1)
    last = pl.num_programs(1) - 1
    used = i < nu_ref[0]

    @pl.when(used)
    def _():
        @pl.when(j == 0)
        def _():
            h_scr[...] = x_ref[...].astype(BF16)

        _swiglu_step(h_scr, wg_ref.at[0], wu_ref.at[0], wd_ref.at[0], acc_scr, j)

        @pl.when(j == last)
        def _():
            o_ref[...] = acc_scr[...]

    @pl.when(jnp.logical_and(jnp.logical_not(used), j == last))
    def _():
        o_ref[...] = jnp.zeros_like(o_ref)


def ffn_routed(xs, tile_expert, n_used, wg, wu, wd, tm, tf=512):
    p, d = xs.shape
    ff = wg.shape[2]
    assert ff % tf == 0 and p % tm == 0
    grid_spec = pltpu.PrefetchScalarGridSpec(
        num_scalar_prefetch=2,
        grid=(p // tm, ff // tf),
        in_specs=[
            pl.BlockSpec((tm, d), lambda i, j, te, nu: (i, 0)),
            pl.BlockSpec((1, d, tf), lambda i, j, te, nu: (te[i], 0, j)),
            pl.BlockSpec((1, d, tf), lambda i, j, te, nu: (te[i], 0, j)),
            pl.BlockSpec((1, tf, d), lambda i, j, te, nu: (te[i], j, 0)),
        ],
        out_specs=pl.BlockSpec((tm, d), lambda i, j, te, nu: (i, 0)),
        scratch_shapes=[pltpu.VMEM((tm, d), BF16), pltpu.VMEM((tm, d), F32)],
    )
    return pl.pallas_call(
        _ffn_routed_kernel,
        grid_spec=grid_spec,
        out_shape=jax.ShapeDtypeStruct((p, d), F32),
        compiler_params=_cparams("arbitrary", "arbitrary"),
        name="ffn_routed",
    )(tile_expert, n_used, xs, wg.astype(BF16), wu.astype(BF16), wd.astype(BF16))


ROUTE_IDX0 = N_EXPERTS
ROUTE_P0 = N_EXPERTS + TOP_K


def _router_kernel(x_ref, g_ref, sh_ref, sc_ref, wr_ref, h_ref, r_ref):
    h = _rms(x_ref[...], g_ref[...]) * (1.0 + sc_ref[0]) + sh_ref[0]
    h_ref[...] = h
    logits = _hdot(h, wr_ref[...])
    lane = lax.broadcasted_iota(jnp.int32, logits.shape, 1)
    neg = jnp.float32(-jnp.inf)
    big = jnp.int32(LANES)
    lg = jnp.where(lane < N_EXPERTS, logits, neg)
    m1 = jnp.max(lg, axis=-1, keepdims=True)
    i1 = jnp.min(jnp.where(lg == m1, lane, big), axis=-1, keepdims=True)
    lg2 = jnp.where(lane == i1, neg, lg)
    m2 = jnp.max(lg2, axis=-1, keepdims=True)
    i2 = jnp.min(jnp.where(lg2 == m2, lane, big), axis=-1, keepdims=True)
    e2 = jnp.exp(m2 - m1)
    p1 = 1.0 / (1.0 + e2)
    p2 = e2 / (1.0 + e2)
    out = jnp.where(lane == ROUTE_IDX0, i1.astype(F32), 0.0)
    out = jnp.where(lane == ROUTE_IDX0 + 1, i2.astype(F32), out)
    out = jnp.where(lane == ROUTE_P0, p1, out)
    out = jnp.where(lane == ROUTE_P0 + 1, p2, out)
    r_ref[...] = out


def router(x, g, shift, scale, w_router, rows_per_mod, tm=512):
    r, d = x.shape
    tm = _row_tile(min(r, rows_per_mod), tm)
    wr = jnp.zeros((d, LANES), F32).at[:, :N_EXPERTS].set(w_router)
    mod_map = lambda i: ((i * tm) // rows_per_mod, 0, 0)
    return pl.pallas_call(
        _router_kernel,
        grid=(r // tm,),
        in_specs=[
            pl.BlockSpec((tm, d), lambda i: (i, 0)),
            pl.BlockSpec((1, d), lambda i: (0, 0)),
            pl.BlockSpec((1, 1, d), mod_map),
            pl.BlockSpec((1, 1, d), mod_map),
            pl.BlockSpec((d, LANES), lambda i: (0, 0)),
        ],
        out_specs=[pl.BlockSpec((tm, d), lambda i: (i, 0)),
                   pl.BlockSpec((tm, LANES), lambda i: (i, 0))],
        out_shape=[jax.ShapeDtypeStruct((r, d), F32), jax.ShapeDtypeStruct((r, LANES), F32)],
        compiler_params=_cparams("parallel"),
        name="router",
    )(x, g.reshape(1, d), shift, scale, wr)


def _gather_kernel(idx_ref, src_ref, o_ref, sem, *, rows):
    base = pl.program_id(0) * rows

    def row_copy(r):
        return pltpu.make_async_copy(src_ref.at[pl.ds(idx_ref[r], 1)], o_ref.at[pl.ds(base + r, 1)], sem)

    def issue(r, carry):
        row_copy(r).start()
        return carry

    def drain(r, carry):
        row_copy(r).wait()
        return carry

    lax.fori_loop(0, rows, issue, 0)
    lax.fori_loop(0, rows, drain, 0)


def gather_rows(src, idx, rows_per_step=1024):
    n = idx.shape[0]
    d = src.shape[1]
    rows = _row_tile(n, rows_per_step)
    return pl.pallas_call(
        functools.partial(_gather_kernel, rows=rows),
        grid=(n // rows,),
        in_specs=[
            pl.BlockSpec((rows,), lambda i: (i,), memory_space=pltpu.SMEM),
            pl.BlockSpec(memory_space=pl.ANY),
        ],
        out_specs=pl.BlockSpec(memory_space=pl.ANY),
        out_shape=jax.ShapeDtypeStruct((n, d), src.dtype),
        scratch_shapes=[pltpu.SemaphoreType.DMA(())],
        compiler_params=_cparams("arbitrary"),
        name="gather_rows",
    )(idx, src)


def _combine_kernel(y0_ref, y1_ref, r_ref, g_ref, gate_ref, x_ref, o_ref):
    rt = r_ref[...]
    p0 = rt[:, ROUTE_P0:ROUTE_P0 + 1]
    p1 = rt[:, ROUTE_P0 + 1:ROUTE_P0 + 2]
    y = p0 * y0_ref[...] + p1 * y1_ref[...]
    o_ref[...] = x_ref[...] + gate_ref[0] * _rms(y, g_ref[...])


def combine(yg, route, g, gate, x, rows_per_mod, tm=512):
    r, d = x.shape
    tm = _row_tile(min(r, rows_per_mod), tm)
    nt = r // tm
    mod_map = lambda i: ((i * tm) // rows_per_mod, 0, 0)
    return pl.pallas_call(
        _combine_kernel,
        grid=(nt,),
        in_specs=[
            pl.BlockSpec((tm, d), lambda i: (i, 0)),
            pl.BlockSpec((tm, d), lambda i: (i + nt, 0)),
            pl.BlockSpec((tm, LANES), lambda i: (i, 0)),
            pl.BlockSpec((1, d), lambda i: (0, 0)),
            pl.BlockSpec((1, 1, d), mod_map),
            pl.BlockSpec((tm, d), lambda i: (i, 0)),
        ],
        out_specs=pl.BlockSpec((tm, d), lambda i: (i, 0)),
        out_shape=jax.ShapeDtypeStruct((r, d), F32),
        compiler_params=_cparams("parallel"),
        name="moe_combine",
    )(yg, yg, route, g.reshape(1, d), gate, x)


def moe_block(x, g1, shift, scale, w_router, wg, wu, wd, g2, gate, rows_per_mod, tm=512):
    n, d = x.shape
    h, route = router(x, g1, shift, scale, w_router, rows_per_mod)
    eidx = route[:, ROUTE_IDX0:ROUTE_IDX0 + TOP_K].astype(jnp.int32)
    flat_e = eidx.T.reshape(-1)
    onehot = (flat_e[:, None] == jnp.arange(N_EXPERTS, dtype=jnp.int32)[None, :]).astype(jnp.int32)
    csum = jnp.cumsum(onehot, axis=0)
    counts = csum[-1]
    rank = jnp.take_along_axis(csum, flat_e[:, None], axis=1)[:, 0] - 1
    tiles_per_e = (counts + tm - 1) // tm
    tile_end = jnp.cumsum(tiles_per_e)
    tile_start = tile_end - tiles_per_e
    slot = tile_start[flat_e] * tm + rank
    n_tiles = (TOP_K * n) // tm + N_EXPERTS
    tok = jnp.tile(jnp.arange(n, dtype=jnp.int32), TOP_K)
    src = jnp.zeros((n_tiles * tm,), jnp.int32).at[slot].set(tok)
    tile_expert = jnp.minimum(
        jnp.searchsorted(tile_end, jnp.arange(n_tiles, dtype=jnp.int32), side="right"),
        N_EXPERTS - 1).astype(jnp.int32)
    n_used = tile_end[-1:].astype(jnp.int32)

    xs = gather_rows(h, src)
    ys = ffn_routed(xs, tile_expert, n_used, wg, wu, wd, tm)
    yg = gather_rows(ys, slot.astype(jnp.int32))
    return combine(yg, route, g2, gate, x, rows_per_mod)


def _gmlp_kernel(u_ref, v_ref, gv_ref, ws_ref, bs_ref, o_ref, vn_scr):
    v = v_ref[...].astype(F32)
    mu = jnp.mean(v, axis=-1, keepdims=True)
    vc = v - mu
    vn = vc * lax.rsqrt(jnp.mean(vc * vc, axis=-1, keepdims=True) + EPS) * gv_ref[...]
    vn_scr[...] = vn.astype(BF16)
    tm, width = vn_scr.shape
    gw = width // GMLP_GROUPS
    for n in range(tm // GMLP_CHUNK):
        rs = slice(n * GMLP_CHUNK, (n + 1) * GMLP_CHUNK)
        for g in range(GMLP_GROUPS):
            cs = slice(g * gw, (g + 1) * gw)
            m = _bdot(ws_ref[g], vn_scr[rs, cs]) + bs_ref[g]
            o_ref[rs, cs] = (u_ref[rs, cs].astype(F32) * m).astype(BF16)


def gmlp_spatial(hw, g_v, w_s, b_s, tm=512):
    r, w2 = hw.shape
    width = w2 // 2
    gw = width // GMLP_GROUPS
    tm = _row_tile(r, tm)
    bsb = jnp.broadcast_to(b_s[:, :, None], (GMLP_GROUPS, GMLP_CHUNK, gw)).astype(F32)
    return pl.pallas_call(
        _gmlp_kernel,
        grid=(r // tm,),
        in_specs=[
            pl.BlockSpec((tm, width), lambda i: (i, 0)),
            pl.BlockSpec((tm, width), lambda i: (i, 1)),
            pl.BlockSpec((1, width), lambda i: (0, 0)),
            pl.BlockSpec((GMLP_GROUPS, GMLP_CHUNK, GMLP_CHUNK), lambda i: (0, 0, 0)),
            pl.BlockSpec((GMLP_GROUPS, GMLP_CHUNK, gw), lambda i: (0, 0, 0)),
        ],
        out_specs=pl.BlockSpec((tm, width), lambda i: (i, 0)),
        out_shape=jax.ShapeDtypeStruct((r, width), BF16),
        scratch_shapes=[pltpu.VMEM((tm, width), BF16)],
        compiler_params=_cparams("parallel"),
        name="gmlp_spatial",
    )(hw, hw, g_v.reshape(1, width), w_s.astype(BF16), bsb)


def _halo_specs(tm, width, col_block, tile_of, n_row_blocks):
    per = tm // HALO
    cur = pl.BlockSpec((tm, width), lambda i: (tile_of(i), col_block))
    prev = pl.BlockSpec((HALO, width), lambda i: (jnp.maximum(tile_of(i) * per - 1, 0), col_block))
    nxt = pl.BlockSpec((HALO, width),
                       lambda i: (jnp.minimum((tile_of(i) + 1) * per, n_row_blocks - 1), col_block))
    return [cur, prev, nxt]


def _fill_ext(ext, cur_ref, prev_ref, next_ref, tile, tm, seq_len):
    first = (tile * tm) % seq_len == 0
    last = ((tile + 1) * tm) % seq_len == 0
    ext[0:HALO, :] = jnp.where(first, 0.0, prev_ref[...])
    ext[HALO:HALO + tm, :] = cur_ref[...]
    ext[HALO + tm:HALO + tm + HALO, :] = jnp.where(last, 0.0, next_ref[...])


def _lru_kernel(xb_ref, xp_ref, xn_ref, cw_ref, cb_ref, wax_ref, ba_ref, bx_ref, lam_ref, h0_ref,
                o_ref, ext, a_scr, b_scr, carry, *, tm, seq_len, n_tiles, reverse):
    i = pl.program_id(0)
    tile = (n_tiles - 1 - i) if reverse else i
    _fill_ext(ext, xb_ref, xp_ref, xn_ref, tile, tm, seq_len)
    kk = cw_ref.shape[0]
    left = kk // 2
    xc = cb_ref[...] + cw_ref[0:1, :] * ext[pl.ds(HALO - left, tm), :]
    for k in range(1, kk):
        xc = xc + cw_ref[k:k + 1, :] * ext[pl.ds(HALO - left + k, tm), :]

    width = xc.shape[1]
    hd = width // LRU_HEADS
    lam = lam_ref[...]
    sp = jnp.maximum(-lam, 0.0) + jnp.log(1.0 + jnp.exp(-jnp.abs(lam)))
    for hh in range(LRU_HEADS):
        cs = slice(hh * hd, (hh + 1) * hd)
        xh = xc[:, cs]
        pre = _bdot(xh.astype(BF16), wax_ref[hh])
        rg = _sigmoid(pre[:, :hd] + ba_ref[:, cs])
        ig = _sigmoid(pre[:, hd:] + bx_ref[:, cs])
        a = jnp.exp(-LRU_C * rg * sp[:, cs])
        a_scr[:, cs] = a
        b_scr[:, cs] = jnp.sqrt(1.0 - a * a) * (ig * xh)

    first = (tile * tm) % seq_len == 0
    last = ((tile + 1) * tm) % seq_len == 0

    @pl.when(last if reverse else first)
    def _():
        carry[...] = jnp.broadcast_to(h0_ref[0], carry.shape)

    row = lax.broadcasted_iota(jnp.int32, (SUBLANES, width), 0)
    nblk = tm // SUBLANES

    def body(k, c):
        blk = (nblk - 1 - k) if reverse else k
        r0 = pl.multiple_of(blk * SUBLANES, SUBLANES)
        a = a_scr[pl.ds(r0, SUBLANES), :]
        b = b_scr[pl.ds(r0, SUBLANES), :]
        for s in (1, 2, 4):
            shift = (SUBLANES - s) if reverse else s
            a_sh = pltpu.roll(a, shift, 0)
            b_sh = pltpu.roll(b, shift, 0)
            m = (row < SUBLANES - s) if reverse else (row >= s)
            b = jnp.where(m, a * b_sh + b, b)
            a = jnp.where(m, a * a_sh, a)
        h = a * c + b
        o_ref[pl.ds(r0, SUBLANES), :] = h
        edge = h[0:1, :] if reverse else h[SUBLANES - 1:SUBLANES, :]
        return jnp.broadcast_to(edge, c.shape)

    carry[...] = lax.fori_loop(0, nblk, body, carry[...])


def lru_scan(z, col_block, width, conv_w, conv_b, w_a, b_a, w_x, b_x, lam, h0, seq_len, reverse, tm=512):
    r = z.shape[0]
    tm = _row_tile(seq_len, tm)
    n_tiles = r // tm
    nb = r // seq_len
    tile_of = (lambda i: n_tiles - 1 - i) if reverse else (lambda i: i)
    wax = jnp.concatenate([w_a, w_x], axis=-1).astype(BF16)
    kk = conv_w.shape[0]
    hd = width // LRU_HEADS
    const2 = lambda i: (0, 0)
    return pl.pallas_call(
        functools.partial(_lru_kernel, tm=tm, seq_len=seq_len, n_tiles=n_tiles, reverse=reverse),
        grid=(n_tiles,),
        in_specs=_halo_specs(tm, width, col_block, tile_of, r // HALO) + [
            pl.BlockSpec((kk, width), const2),
            pl.BlockSpec((1, width), const2),
            pl.BlockSpec((LRU_HEADS, hd, 2 * hd), lambda i: (0, 0, 0)),
            pl.BlockSpec((1, width), const2),
            pl.BlockSpec((1, width), const2),
            pl.BlockSpec((1, width), const2),
            pl.BlockSpec((1, 1, width), lambda i: ((tile_of(i) * tm) // seq_len, 0, 0)),
        ],
        out_specs=pl.BlockSpec((tm, width), lambda i: (tile_of(i), 0)),
        out_shape=jax.ShapeDtypeStruct((r, width), F32),
        scratch_shapes=[
            pltpu.VMEM((tm + 2 * HALO, width), F32),
            pltpu.VMEM((tm, width), F32),
            pltpu.VMEM((tm, width), F32),
            pltpu.VMEM((SUBLANES, width), F32),
        ],
        compiler_params=_cparams("arbitrary"),
        name="lru_scan_bwd" if reverse else "lru_scan_fwd",
    )(z, z, z, conv_w, conv_b.reshape(1, width), wax, b_a.reshape(1, width), b_x.reshape(1, width),
      lam.reshape(1, width), h0.reshape(nb, 1, width))


def _lru_out_prologue(gate, hf, hb):
    return (_gelu(gate) * (hf + hb)).astype(BF16)


def _pool_kernel(p_ref, pp_ref, pn_ref, wg_ref, bg_ref, sc_ref, o_ref, ext, *, tm, seq_len):
    i = pl.program_id(0)
    _fill_ext(ext, p_ref, pp_ref, pn_ref, i, tm, seq_len)
    width = o_ref.shape[1]
    gw = width // len(POOL_WINDOWS)
    t = ((i * tm) % seq_len + lax.broadcasted_iota(jnp.int32, (tm, 1), 0))
    for g, win in enumerate(POOL_WINDOWS):
        half = win // 2
        cs = slice(g * gw, (g + 1) * gw)
        s = ext[pl.ds(HALO - half, tm), cs]
        for k in range(1 - half, half):
            s = s + ext[pl.ds(HALO + k, tm), cs]
        cnt = (jnp.minimum(t + half, seq_len) - jnp.maximum(t - half, 0)).astype(F32)
        q = s / cnt - ext[pl.ds(HALO, tm), cs]
        y = _bdot(q.astype(BF16), wg_ref[g]) + bg_ref[:, cs]
        o_ref[:, cs] = (y * sc_ref[:, cs]).astype(BF16)


def pool_mix(p, w_g, b_g, scale, seq_len, tm=512):
    r, width = p.shape
    assert max(POOL_WINDOWS) // 2 <= HALO
    tm = _row_tile(seq_len, tm)
    ng, gw, _ = w_g.shape
    const2 = lambda i: (0, 0)
    return pl.pallas_call(
        functools.partial(_pool_kernel, tm=tm, seq_len=seq_len),
        grid=(r // tm,),
        in_specs=_halo_specs(tm, width, 0, lambda i: i, r // HALO) + [
            pl.BlockSpec((ng, gw, gw), lambda i: (0, 0, 0)),
            pl.BlockSpec((1, width), const2),
            pl.BlockSpec((1, width), const2),
        ],
        out_specs=pl.BlockSpec((tm, width), lambda i: (i, 0)),
        out_shape=jax.ShapeDtypeStruct((r, width), BF16),
        scratch_shapes=[pltpu.VMEM((tm + 2 * HALO, width), F32)],
        compiler_params=_cparams("parallel"),
        name="pool_mix",
    )(p, p, p, w_g.astype(BF16), b_g.reshape(1, width), scale.reshape(1, width))


def _hyconv_kernel(z_ref, zp_ref, zn_ref, cw_ref, cb_ref, o_ref, ext, *, tm, seq_len):
    i = pl.program_id(0)
    _fill_ext(ext, z_ref, zp_ref, zn_ref, i, tm, seq_len)
    kk = cw_ref.shape[0]
    left = kk // 2
    y = cb_ref[...] + cw_ref[0:1, :] * ext[pl.ds(HALO - left, tm), :]
    for k in range(1, kk):
        y = y + cw_ref[k:k + 1, :] * ext[pl.ds(HALO - left + k, tm), :]
    o_ref[0] = y.astype(BF16)


def hyena_short_conv(z, conv_w, conv_b, width, seq_len, tm=512):
    r = z.shape[0]
    nsplit = z.shape[1] // width
    tm = _row_tile(seq_len, tm)
    kk = conv_w.shape[0]
    n_row_blocks = r // HALO
    per = tm // HALO
    return pl.pallas_call(
        functools.partial(_hyconv_kernel, tm=tm, seq_len=seq_len),
        grid=(r // tm, nsplit),
        in_specs=[
            pl.BlockSpec((tm, width), lambda i, j: (i, j)),
            pl.BlockSpec((HALO, width), lambda i, j: (jnp.maximum(i * per - 1, 0), j)),
            pl.BlockSpec((HALO, width), lambda i, j: (jnp.minimum((i + 1) * per, n_row_blocks - 1), j)),
            pl.BlockSpec((kk, width), lambda i, j: (0, j)),
            pl.BlockSpec((1, width), lambda i, j: (0, j)),
        ],
        out_specs=pl.BlockSpec((1, tm, width), lambda i, j: (j, i, 0)),
        out_shape=jax.ShapeDtypeStruct((nsplit, r, width), BF16),
        scratch_shapes=[pltpu.VMEM((tm + 2 * HALO, width), F32)],
        compiler_params=_cparams("parallel", "arbitrary"),
        name="hyena_short_conv",
    )(z, z, z, conv_w, conv_b.reshape(1, -1))


def _hyfilter_kernel(z_ref, w1_ref, b1_ref, w2_ref, b2_ref, w3_ref, b3_ref, fr_ref, wo_ref, dl_ref,
                     o_ref, *, n_out):
    z = z_ref[...]
    h = jnp.sin(fr_ref[0:1, :] * (_hdot(z, w1_ref[...]) + b1_ref[...]))
    h = jnp.sin(fr_ref[1:2, :] * (_hdot(h, w2_ref[...]) + b2_ref[...]))
    h = jnp.sin(fr_ref[2:3, :] * (_hdot(h, w3_ref[...]) + b3_ref[...]))
    window = jnp.exp(-z[:, 0:1] * dl_ref[...])
    width = dl_ref.shape[1]
    for q in range(n_out):
        cs = slice(q * width, (q + 1) * width)
        o_ref[:, cs] = _hdot(h, wo_ref[:, cs]) * window


def hyena_filters(seq_len, width, f_w1, f_b1, f_w2, f_b2, f_w3, f_b3, f_freq, f_wout, tm=512):
    t = jnp.linspace(0.0, 1.0, seq_len, dtype=F32)[:, None]
    w = 2.0 * math.pi * jnp.arange(seq_len, dtype=F32)[:, None] / seq_len
    bands = jnp.linspace(1e-4, HYENA_BANDS - 1, HYENA_BANDS, dtype=F32)[None, :]
    z = jnp.concatenate([t, jnp.cos(bands * w), jnp.sin(-bands * w)], axis=-1)
    emb = z.shape[1]
    hid = f_w1.shape[1]
    zp = jnp.zeros((seq_len, LANES), F32).at[:, :emb].set(z)
    w1p = jnp.zeros((LANES, hid), F32).at[:emb].set(f_w1)
    max_decay = math.log(HYENA_DECAY_TARGET) / HYENA_FAST_DECAY
    min_decay = math.log(HYENA_DECAY_TARGET) / HYENA_SLOW_DECAY
    deltas = jnp.abs(jnp.linspace(min_decay, max_decay, width, dtype=F32)).reshape(1, width)
    n_tot = f_wout.shape[1]
    tm = _row_tile(seq_len, tm)
    c2 = lambda i: (0, 0)
    return pl.pallas_call(
        functools.partial(_hyfilter_kernel, n_out=n_tot // width),
        grid=(seq_len // tm,),
        in_specs=[
            pl.BlockSpec((tm, LANES), lambda i: (i, 0)),
            pl.BlockSpec((LANES, hid), c2), pl.BlockSpec((1, hid), c2),
            pl.BlockSpec((hid, hid), c2), pl.BlockSpec((1, hid), c2),
            pl.BlockSpec((hid, hid), c2), pl.BlockSpec((1, hid), c2),
            pl.BlockSpec((3, hid), c2),
            pl.BlockSpec((hid, n_tot), c2),
            pl.BlockSpec((1, width), c2),
        ],
        out_specs=pl.BlockSpec((tm, n_tot), lambda i: (i, 0)),
        out_shape=jax.ShapeDtypeStruct((seq_len, n_tot), F32),
        compiler_params=_cparams("parallel"),
        name="hyena_filters",
    )(zp, w1p, f_b1.reshape(1, hid), f_w2, f_b2.reshape(1, hid), f_w3, f_b3.reshape(1, hid),
      f_freq, f_wout, deltas)


def _dft_tables(seq_len):
    n = 2 * seq_len
    p_ = FFT_P
    n1 = n // p_
    t1n = n1 // 2
    f1 = np.arange(n1)[:, None]
    t1 = np.arange(n1)[None, :]
    ang1 = 2.0 * np.pi * ((f1 * t1) % n1) / n1
    c1, s1 = np.cos(ang1), np.sin(ang1)
    w1 = np.zeros((2 * n1, 2 * t1n))
    w1[0::2, :t1n] = c1[:, :t1n]
    w1[0::2, t1n:] = s1[:, :t1n]
    w1[1::2, :t1n] = -s1[:, :t1n]
    w1[1::2, t1n:] = c1[:, :t1n]
    w1k = np.zeros((2 * n1, n1))
    w1k[0::2] = c1
    w1k[1::2] = -s1
    w1i = np.zeros((2 * t1n, 2 * n1))
    ct, st = c1.T[:t1n] / n, s1.T[:t1n] / n
    w1i[:t1n, 0::2] = ct
    w1i[:t1n, 1::2] = -st
    w1i[t1n:, 0::2] = st
    w1i[t1n:, 1::2] = ct
    f2 = np.arange(p_)[:, None]
    pp = np.arange(p_)[None, :]
    ang2 = 2.0 * np.pi * ((f2 * pp) % p_) / p_
    c2, s2 = np.cos(ang2), np.sin(ang2)
    fb = np.block([[c2, s2], [-s2, c2]])
    fbi = np.block([[c2, -s2], [s2, c2]])
    angt = 2.0 * np.pi * ((np.arange(n1)[:, None] * np.arange(p_)[None, :]) % n) / n
    lane_bcast = lambda a: jnp.broadcast_to(jnp.asarray(a, dtype=F32)[:, :, None], (n1, p_, LANES))
    as_bf = lambda a: jnp.asarray(a, dtype=F32).astype(BF16)
    return dict(n1=n1, t1n=t1n, w1=as_bf(w1), w1k=as_bf(w1k), w1i=as_bf(w1i), fb=as_bf(fb), fbi=as_bf(fbi),
                twc=lane_bcast(np.cos(angt)), tws=lane_bcast(-np.sin(angt)))


def _mm_kernel(w_ref, x_ref, o_ref):
    o_ref[...] = _bdot(w_ref[...], x_ref[...]).astype(o_ref.dtype)


def left_matmul(w, x, out_dtype, tn=4096):
    m, k = w.shape
    n = x.shape[1]
    tn = min(tn, n)
    assert n % tn == 0
    return pl.pallas_call(
        _mm_kernel,
        grid=(n // tn,),
        in_specs=[pl.BlockSpec((m, k), lambda j: (0, 0)), pl.BlockSpec((k, tn), lambda j: (0, j))],
        out_specs=pl.BlockSpec((m, tn), lambda j: (0, j)),
        out_shape=jax.ShapeDtypeStruct((m, n), out_dtype),
        compiler_params=_cparams("parallel"),
        name="dft_stage1",
    )(w, x)


def _mm_gate_kernel(w_ref, x_ref, g_ref, v_ref, b_ref, o_ref):
    y = _bdot(w_ref[...], x_ref[...])
    v = v_ref[...].astype(F32)
    o_ref[...] = (g_ref[...].astype(F32) * (y + b_ref[...] * v)).astype(o_ref.dtype)


def left_matmul_gate(w, x, gate, v, bias_row, tn=4096):
    m, k = w.shape
    n = x.shape[1]
    tn = min(tn, n)
    assert n % tn == 0
    col = lambda j: (0, j)
    return pl.pallas_call(
        _mm_gate_kernel,
        grid=(n // tn,),
        in_specs=[pl.BlockSpec((m, k), lambda j: (0, 0)), pl.BlockSpec((k, tn), col),
                  pl.BlockSpec((m, tn), col), pl.BlockSpec((m, tn), col), pl.BlockSpec((1, tn), col)],
        out_specs=pl.BlockSpec((m, tn), col),
        out_shape=jax.ShapeDtypeStruct((m, n), BF16),
        compiler_params=_cparams("parallel"),
        name="dft_inverse_stage1_gate",
    )(w, x, gate, v, bias_row)


def _twiddle(re, im, tc, ts, reps):
    tc = jnp.tile(tc, (1, reps))
    ts = jnp.tile(ts, (1, reps))
    return re * tc - im * ts, re * ts + im * tc


def _spec_fwd_kernel(a_ref, tc_ref, ts_ref, fb_ref, o_ref):
    reps = a_ref.shape[-1] // LANES
    tr, ti = _twiddle(a_ref[0, 0].astype(F32), a_ref[0, 1].astype(F32), tc_ref[0], ts_ref[0], reps)
    rhs = jnp.concatenate([tr, ti], axis=0).astype(BF16)
    o_ref[0] = _bdot(fb_ref[...], rhs)


def _spec_conv_kernel(a_ref, tc_ref, ts_ref, fb_ref, fbi_ref, k_ref, o_ref):
    p_ = FFT_P
    reps = a_ref.shape[-1] // LANES
    tc, ts = tc_ref[0], ts_ref[0]
    tr, ti = _twiddle(a_ref[0, 0].astype(F32), a_ref[0, 1].astype(F32), tc, ts, reps)
    x = _bdot(fb_ref[...], jnp.concatenate([tr, ti], axis=0).astype(BF16))
    xr, xi = x[:p_], x[p_:]
    kr, ki = k_ref[0, :p_], k_ref[0, p_:]
    yr = xr * kr - xi * ki
    yi = xr * ki + xi * kr
    bv = _bdot(fbi_ref[...], jnp.concatenate([yr, yi], axis=0).astype(BF16))
    orr, oi = _twiddle(bv[:p_], bv[p_:], tc, -ts, reps)
    o_ref[0, 0] = orr.astype(o_ref.dtype)
    o_ref[0, 1] = oi.astype(o_ref.dtype)


def spectrum_forward(a4, tab, ct=1024):
    n1, _, p_, c = a4.shape
    ct = min(ct, c)
    return pl.pallas_call(
        _spec_fwd_kernel,
        grid=(c // ct, n1),
        in_specs=[
            pl.BlockSpec((1, 2, p_, ct), lambda j, f: (f, 0, 0, j)),
            pl.BlockSpec((1, p_, LANES), lambda j, f: (f, 0, 0)),
            pl.BlockSpec((1, p_, LANES), lambda j, f: (f, 0, 0)),
            pl.BlockSpec((2 * p_, 2 * p_), lambda j, f: (0, 0)),
        ],
        out_specs=pl.BlockSpec((1, 2 * p_, ct), lambda j, f: (f, 0, j)),
        out_shape=jax.ShapeDtypeStruct((n1, 2 * p_, c), F32),
        compiler_params=_cparams("parallel", "parallel"),
        name="dft_filter_stage2",
    )(a4, tab["twc"], tab["tws"], tab["fb"])


def spectrum_conv(a4, kspec, k_col_block, tab):
    n1, _, p_, c = a4.shape
    return pl.pallas_call(
        _spec_conv_kernel,
        grid=(n1,),
        in_specs=[
            pl.BlockSpec((1, 2, p_, c), lambda f: (f, 0, 0, 0)),
            pl.BlockSpec((1, p_, LANES), lambda f: (f, 0, 0)),
            pl.BlockSpec((1, p_, LANES), lambda f: (f, 0, 0)),
            pl.BlockSpec((2 * p_, 2 * p_), lambda f: (0, 0)),
            pl.BlockSpec((2 * p_, 2 * p_), lambda f: (0, 0)),
            pl.BlockSpec((1, 2 * p_, c), lambda f: (f, 0, k_col_block)),
        ],
        out_specs=pl.BlockSpec((1, 2, p_, c), lambda f: (f, 0, 0, 0)),
        out_shape=jax.ShapeDtypeStruct(a4.shape, BF16),
        compiler_params=_cparams("parallel"),
        name="dft_stage2_conv",
    )(a4, tab["twc"], tab["tws"], tab["fb"], tab["fbi"], kspec)


def hyena_long_convs(xv, filt, bias, batch, seq_len):
    assert batch == 2, "the two sequences of the batch are packed as one complex sequence"
    c = xv.shape[-1]
    tab = _dft_tables(seq_len)
    n1, t1n, p_ = tab["n1"], tab["t1n"], FFT_P
    order = bias.shape[0]
    f4 = filt.reshape(seq_len, order, 2, c)
    kern = jnp.concatenate([f4[:, :, 0], jnp.zeros((1, order, c), F32), f4[:0:-1, :, 1]], axis=0)
    kern = kern.reshape(n1, p_ * order * c).astype(BF16)
    ak = left_matmul(tab["w1k"], kern, BF16)
    kspec = spectrum_forward(ak.reshape(n1, 2, p_, order * c), tab)
    rl = lambda a: a.reshape(batch * t1n, p_ * c)
    y = rl(xv[2])
    for o in range(order):
        a = left_matmul(tab["w1"], y, BF16)
        b4 = spectrum_conv(a.reshape(n1, 2, p_, c), kspec, o, tab)
        brow = jnp.tile(bias[o].astype(F32), p_).reshape(1, p_ * c)
        y = left_matmul_gate(tab["w1i"], b4.reshape(2 * n1, p_ * c), rl(xv[o]), y, brow)
    return y.reshape(batch * seq_len, c)


def kernel(x, c, ctx, c_ctx, ada_w, ada_b, norm_g, gmlp_w_in, gmlp_g_v, gmlp_w_s, gmlp_b_s, gmlp_w_out, lru_w_in, lru_conv_w, lru_conv_b, lru_w_a, lru_b_a, lru_w_x, lru_b_x, lru_lam, lru_w_out, hyena_w_in, hyena_conv_w, hyena_conv_b, hyena_f_w1, hyena_f_b1, hyena_f_w2, hyena_f_b2, hyena_f_w3, hyena_f_b3, hyena_f_freq, hyena_f_wout, hyena_bias, hyena_w_out, pool_w_in, pool_w_g, pool_b_g, pool_scale, pool_w_out, ffn_w_gate, ffn_w_up, ffn_w_down, moe_w_router, moe_w_gate, moe_w_up, moe_w_down):
    B, L, D = x.shape
    Lc = ctx.shape[1]
    depth = ada_w.shape[0]
    n_mixers = 4
    assert B + 1 <= SUBLANES

    cc = jnp.zeros((SUBLANES, D), F32).at[:B].set(c).at[B].set(c_ctx)
    ada = ada_all(cc, ada_w, ada_b).reshape(depth, SUBLANES, 6, D)

    xs = add_pos(x)
    xc = ctx.reshape(B * Lc, D)
    last_ctx = max([i for i in range(depth) if i % n_mixers == 1], default=-1)

    for i in range(depth):
        kind, j = i % n_mixers, i // n_mixers
        update_ctx = i < last_ctx
        read_ctx = i <= last_ctx
        lat = [ada[i, :B, q].reshape(B, 1, D) for q in range(6)]
        cx = [jnp.broadcast_to(ada[i, B:B + 1, q].reshape(1, 1, D), (B, 1, D)) for q in range(6)]
        g = norm_g[i]
        streams = [(xs, lat, L)]
        if read_ctx:
            streams.append((xc, cx, Lc))

        if kind == 0:
            outs = []
            for (s, m, sl) in streams[:1 + int(update_ctx)]:
                hw = norm_mod_matmul(s, g[0], m[0], m[1], gmlp_w_in[j], sl, BF16, act="gelu")
                a = gmlp_spatial(hw, gmlp_g_v[j], gmlp_w_s[j], gmlp_b_s[j])
                outs.append(matmul_norm_res([(a, a.shape[1], 0)], _identity_bf16, gmlp_w_out[j],
                                            g[1], m[2], s, sl))
        elif kind == 1:
            W = lru_w_in.shape[2] // 2
            sc_args = lambda d: (lru_conv_w[j], lru_conv_b[j], lru_w_a[j, d], lru_b_a[j, d],
                                 lru_w_x[j, d], lru_b_x[j, d], lru_lam[j, d])
            zc = norm_mod_matmul(xc, g[0], cx[0], cx[1], lru_w_in[j], Lc, F32)
            zero = jnp.zeros((B, W), F32)
            hf_c = lru_scan(zc, 1, W, *sc_args(0), zero, Lc, False)
            hb_c = lru_scan(zc, 1, W, *sc_args(1), zero, Lc, True)
            zl = norm_mod_matmul(xs, g[0], lat[0], lat[1], lru_w_in[j], L, F32)
            hf = lru_scan(zl, 1, W, *sc_args(0), hf_c.reshape(B, Lc, W)[:, -1], L, False)
            hb = lru_scan(zl, 1, W, *sc_args(1), hb_c.reshape(B, Lc, W)[:, 0], L, True)
            outs = [matmul_norm_res([(zl, W, 0), (hf, W, 0), (hb, W, 0)], _lru_out_prologue,
                                    lru_w_out[j], g[1], lat[2], xs, L)]
            if update_ctx:
                outs.append(matmul_norm_res([(zc, W, 0), (hf_c, W, 0), (hb_c, W, 0)], _lru_out_prologue,
                                            lru_w_out[j], g[1], cx[2], xc, Lc))
        elif kind == 2:
            W = hyena_w_out.shape[1]
            filt = hyena_filters(L, W, hyena_f_w1[j], hyena_f_b1[j], hyena_f_w2[j], hyena_f_b2[j],
                                 hyena_f_w3[j], hyena_f_b3[j], hyena_f_freq[j], hyena_f_wout[j])
            outs = []
            for (s, m, sl) in streams[:1 + int(update_ctx)]:
                nb = s.shape[0] // sl
                z = norm_mod_matmul(s, g[0], m[0], m[1], hyena_w_in[j], sl, F32, tn=1024)
                xv = hyena_short_conv(z, hyena_conv_w[j], hyena_conv_b[j], W, sl)
                fl = filt if sl == L else hyena_filters(
                    sl, W, hyena_f_w1[j], hyena_f_b1[j], hyena_f_w2[j], hyena_f_b2[j],
                    hyena_f_w3[j], hyena_f_b3[j], hyena_f_freq[j], hyena_f_wout[j])
                y = hyena_long_convs(xv, fl, hyena_bias[j], nb, sl)
                outs.append(matmul_norm_res([(y, W, 0)], _identity_bf16, hyena_w_out[j], g[1], m[2], s, sl))
        else:
            outs = []
            for (s, m, sl) in streams[:1 + int(update_ctx)]:
                p = norm_mod_matmul(s, g[0], m[0], m[1], pool_w_in[j], sl, F32)
                a = pool_mix(p, pool_w_g[j], pool_b_g[j], pool_scale[j], sl)
                outs.append(matmul_norm_res([(a, a.shape[1], 0)], _identity_bf16, pool_w_out[j],
                                            g[1], m[2], s, sl))
        xs = outs[0]
        if update_ctx:
            xc = outs[1]

        k = i // 2
        todo = [(xs, lat, L)] + ([(xc, cx, Lc)] if update_ctx else [])
        res = []
        for (s, m, sl) in todo:
            if i % 2 == 0:
                res.append(ffn_dense(s, g[2], m[3], m[4], ffn_w_gate[k], ffn_w_up[k], ffn_w_down[k],
                                     g[3], m[5], sl))
            else:
                res.append(moe_block(s, g[2], m[3], m[4], moe_w_router[k], moe_w_gate[k], moe_w_up[k],
                                     moe_w_down[k], g[3], m[5], sl))
        xs = res[0]
        if update_ctx:
            xc = res[1]
    return xs.reshape(B, L, D)
```

```python
import functools
import math

import jax
import jax.numpy as jnp
import numpy as np
from jax import lax
from jax.experimental import pallas as pl
from jax.experimental.pallas import tpu as pltpu

F32 = jnp.float32
BF16 = jnp.bfloat16
EPS = 1e-6

VMEM_LIMIT_BYTES = 52 * 1024 * 1024
LANES = 128
SUBLANES = 8

GRID_W = 64
GMLP_CHUNK = 128
GMLP_GROUPS = 8
LRU_HEADS = 8
LRU_C = 8.0
POOL_WINDOWS = (2, 4, 8, 16)
HYENA_BANDS = 16
HYENA_FAST_DECAY = 0.3
HYENA_SLOW_DECAY = 1.5
HYENA_DECAY_TARGET = 1e-2
N_EXPERTS = 8
TOP_K = 2
FFT_P = 128
HALO = SUBLANES


def _cparams(*sem):
    return pltpu.CompilerParams(dimension_semantics=sem, vmem_limit_bytes=VMEM_LIMIT_BYTES)


def _rms(x, g):
    return x * lax.rsqrt(jnp.mean(x * x, axis=-1, keepdims=True) + EPS) * g


def _gelu(x):
    return 0.5 * x * (1.0 + jnp.tanh(math.sqrt(2.0 / math.pi) * (x + 0.044715 * (x * x * x))))


def _silu(x):
    return x * (1.0 / (1.0 + jnp.exp(-x)))


def _sigmoid(x):
    return 1.0 / (1.0 + jnp.exp(-x))


def _bdot(a, b):
    return jnp.dot(a, b, preferred_element_type=F32)


def _hdot(a, b):
    return jnp.dot(a, b, preferred_element_type=F32, precision=lax.Precision.HIGHEST)


def _row_tile(rows, want):
    t = min(rows, want)
    assert rows % t == 0, (rows, t)
    return t


def _ada_kernel(c_ref, w_ref, b_ref, o_ref):
    o_ref[0] = _hdot(_silu(c_ref[...]), w_ref[0]) + b_ref[0]


def ada_all(cc, ada_w, ada_b):
    depth, d, d6 = ada_w.shape
    nchunk = d6 // d
    return pl.pallas_call(
        _ada_kernel,
        grid=(depth, nchunk),
        in_specs=[
            pl.BlockSpec((SUBLANES, d), lambda i, j: (0, 0)),
            pl.BlockSpec((1, d, d), lambda i, j: (i, 0, j)),
            pl.BlockSpec((1, 1, d), lambda i, j: (i, 0, j)),
        ],
        out_specs=pl.BlockSpec((1, SUBLANES, d), lambda i, j: (i, 0, j)),
        out_shape=jax.ShapeDtypeStruct((depth, SUBLANES, d6), F32),
        compiler_params=_cparams("parallel", "parallel"),
        name="ada",
    )(cc, ada_w, ada_b.reshape(depth, 1, d6))


def _pos_kernel(x_ref, rt_ref, ct_ref, o_ref):
    half = rt_ref.shape[-1]
    x = x_ref[...]
    o_ref[:, :, :half] = x[:, :, :half] + rt_ref[...]
    o_ref[:, :, half:] = x[:, :, half:] + ct_ref[...][None]


def add_pos(x):
    b, l, d = x.shape
    rows = l // GRID_W
    quarter = d // 4
    omega = 1.0 / (10000.0 ** (jnp.arange(quarter, dtype=F32) / quarter))

    def sincos(p):
        ang = p.reshape(-1, 1) * omega[None, :]
        return jnp.concatenate([jnp.sin(ang), jnp.cos(ang)], axis=-1)

    rtab = sincos(jnp.arange(rows, dtype=F32)).reshape(rows, 1, 2 * quarter)
    ctab = sincos(jnp.arange(GRID_W, dtype=F32))
    x3 = x.reshape(b * rows, GRID_W, d)
    tr = _row_tile(rows, 16)
    nrt = rows // tr
    out = pl.pallas_call(
        _pos_kernel,
        grid=(b * nrt,),
        in_specs=[
            pl.BlockSpec((tr, GRID_W, d), lambda i: (i, 0, 0)),
            pl.BlockSpec((tr, 1, 2 * quarter), lambda i: (i % nrt, 0, 0)),
            pl.BlockSpec((GRID_W, 2 * quarter), lambda i: (0, 0)),
        ],
        out_specs=pl.BlockSpec((tr, GRID_W, d), lambda i: (i, 0, 0)),
        out_shape=jax.ShapeDtypeStruct(x3.shape, F32),
        compiler_params=_cparams("parallel"),
        name="add_pos",
    )(x3, rtab, ctab)
    return out.reshape(b * l, d)


def _nmm_kernel(x_ref, g_ref, sh_ref, sc_ref, w_ref, o_ref, h_scr, *, act):
    @pl.when(pl.program_id(1) == 0)
    def _():
        h = _rms(x_ref[...], g_ref[...]) * (1.0 + sc_ref[0]) + sh_ref[0]
        h_scr[...] = h.astype(BF16)

    y = _bdot(h_scr[...], w_ref[...])
    if act == "gelu":
        y = _gelu(y)
    o_ref[...] = y.astype(o_ref.dtype)


def norm_mod_matmul(x, g, shift, scale, w, rows_per_mod, out_dtype, act=None, tm=512, tn=2048):
    r, d = x.shape
    n = w.shape[1]
    tm = _row_tile(min(r, rows_per_mod), tm)
    tn = min(tn, n)
    assert n % tn == 0
    mod_map = lambda i, j: ((i * tm) // rows_per_mod, 0, 0)
    return pl.pallas_call(
        functools.partial(_nmm_kernel, act=act),
        grid=(r // tm, n // tn),
        in_specs=[
            pl.BlockSpec((tm, d), lambda i, j: (i, 0)),
            pl.BlockSpec((1, d), lambda i, j: (0, 0)),
            pl.BlockSpec((1, 1, d), mod_map),
            pl.BlockSpec((1, 1, d), mod_map),
            pl.BlockSpec((d, tn), lambda i, j: (0, j)),
        ],
        out_specs=pl.BlockSpec((tm, tn), lambda i, j: (i, j)),
        out_shape=jax.ShapeDtypeStruct((r, n), out_dtype),
        scratch_shapes=[pltpu.VMEM((tm, d), BF16)],
        compiler_params=_cparams("parallel", "arbitrary"),
        name="norm_mod_matmul",
    )(x, g.reshape(1, d), shift, scale, w.astype(BF16))


def _mnr_kernel(*refs, n_in, prologue):
    in_refs = refs[:n_in]
    w_ref, g_ref, gate_ref, x_ref, o_ref = refs[n_in:]
    a = prologue(*[r[...] for r in in_refs])
    y = _bdot(a, w_ref[...])
    o_ref[...] = x_ref[...] + gate_ref[0] * _rms(y, g_ref[...])


def matmul_norm_res(inputs, prologue, w, g, gate, x, rows_per_mod, tm=512):
    r, d = x.shape
    k = w.shape[0]
    tm = _row_tile(min(r, rows_per_mod), tm)
    mod_map = lambda i: ((i * tm) // rows_per_mod, 0, 0)
    in_specs = [pl.BlockSpec((tm, wd), functools.partial(lambda i, cb: (i, cb), cb=cb))
                for (_, wd, cb) in inputs]
    in_specs += [
        pl.BlockSpec((k, d), lambda i: (0, 0)),
        pl.BlockSpec((1, d), lambda i: (0, 0)),
        pl.BlockSpec((1, 1, d), mod_map),
        pl.BlockSpec((tm, d), lambda i: (i, 0)),
    ]
    return pl.pallas_call(
        functools.partial(_mnr_kernel, n_in=len(inputs), prologue=prologue),
        grid=(r // tm,),
        in_specs=in_specs,
        out_specs=pl.BlockSpec((tm, d), lambda i: (i, 0)),
        out_shape=jax.ShapeDtypeStruct((r, d), F32),
        compiler_params=_cparams("parallel"),
        name="matmul_norm_res",
    )(*[a for (a, _, _) in inputs], w.astype(BF16), g.reshape(1, d), gate, x)


def _identity_bf16(a):
    return a.astype(BF16)


def _swiglu_step(h_scr, wg_ref, wu_ref, wd_ref, acc_scr, j):
    h = h_scr[...]
    t = (_silu(_bdot(h, wg_ref[...])) * _bdot(h, wu_ref[...])).astype(BF16)
    part = _bdot(t, wd_ref[...])

    @pl.when(j == 0)
    def _():
        acc_scr[...] = part

    @pl.when(j > 0)
    def _():
        acc_scr[...] += part


def _ffn_dense_kernel(x_ref, g1_ref, sh_ref, sc_ref, wg_ref, wu_ref, wd_ref, g2_ref, gate_ref,
                      o_ref, h_scr, acc_scr):
    j = pl.program_id(1)

    @pl.when(j == 0)
    def _():
        h = _rms(x_ref[...], g1_ref[...]) * (1.0 + sc_ref[0]) + sh_ref[0]
        h_scr[...] = h.astype(BF16)

    _swiglu_step(h_scr, wg_ref, wu_ref, wd_ref, acc_scr, j)

    @pl.when(j == pl.num_programs(1) - 1)
    def _():
        o_ref[...] = x_ref[...] + gate_ref[0] * _rms(acc_scr[...], g2_ref[...])


def ffn_dense(x, g1, shift, scale, wg, wu, wd, g2, gate, rows_per_mod, tm=1024, tf=256):
    r, d = x.shape
    ff = wg.shape[1]
    tm = _row_tile(min(r, rows_per_mod), tm)
    assert ff % tf == 0
    mod_map = lambda i, j: ((i * tm) // rows_per_mod, 0, 0)
    return pl.pallas_call(
        _ffn_dense_kernel,
        grid=(r // tm, ff // tf),
        in_specs=[
            pl.BlockSpec((tm, d), lambda i, j: (i, 0)),
            pl.BlockSpec((1, d), lambda i, j: (0, 0)),
            pl.BlockSpec((1, 1, d), mod_map),
            pl.BlockSpec((1, 1, d), mod_map),
            pl.BlockSpec((d, tf), lambda i, j: (0, j)),
            pl.BlockSpec((d, tf), lambda i, j: (0, j)),
            pl.BlockSpec((tf, d), lambda i, j: (j, 0)),
            pl.BlockSpec((1, d), lambda i, j: (0, 0)),
            pl.BlockSpec((1, 1, d), mod_map),
        ],
        out_specs=pl.BlockSpec((tm, d), lambda i, j: (i, 0)),
        out_shape=jax.ShapeDtypeStruct((r, d), F32),
        scratch_shapes=[pltpu.VMEM((tm, d), BF16), pltpu.VMEM((tm, d), F32)],
        compiler_params=_cparams("parallel", "arbitrary"),
        name="ffn_dense",
    )(x, g1.reshape(1, d), shift, scale, wg.astype(BF16), wu.astype(BF16), wd.astype(BF16),
      g2.reshape(1, d), gate)


def _ffn_routed_kernel(te_ref, nu_ref, x_ref, wg_ref, wu_ref, wd_ref, o_ref, h_scr, acc_scr):
    i = pl.program_id(0)
    j = pl.program_id(1)
    last = pl.num_programs(1) - 1
    used = i < nu_ref[0]

    @pl.when(used)
    def _():
        @pl.when(j == 0)
        def _():
            h_scr[...] = x_ref[...].astype(BF16)

        _swiglu_step(h_scr, wg_ref.at[0], wu_ref.at[0], wd_ref.at[0], acc_scr, j)

        @pl.when(j == last)
        def _():
            o_ref[...] = acc_scr[...]

    @pl.when(jnp.logical_and(jnp.logical_not(used), j == last))
    def _():
        o_ref[...] = jnp.zeros_like(o_ref)


def ffn_routed(xs, tile_expert, n_used, wg, wu, wd, tm, tf=512):
    p, d = xs.shape
    ff = wg.shape[2]
    assert ff % tf == 0 and p % tm == 0
    grid_spec = pltpu.PrefetchScalarGridSpec(
        num_scalar_prefetch=2,
        grid=(p // tm, ff // tf),
        in_specs=[
            pl.BlockSpec((tm, d), lambda i, j, te, nu: (i, 0)),
            pl.BlockSpec((1, d, tf), lambda i, j, te, nu: (te[i], 0, j)),
            pl.BlockSpec((1, d, tf), lambda i, j, te, nu: (te[i], 0, j)),
            pl.BlockSpec((1, tf, d), lambda i, j, te, nu: (te[i], j, 0)),
        ],
        out_specs=pl.BlockSpec((tm, d), lambda i, j, te, nu: (i, 0)),
        scratch_shapes=[pltpu.VMEM((tm, d), BF16), pltpu.VMEM((tm, d), F32)],
    )
    return pl.pallas_call(
        _ffn_routed_kernel,
        grid_spec=grid_spec,
        out_shape=jax.ShapeDtypeStruct((p, d), F32),
        compiler_params=_cparams("arbitrary", "arbitrary"),
        name="ffn_routed",
    )(tile_expert, n_used, xs, wg.astype(BF16), wu.astype(BF16), wd.astype(BF16))


ROUTE_IDX0 = N_EXPERTS
ROUTE_P0 = N_EXPERTS + TOP_K


def _router_kernel(x_ref, g_ref, sh_ref, sc_ref, wr_ref, h_ref, r_ref):
    h = _rms(x_ref[...], g_ref[...]) * (1.0 + sc_ref[0]) + sh_ref[0]
    h_ref[...] = h
    logits = _hdot(h, wr_ref[...])
    lane = lax.broadcasted_iota(jnp.int32, logits.shape, 1)
    neg = jnp.float32(-jnp.inf)
    big = jnp.int32(LANES)
    lg = jnp.where(lane < N_EXPERTS, logits, neg)
    m1 = jnp.max(lg, axis=-1, keepdims=True)
    i1 = jnp.min(jnp.where(lg == m1, lane, big), axis=-1, keepdims=True)
    lg2 = jnp.where(lane == i1, neg, lg)
    m2 = jnp.max(lg2, axis=-1, keepdims=True)
    i2 = jnp.min(jnp.where(lg2 == m2, lane, big), axis=-1, keepdims=True)
    e2 = jnp.exp(m2 - m1)
    p1 = 1.0 / (1.0 + e2)
    p2 = e2 / (1.0 + e2)
    out = jnp.where(lane == ROUTE_IDX0, i1.astype(F32), 0.0)
    out = jnp.where(lane == ROUTE_IDX0 + 1, i2.astype(F32), out)
    out = jnp.where(lane == ROUTE_P0, p1, out)
    out = jnp.where(lane == ROUTE_P0 + 1, p2, out)
    r_ref[...] = out


def router(x, g, shift, scale, w_router, rows_per_mod, tm=512):
    r, d = x.shape
    tm = _row_tile(min(r, rows_per_mod), tm)
    wr = jnp.zeros((d, LANES), F32).at[:, :N_EXPERTS].set(w_router)
    mod_map = lambda i: ((i * tm) // rows_per_mod, 0, 0)
    return pl.pallas_call(
        _router_kernel,
        grid=(r // tm,),
        in_specs=[
            pl.BlockSpec((tm, d), lambda i: (i, 0)),
            pl.BlockSpec((1, d), lambda i: (0, 0)),
            pl.BlockSpec((1, 1, d), mod_map),
            pl.BlockSpec((1, 1, d), mod_map),
            pl.BlockSpec((d, LANES), lambda i: (0, 0)),
        ],
        out_specs=[pl.BlockSpec((tm, d), lambda i: (i, 0)),
                   pl.BlockSpec((tm, LANES), lambda i: (i, 0))],
        out_shape=[jax.ShapeDtypeStruct((r, d), F32), jax.ShapeDtypeStruct((r, LANES), F32)],
        compiler_params=_cparams("parallel"),
        name="router",
    )(x, g.reshape(1, d), shift, scale, wr)


def _gather_kernel(idx_ref, src_ref, o_ref, sem, *, rows):
    base = pl.program_id(0) * rows

    def issue(r, carry):
        pltpu.make_async_copy(src_ref.at[idx_ref[r]], o_ref.at[base + r], sem).start()
        return carry

    lax.fori_loop(0, rows, issue, 0)
    pltpu.make_async_copy(src_ref.at[pl.ds(0, rows)], o_ref.at[pl.ds(base, rows)], sem).wait()


def gather_rows(src, idx, rows_per_step=1024):
    n = idx.shape[0]
    d = src.shape[1]
    rows = _row_tile(n, rows_per_step)
    src3 = src.reshape(src.shape[0], d // LANES, LANES)
    out = pl.pallas_call(
        functools.partial(_gather_kernel, rows=rows),
        grid=(n // rows,),
        in_specs=[
            pl.BlockSpec((rows,), lambda i: (i,), memory_space=pltpu.SMEM),
            pl.BlockSpec(memory_space=pl.ANY),
        ],
        out_specs=pl.BlockSpec(memory_space=pl.ANY),
        out_shape=jax.ShapeDtypeStruct((n, d // LANES, LANES), src.dtype),
        scratch_shapes=[pltpu.SemaphoreType.DMA(())],
        compiler_params=_cparams("arbitrary"),
        name="gather_rows",
    )(idx, src3)
    return out.reshape(n, d)


def _combine_kernel(y0_ref, y1_ref, r_ref, g_ref, gate_ref, x_ref, o_ref):
    rt = r_ref[...]
    p0 = rt[:, ROUTE_P0:ROUTE_P0 + 1]
    p1 = rt[:, ROUTE_P0 + 1:ROUTE_P0 + 2]
    y = p0 * y0_ref[...] + p1 * y1_ref[...]
    o_ref[...] = x_ref[...] + gate_ref[0] * _rms(y, g_ref[...])


def combine(yg, route, g, gate, x, rows_per_mod, tm=512):
    r, d = x.shape
    tm = _row_tile(min(r, rows_per_mod), tm)
    nt = r // tm
    mod_map = lambda i: ((i * tm) // rows_per_mod, 0, 0)
    return pl.pallas_call(
        _combine_kernel,
        grid=(nt,),
        in_specs=[
            pl.BlockSpec((tm, d), lambda i: (i, 0)),
            pl.BlockSpec((tm, d), lambda i: (i + nt, 0)),
            pl.BlockSpec((tm, LANES), lambda i: (i, 0)),
            pl.BlockSpec((1, d), lambda i: (0, 0)),
            pl.BlockSpec((1, 1, d), mod_map),
            pl.BlockSpec((tm, d), lambda i: (i, 0)),
        ],
        out_specs=pl.BlockSpec((tm, d), lambda i: (i, 0)),
        out_shape=jax.ShapeDtypeStruct((r, d), F32),
        compiler_params=_cparams("parallel"),
        name="moe_combine",
    )(yg, yg, route, g.reshape(1, d), gate, x)


def moe_block(x, g1, shift, scale, w_router, wg, wu, wd, g2, gate, rows_per_mod, tm=512):
    n, d = x.shape
    h, route = router(x, g1, shift, scale, w_router, rows_per_mod)
    eidx = route[:, ROUTE_IDX0:ROUTE_IDX0 + TOP_K].astype(jnp.int32)
    flat_e = eidx.T.reshape(-1)
    onehot = (flat_e[:, None] == jnp.arange(N_EXPERTS, dtype=jnp.int32)[None, :]).astype(jnp.int32)
    csum = jnp.cumsum(onehot, axis=0)
    counts = csum[-1]
    rank = jnp.take_along_axis(csum, flat_e[:, None], axis=1)[:, 0] - 1
    tiles_per_e = (counts + tm - 1) // tm
    tile_end = jnp.cumsum(tiles_per_e)
    tile_start = tile_end - tiles_per_e
    slot = tile_start[flat_e] * tm + rank
    n_tiles = (TOP_K * n) // tm + N_EXPERTS
    tok = jnp.tile(jnp.arange(n, dtype=jnp.int32), TOP_K)
    src = jnp.zeros((n_tiles * tm,), jnp.int32).at[slot].set(tok)
    tile_ids = jnp.arange(n_tiles, dtype=jnp.int32)
    tile_expert = jnp.minimum(jnp.sum((tile_ids[:, None] >= tile_end[None, :]).astype(jnp.int32), axis=1),
                              N_EXPERTS - 1).astype(jnp.int32)
    n_used = tile_end[-1:].astype(jnp.int32)

    xs = gather_rows(h, src)
    ys = ffn_routed(xs, tile_expert, n_used, wg, wu, wd, tm)
    yg = gather_rows(ys, slot.astype(jnp.int32))
    return combine(yg, route, g2, gate, x, rows_per_mod)


def _gmlp_kernel(u_ref, v_ref, gv_ref, ws_ref, bs_ref, o_ref, vn_scr):
    v = v_ref[...].astype(F32)
    mu = jnp.mean(v, axis=-1, keepdims=True)
    vc = v - mu
    vn = vc * lax.rsqrt(jnp.mean(vc * vc, axis=-1, keepdims=True) + EPS) * gv_ref[...]
    vn_scr[...] = vn.astype(BF16)
    tm, width = vn_scr.shape
    gw = width // GMLP_GROUPS
    for n in range(tm // GMLP_CHUNK):
        rs = slice(n * GMLP_CHUNK, (n + 1) * GMLP_CHUNK)
        for g in range(GMLP_GROUPS):
            cs = slice(g * gw, (g + 1) * gw)
            m = _bdot(ws_ref[g], vn_scr[rs, cs]) + bs_ref[g]
            o_ref[rs, cs] = (u_ref[rs, cs].astype(F32) * m).astype(BF16)


def gmlp_spatial(hw, g_v, w_s, b_s, tm=512):
    r, w2 = hw.shape
    width = w2 // 2
    gw = width // GMLP_GROUPS
    tm = _row_tile(r, tm)
    bsb = jnp.broadcast_to(b_s[:, :, None], (GMLP_GROUPS, GMLP_CHUNK, gw)).astype(F32)
    return pl.pallas_call(
        _gmlp_kernel,
        grid=(r // tm,),
        in_specs=[
            pl.BlockSpec((tm, width), lambda i: (i, 0)),
            pl.BlockSpec((tm, width), lambda i: (i, 1)),
            pl.BlockSpec((1, width), lambda i: (0, 0)),
            pl.BlockSpec((GMLP_GROUPS, GMLP_CHUNK, GMLP_CHUNK), lambda i: (0, 0, 0)),
            pl.BlockSpec((GMLP_GROUPS, GMLP_CHUNK, gw), lambda i: (0, 0, 0)),
        ],
        out_specs=pl.BlockSpec((tm, width), lambda i: (i, 0)),
        out_shape=jax.ShapeDtypeStruct((r, width), BF16),
        scratch_shapes=[pltpu.VMEM((tm, width), BF16)],
        compiler_params=_cparams("parallel"),
        name="gmlp_spatial",
    )(hw, hw, g_v.reshape(1, width), w_s.astype(BF16), bsb)


def _halo_specs(tm, width, col_block, tile_of, n_row_blocks):
    per = tm // HALO
    cur = pl.BlockSpec((tm, width), lambda i: (tile_of(i), col_block))
    prev = pl.BlockSpec((HALO, width), lambda i: (jnp.maximum(tile_of(i) * per - 1, 0), col_block))
    nxt = pl.BlockSpec((HALO, width),
                       lambda i: (jnp.minimum((tile_of(i) + 1) * per, n_row_blocks - 1), col_block))
    return [cur, prev, nxt]


def _fill_ext(ext, cur_ref, prev_ref, next_ref, tile, tm, seq_len):
    first = (tile * tm) % seq_len == 0
    last = ((tile + 1) * tm) % seq_len == 0
    ext[0:HALO, :] = jnp.where(first, 0.0, prev_ref[...])
    ext[HALO:HALO + tm, :] = cur_ref[...]
    ext[HALO + tm:HALO + tm + HALO, :] = jnp.where(last, 0.0, next_ref[...])


def _lru_kernel(xb_ref, xp_ref, xn_ref, cw_ref, cb_ref, wax_ref, ba_ref, bx_ref, lam_ref, h0_ref,
                o_ref, ext, a_scr, b_scr, carry, *, tm, seq_len, n_tiles, reverse):
    i = pl.program_id(0)
    tile = (n_tiles - 1 - i) if reverse else i
    _fill_ext(ext, xb_ref, xp_ref, xn_ref, tile, tm, seq_len)
    kk = cw_ref.shape[0]
    left = kk // 2
    xc = cb_ref[...] + cw_ref[0:1, :] * ext[pl.ds(HALO - left, tm), :]
    for k in range(1, kk):
        xc = xc + cw_ref[k:k + 1, :] * ext[pl.ds(HALO - left + k, tm), :]

    width = xc.shape[1]
    hd = width // LRU_HEADS
    lam = lam_ref[...]
    sp = jnp.maximum(-lam, 0.0) + jnp.log(1.0 + jnp.exp(-jnp.abs(lam)))
    for hh in range(LRU_HEADS):
        cs = slice(hh * hd, (hh + 1) * hd)
        xh = xc[:, cs]
        pre = _bdot(xh.astype(BF16), wax_ref[hh])
        rg = _sigmoid(pre[:, :hd] + ba_ref[:, cs])
        ig = _sigmoid(pre[:, hd:] + bx_ref[:, cs])
        a = jnp.exp(-LRU_C * rg * sp[:, cs])
        a_scr[:, cs] = a
        b_scr[:, cs] = jnp.sqrt(1.0 - a * a) * (ig * xh)

    first = (tile * tm) % seq_len == 0
    last = ((tile + 1) * tm) % seq_len == 0

    @pl.when(last if reverse else first)
    def _():
        carry[...] = jnp.broadcast_to(h0_ref[0], carry.shape)

    row = lax.broadcasted_iota(jnp.int32, (SUBLANES, width), 0)
    nblk = tm // SUBLANES

    def body(k, c):
        blk = (nblk - 1 - k) if reverse else k
        r0 = pl.multiple_of(blk * SUBLANES, SUBLANES)
        a = a_scr[pl.ds(r0, SUBLANES), :]
        b = b_scr[pl.ds(r0, SUBLANES), :]
        for s in (1, 2, 4):
            shift = (SUBLANES - s) if reverse else s
            a_sh = pltpu.roll(a, shift, 0)
            b_sh = pltpu.roll(b, shift, 0)
            m = (row < SUBLANES - s) if reverse else (row >= s)
            b = jnp.where(m, a * b_sh + b, b)
            a = jnp.where(m, a * a_sh, a)
        h = a * c + b
        o_ref[pl.ds(r0, SUBLANES), :] = h
        edge = h[0:1, :] if reverse else h[SUBLANES - 1:SUBLANES, :]
        return jnp.broadcast_to(edge, c.shape)

    carry[...] = lax.fori_loop(0, nblk, body, carry[...])


def lru_scan(z, col_block, width, conv_w, conv_b, w_a, b_a, w_x, b_x, lam, h0, seq_len, reverse, tm=512):
    r = z.shape[0]
    tm = _row_tile(seq_len, tm)
    n_tiles = r // tm
    nb = r // seq_len
    tile_of = (lambda i: n_tiles - 1 - i) if reverse else (lambda i: i)
    wax = jnp.concatenate([w_a, w_x], axis=-1).astype(BF16)
    kk = conv_w.shape[0]
    hd = width // LRU_HEADS
    const2 = lambda i: (0, 0)
    return pl.pallas_call(
        functools.partial(_lru_kernel, tm=tm, seq_len=seq_len, n_tiles=n_tiles, reverse=reverse),
        grid=(n_tiles,),
        in_specs=_halo_specs(tm, width, col_block, tile_of, r // HALO) + [
            pl.BlockSpec((kk, width), const2),
            pl.BlockSpec((1, width), const2),
            pl.BlockSpec((LRU_HEADS, hd, 2 * hd), lambda i: (0, 0, 0)),
            pl.BlockSpec((1, width), const2),
            pl.BlockSpec((1, width), const2),
            pl.BlockSpec((1, width), const2),
            pl.BlockSpec((1, 1, width), lambda i: ((tile_of(i) * tm) // seq_len, 0, 0)),
        ],
        out_specs=pl.BlockSpec((tm, width), lambda i: (tile_of(i), 0)),
        out_shape=jax.ShapeDtypeStruct((r, width), F32),
        scratch_shapes=[
            pltpu.VMEM((tm + 2 * HALO, width), F32),
            pltpu.VMEM((tm, width), F32),
            pltpu.VMEM((tm, width), F32),
            pltpu.VMEM((SUBLANES, width), F32),
        ],
        compiler_params=_cparams("arbitrary"),
        name="lru_scan_bwd" if reverse else "lru_scan_fwd",
    )(z, z, z, conv_w, conv_b.reshape(1, width), wax, b_a.reshape(1, width), b_x.reshape(1, width),
      lam.reshape(1, width), h0.reshape(nb, 1, width))


def _lru_out_prologue(gate, hf, hb):
    return (_gelu(gate) * (hf + hb)).astype(BF16)


def _pool_kernel(p_ref, pp_ref, pn_ref, wg_ref, bg_ref, sc_ref, o_ref, ext, *, tm, seq_len):
    i = pl.program_id(0)
    _fill_ext(ext, p_ref, pp_ref, pn_ref, i, tm, seq_len)
    width = o_ref.shape[1]
    gw = width // len(POOL_WINDOWS)
    t = ((i * tm) % seq_len + lax.broadcasted_iota(jnp.int32, (tm, 1), 0))
    for g, win in enumerate(POOL_WINDOWS):
        half = win // 2
        cs = slice(g * gw, (g + 1) * gw)
        s = ext[pl.ds(HALO - half, tm), cs]
        for k in range(1 - half, half):
            s = s + ext[pl.ds(HALO + k, tm), cs]
        cnt = (jnp.minimum(t + half, seq_len) - jnp.maximum(t - half, 0)).astype(F32)
        q = s / cnt - ext[pl.ds(HALO, tm), cs]
        y = _bdot(q.astype(BF16), wg_ref[g]) + bg_ref[:, cs]
        o_ref[:, cs] = (y * sc_ref[:, cs]).astype(BF16)


def pool_mix(p, w_g, b_g, scale, seq_len, tm=512):
    r, width = p.shape
    assert max(POOL_WINDOWS) // 2 <= HALO
    tm = _row_tile(seq_len, tm)
    ng, gw, _ = w_g.shape
    const2 = lambda i: (0, 0)
    return pl.pallas_call(
        functools.partial(_pool_kernel, tm=tm, seq_len=seq_len),
        grid=(r // tm,),
        in_specs=_halo_specs(tm, width, 0, lambda i: i, r // HALO) + [
            pl.BlockSpec((ng, gw, gw), lambda i: (0, 0, 0)),
            pl.BlockSpec((1, width), const2),
            pl.BlockSpec((1, width), const2),
        ],
        out_specs=pl.BlockSpec((tm, width), lambda i: (i, 0)),
        out_shape=jax.ShapeDtypeStruct((r, width), BF16),
        scratch_shapes=[pltpu.VMEM((tm + 2 * HALO, width), F32)],
        compiler_params=_cparams("parallel"),
        name="pool_mix",
    )(p, p, p, w_g.astype(BF16), b_g.reshape(1, width), scale.reshape(1, width))


def _hyconv_kernel(z_ref, zp_ref, zn_ref, cw_ref, cb_ref, o_ref, ext, *, tm, seq_len):
    i = pl.program_id(0)
    _fill_ext(ext, z_ref, zp_ref, zn_ref, i, tm, seq_len)
    kk = cw_ref.shape[0]
    left = kk // 2
    y = cb_ref[...] + cw_ref[0:1, :] * ext[pl.ds(HALO - left, tm), :]
    for k in range(1, kk):
        y = y + cw_ref[k:k + 1, :] * ext[pl.ds(HALO - left + k, tm), :]
    o_ref[0] = y.astype(BF16)


def hyena_short_conv(z, conv_w, conv_b, width, seq_len, tm=512):
    r = z.shape[0]
    nsplit = z.shape[1] // width
    tm = _row_tile(seq_len, tm)
    kk = conv_w.shape[0]
    n_row_blocks = r // HALO
    per = tm // HALO
    return pl.pallas_call(
        functools.partial(_hyconv_kernel, tm=tm, seq_len=seq_len),
        grid=(r // tm, nsplit),
        in_specs=[
            pl.BlockSpec((tm, width), lambda i, j: (i, j)),
            pl.BlockSpec((HALO, width), lambda i, j: (jnp.maximum(i * per - 1, 0), j)),
            pl.BlockSpec((HALO, width), lambda i, j: (jnp.minimum((i + 1) * per, n_row_blocks - 1), j)),
            pl.BlockSpec((kk, width), lambda i, j: (0, j)),
            pl.BlockSpec((1, width), lambda i, j: (0, j)),
        ],
        out_specs=pl.BlockSpec((1, tm, width), lambda i, j: (j, i, 0)),
        out_shape=jax.ShapeDtypeStruct((nsplit, r, width), BF16),
        scratch_shapes=[pltpu.VMEM((tm + 2 * HALO, width), F32)],
        compiler_params=_cparams("parallel", "arbitrary"),
        name="hyena_short_conv",
    )(z, z, z, conv_w, conv_b.reshape(1, -1))


def _hyfilter_kernel(z_ref, w1_ref, b1_ref, w2_ref, b2_ref, w3_ref, b3_ref, fr_ref, wo_ref, dl_ref,
                     o_ref, *, n_out):
    z = z_ref[...]
    h = jnp.sin(fr_ref[0:1, :] * (_hdot(z, w1_ref[...]) + b1_ref[...]))
    h = jnp.sin(fr_ref[1:2, :] * (_hdot(h, w2_ref[...]) + b2_ref[...]))
    h = jnp.sin(fr_ref[2:3, :] * (_hdot(h, w3_ref[...]) + b3_ref[...]))
    window = jnp.exp(-z[:, 0:1] * dl_ref[...])
    width = dl_ref.shape[1]
    for q in range(n_out):
        cs = slice(q * width, (q + 1) * width)
        o_ref[:, cs] = _hdot(h, wo_ref[:, cs]) * window


def hyena_filters(seq_len, width, f_w1, f_b1, f_w2, f_b2, f_w3, f_b3, f_freq, f_wout, tm=512):
    t = jnp.linspace(0.0, 1.0, seq_len, dtype=F32)[:, None]
    w = 2.0 * math.pi * jnp.arange(seq_len, dtype=F32)[:, None] / seq_len
    bands = jnp.linspace(1e-4, HYENA_BANDS - 1, HYENA_BANDS, dtype=F32)[None, :]
    z = jnp.concatenate([t, jnp.cos(bands * w), jnp.sin(-bands * w)], axis=-1)
    emb = z.shape[1]
    hid = f_w1.shape[1]
    zp = jnp.zeros((seq_len, LANES), F32).at[:, :emb].set(z)
    w1p = jnp.zeros((LANES, hid), F32).at[:emb].set(f_w1)
    max_decay = math.log(HYENA_DECAY_TARGET) / HYENA_FAST_DECAY
    min_decay = math.log(HYENA_DECAY_TARGET) / HYENA_SLOW_DECAY
    deltas = jnp.abs(jnp.linspace(min_decay, max_decay, width, dtype=F32)).reshape(1, width)
    n_tot = f_wout.shape[1]
    tm = _row_tile(seq_len, tm)
    c2 = lambda i: (0, 0)
    return pl.pallas_call(
        functools.partial(_hyfilter_kernel, n_out=n_tot // width),
        grid=(seq_len // tm,),
        in_specs=[
            pl.BlockSpec((tm, LANES), lambda i: (i, 0)),
            pl.BlockSpec((LANES, hid), c2), pl.BlockSpec((1, hid), c2),
            pl.BlockSpec((hid, hid), c2), pl.BlockSpec((1, hid), c2),
            pl.BlockSpec((hid, hid), c2), pl.BlockSpec((1, hid), c2),
            pl.BlockSpec((3, hid), c2),
            pl.BlockSpec((hid, n_tot), c2),
            pl.BlockSpec((1, width), c2),
        ],
        out_specs=pl.BlockSpec((tm, n_tot), lambda i: (i, 0)),
        out_shape=jax.ShapeDtypeStruct((seq_len, n_tot), F32),
        compiler_params=_cparams("parallel"),
        name="hyena_filters",
    )(zp, w1p, f_b1.reshape(1, hid), f_w2, f_b2.reshape(1, hid), f_w3, f_b3.reshape(1, hid),
      f_freq, f_wout, deltas)


def _dft_tables(seq_len):
    n = 2 * seq_len
    p_ = FFT_P
    n1 = n // p_
    t1n = n1 // 2
    f1 = np.arange(n1)[:, None]
    t1 = np.arange(n1)[None, :]
    ang1 = 2.0 * np.pi * ((f1 * t1) % n1) / n1
    c1, s1 = np.cos(ang1), np.sin(ang1)
    w1 = np.zeros((2 * n1, 2 * t1n))
    w1[0::2, :t1n] = c1[:, :t1n]
    w1[0::2, t1n:] = s1[:, :t1n]
    w1[1::2, :t1n] = -s1[:, :t1n]
    w1[1::2, t1n:] = c1[:, :t1n]
    w1k = np.zeros((2 * n1, n1))
    w1k[0::2] = c1
    w1k[1::2] = -s1
    w1i = np.zeros((2 * t1n, 2 * n1))
    ct, st = c1.T[:t1n] / n, s1.T[:t1n] / n
    w1i[:t1n, 0::2] = ct
    w1i[:t1n, 1::2] = -st
    w1i[t1n:, 0::2] = st
    w1i[t1n:, 1::2] = ct
    f2 = np.arange(p_)[:, None]
    pp = np.arange(p_)[None, :]
    ang2 = 2.0 * np.pi * ((f2 * pp) % p_) / p_
    c2, s2 = np.cos(ang2), np.sin(ang2)
    fb = np.block([[c2, s2], [-s2, c2]])
    fbi = np.block([[c2, -s2], [s2, c2]])
    angt = 2.0 * np.pi * ((np.arange(n1)[:, None] * np.arange(p_)[None, :]) % n) / n
    lane_bcast = lambda a: jnp.broadcast_to(jnp.asarray(a, dtype=F32)[:, :, None], (n1, p_, LANES))
    as_bf = lambda a: jnp.asarray(a, dtype=F32).astype(BF16)
    return dict(n1=n1, t1n=t1n, w1=as_bf(w1), w1k=as_bf(w1k), w1i=as_bf(w1i), fb=as_bf(fb), fbi=as_bf(fbi),
                twc=lane_bcast(np.cos(angt)), tws=lane_bcast(-np.sin(angt)))


def _mm_kernel(w_ref, x_ref, o_ref):
    o_ref[...] = _bdot(w_ref[...], x_ref[...]).astype(o_ref.dtype)


def left_matmul(w, x, out_dtype, tn=4096):
    m, k = w.shape
    n = x.shape[1]
    tn = min(tn, n)
    assert n % tn == 0
    return pl.pallas_call(
        _mm_kernel,
        grid=(n // tn,),
        in_specs=[pl.BlockSpec((m, k), lambda j: (0, 0)), pl.BlockSpec((k, tn), lambda j: (0, j))],
        out_specs=pl.BlockSpec((m, tn), lambda j: (0, j)),
        out_shape=jax.ShapeDtypeStruct((m, n), out_dtype),
        compiler_params=_cparams("parallel"),
        name="dft_stage1",
    )(w, x)


def _mm_gate_kernel(w_ref, x_ref, g_ref, v_ref, b_ref, o_ref):
    y = _bdot(w_ref[...], x_ref[...])
    v = v_ref[...].astype(F32)
    o_ref[...] = (g_ref[...].astype(F32) * (y + b_ref[...] * v)).astype(o_ref.dtype)


def left_matmul_gate(w, x, gate, v, bias_row, tn=4096):
    m, k = w.shape
    n = x.shape[1]
    tn = min(tn, n)
    assert n % tn == 0
    col = lambda j: (0, j)
    return pl.pallas_call(
        _mm_gate_kernel,
        grid=(n // tn,),
        in_specs=[pl.BlockSpec((m, k), lambda j: (0, 0)), pl.BlockSpec((k, tn), col),
                  pl.BlockSpec((m, tn), col), pl.BlockSpec((m, tn), col), pl.BlockSpec((1, tn), col)],
        out_specs=pl.BlockSpec((m, tn), col),
        out_shape=jax.ShapeDtypeStruct((m, n), BF16),
        compiler_params=_cparams("parallel"),
        name="dft_inverse_stage1_gate",
    )(w, x, gate, v, bias_row)


def _twiddle(re, im, tc, ts, reps):
    tc = jnp.tile(tc, (1, reps))
    ts = jnp.tile(ts, (1, reps))
    return re * tc - im * ts, re * ts + im * tc


def _spec_fwd_kernel(a_ref, tc_ref, ts_ref, fb_ref, o_ref):
    reps = a_ref.shape[-1] // LANES
    tr, ti = _twiddle(a_ref[0, 0].astype(F32), a_ref[0, 1].astype(F32), tc_ref[0], ts_ref[0], reps)
    rhs = jnp.concatenate([tr, ti], axis=0).astype(BF16)
    o_ref[0] = _bdot(fb_ref[...], rhs).astype(o_ref.dtype)


def _spec_conv_kernel(a_ref, tc_ref, ts_ref, fb_ref, fbi_ref, k_ref, o_ref):
    p_ = FFT_P
    reps = a_ref.shape[-1] // LANES
    tc, ts = tc_ref[0], ts_ref[0]
    tr, ti = _twiddle(a_ref[0, 0].astype(F32), a_ref[0, 1].astype(F32), tc, ts, reps)
    x = _bdot(fb_ref[...], jnp.concatenate([tr, ti], axis=0).astype(BF16))
    xr, xi = x[:p_], x[p_:]
    kr, ki = k_ref[0, :p_].astype(F32), k_ref[0, p_:].astype(F32)
    yr = xr * kr - xi * ki
    yi = xr * ki + xi * kr
    bv = _bdot(fbi_ref[...], jnp.concatenate([yr, yi], axis=0).astype(BF16))
    orr, oi = _twiddle(bv[:p_], bv[p_:], tc, -ts, reps)
    o_ref[0, 0] = orr.astype(o_ref.dtype)
    o_ref[0, 1] = oi.astype(o_ref.dtype)


def spectrum_forward(a4, tab, ct=1024):
    n1, _, p_, c = a4.shape
    ct = min(ct, c)
    return pl.pallas_call(
        _spec_fwd_kernel,
        grid=(c // ct, n1),
        in_specs=[
            pl.BlockSpec((1, 2, p_, ct), lambda j, f: (f, 0, 0, j)),
            pl.BlockSpec((1, p_, LANES), lambda j, f: (f, 0, 0)),
            pl.BlockSpec((1, p_, LANES), lambda j, f: (f, 0, 0)),
            pl.BlockSpec((2 * p_, 2 * p_), lambda j, f: (0, 0)),
        ],
        out_specs=pl.BlockSpec((1, 2 * p_, ct), lambda j, f: (f, 0, j)),
        out_shape=jax.ShapeDtypeStruct((n1, 2 * p_, c), BF16),
        compiler_params=_cparams("parallel", "parallel"),
        name="dft_filter_stage2",
    )(a4, tab["twc"], tab["tws"], tab["fb"])


def spectrum_conv(a4, kspec, k_col_block, tab):
    n1, _, p_, c = a4.shape
    return pl.pallas_call(
        _spec_conv_kernel,
        grid=(n1,),
        in_specs=[
            pl.BlockSpec((1, 2, p_, c), lambda f: (f, 0, 0, 0)),
            pl.BlockSpec((1, p_, LANES), lambda f: (f, 0, 0)),
            pl.BlockSpec((1, p_, LANES), lambda f: (f, 0, 0)),
            pl.BlockSpec((2 * p_, 2 * p_), lambda f: (0, 0)),
            pl.BlockSpec((2 * p_, 2 * p_), lambda f: (0, 0)),
            pl.BlockSpec((1, 2 * p_, c), lambda f: (f, 0, k_col_block)),
        ],
        out_specs=pl.BlockSpec((1, 2, p_, c), lambda f: (f, 0, 0, 0)),
        out_shape=jax.ShapeDtypeStruct(a4.shape, BF16),
        compiler_params=_cparams("parallel"),
        name="dft_stage2_conv",
    )(a4, tab["twc"], tab["tws"], tab["fb"], tab["fbi"], kspec)


def hyena_long_convs(xv, filt, bias, batch, seq_len):
    assert batch == 2, "the two sequences of the batch are packed as one complex sequence"
    c = xv.shape[-1]
    tab = _dft_tables(seq_len)
    n1, t1n, p_ = tab["n1"], tab["t1n"], FFT_P
    order = bias.shape[0]
    f4 = filt.reshape(seq_len, order, 2, c)
    kern = jnp.concatenate([f4[:, :, 0], jnp.zeros((1, order, c), F32), f4[:0:-1, :, 1]], axis=0)
    kern = kern.reshape(n1, p_ * order * c).astype(BF16)
    ak = left_matmul(tab["w1k"], kern, BF16)
    kspec = spectrum_forward(ak.reshape(n1, 2, p_, order * c), tab)
    rl = lambda a: a.reshape(batch * t1n, p_ * c)
    y = rl(xv[2])
    for o in range(order):
        a = left_matmul(tab["w1"], y, BF16)
        b4 = spectrum_conv(a.reshape(n1, 2, p_, c), kspec, o, tab)
        brow = jnp.tile(bias[o].astype(F32), p_).reshape(1, p_ * c)
        y = left_matmul_gate(tab["w1i"], b4.reshape(2 * n1, p_ * c), rl(xv[o]), y, brow)
    return y.reshape(batch * seq_len, c)


def kernel(x, c, ctx, c_ctx, ada_w, ada_b, norm_g, gmlp_w_in, gmlp_g_v, gmlp_w_s, gmlp_b_s, gmlp_w_out, lru_w_in, lru_conv_w, lru_conv_b, lru_w_a, lru_b_a, lru_w_x, lru_b_x, lru_lam, lru_w_out, hyena_w_in, hyena_conv_w, hyena_conv_b, hyena_f_w1, hyena_f_b1, hyena_f_w2, hyena_f_b2, hyena_f_w3, hyena_f_b3, hyena_f_freq, hyena_f_wout, hyena_bias, hyena_w_out, pool_w_in, pool_w_g, pool_b_g, pool_scale, pool_w_out, ffn_w_gate, ffn_w_up, ffn_w_down, moe_w_router, moe_w_gate, moe_w_up, moe_w_down):
    B, L, D = x.shape
    Lc = ctx.shape[1]
    depth = ada_w.shape[0]
    n_mixers = 4
    assert B + 1 <= SUBLANES

    cc = jnp.zeros((SUBLANES, D), F32).at[:B].set(c).at[B].set(c_ctx)
    ada = ada_all(cc, ada_w, ada_b).reshape(depth, SUBLANES, 6, D)

    xs = add_pos(x)
    xc = ctx.reshape(B * Lc, D)
    last_ctx = max([i for i in range(depth) if i % n_mixers == 1], default=-1)

    for i in range(depth):
        kind, j = i % n_mixers, i // n_mixers
        update_ctx = i < last_ctx
        read_ctx = i <= last_ctx
        lat = [ada[i, :B, q].reshape(B, 1, D) for q in range(6)]
        cx = [jnp.broadcast_to(ada[i, B:B + 1, q].reshape(1, 1, D), (B, 1, D)) for q in range(6)]
        g = norm_g[i]
        streams = [(xs, lat, L)]
        if read_ctx:
            streams.append((xc, cx, Lc))

        if kind == 0:
            outs = []
            for (s, m, sl) in streams[:1 + int(update_ctx)]:
                hw = norm_mod_matmul(s, g[0], m[0], m[1], gmlp_w_in[j], sl, BF16, act="gelu")
                a = gmlp_spatial(hw, gmlp_g_v[j], gmlp_w_s[j], gmlp_b_s[j])
                outs.append(matmul_norm_res([(a, a.shape[1], 0)], _identity_bf16, gmlp_w_out[j],
                                            g[1], m[2], s, sl))
        elif kind == 1:
            W = lru_w_in.shape[2] // 2
            sc_args = lambda d: (lru_conv_w[j], lru_conv_b[j], lru_w_a[j, d], lru_b_a[j, d],
                                 lru_w_x[j, d], lru_b_x[j, d], lru_lam[j, d])
            zc = norm_mod_matmul(xc, g[0], cx[0], cx[1], lru_w_in[j], Lc, F32)
            zero = jnp.zeros((B, W), F32)
            hf_c = lru_scan(zc, 1, W, *sc_args(0), zero, Lc, False)
            hb_c = lru_scan(zc, 1, W, *sc_args(1), zero, Lc, True)
            zl = norm_mod_matmul(xs, g[0], lat[0], lat[1], lru_w_in[j], L, F32)
            hf = lru_scan(zl, 1, W, *sc_args(0), hf_c.reshape(B, Lc, W)[:, -1], L, False)
            hb = lru_scan(zl, 1, W, *sc_args(1), hb_c.reshape(B, Lc, W)[:, 0], L, True)
            outs = [matmul_norm_res([(zl, W, 0), (hf, W, 0), (hb, W, 0)], _lru_out_prologue,
                                    lru_w_out[j], g[1], lat[2], xs, L)]
            if update_ctx:
                outs.append(matmul_norm_res([(zc, W, 0), (hf_c, W, 0), (hb_c, W, 0)], _lru_out_prologue,
                                            lru_w_out[j], g[1], cx[2], xc, Lc))
        elif kind == 2:
            W = hyena_w_out.shape[1]
            filt = hyena_filters(L, W, hyena_f_w1[j], hyena_f_b1[j], hyena_f_w2[j], hyena_f_b2[j],
                                 hyena_f_w3[j], hyena_f_b3[j], hyena_f_freq[j], hyena_f_wout[j])
            outs = []
            for (s, m, sl) in streams[:1 + int(update_ctx)]:
                nb = s.shape[0] // sl
                z = norm_mod_matmul(s, g[0], m[0], m[1], hyena_w_in[j], sl, F32, tn=1024)
                xv = hyena_short_conv(z, hyena_conv_w[j], hyena_conv_b[j], W, sl)
                fl = filt if sl == L else hyena_filters(
                    sl, W, hyena_f_w1[j], hyena_f_b1[j], hyena_f_w2[j], hyena_f_b2[j],
                    hyena_f_w3[j], hyena_f_b3[j], hyena_f_freq[j], hyena_f_wout[j])
                y = hyena_long_convs(xv, fl, hyena_bias[j], nb, sl)
                outs.append(matmul_norm_res([(y, W, 0)], _identity_bf16, hyena_w_out[j], g[1], m[2], s, sl))
        else:
            outs = []
            for (s, m, sl) in streams[:1 + int(update_ctx)]:
                p = norm_mod_matmul(s, g[0], m[0], m[1], pool_w_in[j], sl, F32)
                a = pool_mix(p, pool_w_g[j], pool_b_g[j], pool_scale[j], sl)
                outs.append(matmul_norm_res([(a, a.shape[1], 0)], _identity_bf16, pool_w_out[j],
                                            g[1], m[2], s, sl))
        xs = outs[0]
        if update_ctx:
            xc = outs[1]

        k = i // 2
        todo = [(xs, lat, L)] + ([(xc, cx, Lc)] if update_ctx else [])
        res = []
        for (s, m, sl) in todo:
            if i % 2 == 0:
                res.append(ffn_dense(s, g[2], m[3], m[4], ffn_w_gate[k], ffn_w_up[k], ffn_w_down[k],
                                     g[3], m[5], sl))
            else:
                res.append(moe_block(s, g[2], m[3], m[4], moe_w_router[k], moe_w_gate[k], moe_w_up[k],
                                     moe_w_down[k], g[3], m[5], sl))
        xs = res[0]
        if update_ctx:
            xc = res[1]
    return xs.reshape(B, L, D)
```

```python
import functools
import math

import jax
import jax.numpy as jnp
import numpy as np
from jax import lax
from jax.experimental import pallas as pl
from jax.experimental.pallas import tpu as pltpu

F32 = jnp.float32
BF16 = jnp.bfloat16
EPS = 1e-6

VMEM_LIMIT_BYTES = 52 * 1024 * 1024
LANES = 128
SUBLANES = 8

GRID_W = 64
GMLP_CHUNK = 128
GMLP_GROUPS = 8
LRU_HEADS = 8
LRU_C = 8.0
POOL_WINDOWS = (2, 4, 8, 16)
HYENA_BANDS = 16
HYENA_FAST_DECAY = 0.3
HYENA_SLOW_DECAY = 1.5
HYENA_DECAY_TARGET = 1e-2
N_EXPERTS = 8
TOP_K = 2
FFT_P = 128
HALO = SUBLANES


def _cparams(*sem):
    return pltpu.CompilerParams(dimension_semantics=sem, vmem_limit_bytes=VMEM_LIMIT_BYTES)


def _rms(x, g):
    return x * lax.rsqrt(jnp.mean(x * x, axis=-1, keepdims=True) + EPS) * g


def _gelu(x):
    return 0.5 * x * (1.0 + jnp.tanh(math.sqrt(2.0 / math.pi) * (x + 0.044715 * (x * x * x))))


def _silu(x):
    return x * (1.0 / (1.0 + jnp.exp(-x)))


def _sigmoid(x):
    return 1.0 / (1.0 + jnp.exp(-x))


def _bdot(a, b):
    return jnp.dot(a, b, preferred_element_type=F32)


def _hdot(a, b):
    return jnp.dot(a, b, preferred_element_type=F32, precision=lax.Precision.HIGHEST)


def _row_tile(rows, want):
    t = min(rows, want)
    assert rows % t == 0, (rows, t)
    return t


def _ada_kernel(c_ref, w_ref, b_ref, o_ref):
    o_ref[0] = _hdot(_silu(c_ref[...]), w_ref[0]) + b_ref[0]


def ada_all(cc, ada_w, ada_b):
    depth, d, d6 = ada_w.shape
    nchunk = d6 // d
    return pl.pallas_call(
        _ada_kernel,
        grid=(depth, nchunk),
        in_specs=[
            pl.BlockSpec((SUBLANES, d), lambda i, j: (0, 0)),
            pl.BlockSpec((1, d, d), lambda i, j: (i, 0, j)),
            pl.BlockSpec((1, 1, d), lambda i, j: (i, 0, j)),
        ],
        out_specs=pl.BlockSpec((1, SUBLANES, d), lambda i, j: (i, 0, j)),
        out_shape=jax.ShapeDtypeStruct((depth, SUBLANES, d6), F32),
        compiler_params=_cparams("parallel", "parallel"),
        name="ada",
    )(cc, ada_w, ada_b.reshape(depth, 1, d6))


def _pos_kernel(x_ref, rt_ref, ct_ref, o_ref):
    half = rt_ref.shape[-1]
    x = x_ref[...]
    o_ref[:, :, :half] = x[:, :, :half] + rt_ref[...]
    o_ref[:, :, half:] = x[:, :, half:] + ct_ref[...][None]


def add_pos(x):
    b, l, d = x.shape
    rows = l // GRID_W
    quarter = d // 4
    omega = 1.0 / (10000.0 ** (jnp.arange(quarter, dtype=F32) / quarter))

    def sincos(p):
        ang = p.reshape(-1, 1) * omega[None, :]
        return jnp.concatenate([jnp.sin(ang), jnp.cos(ang)], axis=-1)

    rtab = sincos(jnp.arange(rows, dtype=F32)).reshape(rows, 1, 2 * quarter)
    ctab = sincos(jnp.arange(GRID_W, dtype=F32))
    x3 = x.reshape(b * rows, GRID_W, d)
    tr = _row_tile(rows, 16)
    nrt = rows // tr
    out = pl.pallas_call(
        _pos_kernel,
        grid=(b * nrt,),
        in_specs=[
            pl.BlockSpec((tr, GRID_W, d), lambda i: (i, 0, 0)),
            pl.BlockSpec((tr, 1, 2 * quarter), lambda i: (i % nrt, 0, 0)),
            pl.BlockSpec((GRID_W, 2 * quarter), lambda i: (0, 0)),
        ],
        out_specs=pl.BlockSpec((tr, GRID_W, d), lambda i: (i, 0, 0)),
        out_shape=jax.ShapeDtypeStruct(x3.shape, F32),
        compiler_params=_cparams("parallel"),
        name="add_pos",
    )(x3, rtab, ctab)
    return out.reshape(b * l, d)


def _nmm_kernel(x_ref, g_ref, sh_ref, sc_ref, w_ref, o_ref, h_scr, *, act):
    @pl.when(pl.program_id(1) == 0)
    def _():
        h = _rms(x_ref[...], g_ref[...]) * (1.0 + sc_ref[0]) + sh_ref[0]
        h_scr[...] = h.astype(BF16)

    y = _bdot(h_scr[...], w_ref[...])
    if act == "gelu":
        y = _gelu(y)
    o_ref[...] = y.astype(o_ref.dtype)


def norm_mod_matmul(x, g, shift, scale, w, rows_per_mod, out_dtype, act=None, tm=512, tn=2048):
    r, d = x.shape
    n = w.shape[1]
    tm = _row_tile(min(r, rows_per_mod), tm)
    tn = min(tn, n)
    assert n % tn == 0
    mod_map = lambda i, j: ((i * tm) // rows_per_mod, 0, 0)
    return pl.pallas_call(
        functools.partial(_nmm_kernel, act=act),
        grid=(r // tm, n // tn),
        in_specs=[
            pl.BlockSpec((tm, d), lambda i, j: (i, 0)),
            pl.BlockSpec((1, d), lambda i, j: (0, 0)),
            pl.BlockSpec((1, 1, d), mod_map),
            pl.BlockSpec((1, 1, d), mod_map),
            pl.BlockSpec((d, tn), lambda i, j: (0, j)),
        ],
        out_specs=pl.BlockSpec((tm, tn), lambda i, j: (i, j)),
        out_shape=jax.ShapeDtypeStruct((r, n), out_dtype),
        scratch_shapes=[pltpu.VMEM((tm, d), BF16)],
        compiler_params=_cparams("parallel", "arbitrary"),
        name="norm_mod_matmul",
    )(x, g.reshape(1, d), shift, scale, w.astype(BF16))


def _mnr_kernel(*refs, n_in, prologue):
    in_refs = refs[:n_in]
    w_ref, g_ref, gate_ref, x_ref, o_ref = refs[n_in:]
    a = prologue(*[r[...] for r in in_refs])
    y = _bdot(a, w_ref[...])
    o_ref[...] = x_ref[...] + gate_ref[0] * _rms(y, g_ref[...])


def matmul_norm_res(inputs, prologue, w, g, gate, x, rows_per_mod, tm=512):
    r, d = x.shape
    k = w.shape[0]
    tm = _row_tile(min(r, rows_per_mod), tm)
    mod_map = lambda i: ((i * tm) // rows_per_mod, 0, 0)
    in_specs = [pl.BlockSpec((tm, wd), functools.partial(lambda i, cb: (i, cb), cb=cb))
                for (_, wd, cb) in inputs]
    in_specs += [
        pl.BlockSpec((k, d), lambda i: (0, 0)),
        pl.BlockSpec((1, d), lambda i: (0, 0)),
        pl.BlockSpec((1, 1, d), mod_map),
        pl.BlockSpec((tm, d), lambda i: (i, 0)),
    ]
    return pl.pallas_call(
        functools.partial(_mnr_kernel, n_in=len(inputs), prologue=prologue),
        grid=(r // tm,),
        in_specs=in_specs,
        out_specs=pl.BlockSpec((tm, d), lambda i: (i, 0)),
        out_shape=jax.ShapeDtypeStruct((r, d), F32),
        compiler_params=_cparams("parallel"),
        name="matmul_norm_res",
    )(*[a for (a, _, _) in inputs], w.astype(BF16), g.reshape(1, d), gate, x)


def _identity_bf16(a):
    return a.astype(BF16)


def _swiglu_step(h_scr, wg_ref, wu_ref, wd_ref, acc_scr, j):
    h = h_scr[...]
    t = (_silu(_bdot(h, wg_ref[...])) * _bdot(h, wu_ref[...])).astype(BF16)
    part = _bdot(t, wd_ref[...])

    @pl.when(j == 0)
    def _():
        acc_scr[...] = part

    @pl.when(j > 0)
    def _():
        acc_scr[...] += part


def _ffn_dense_kernel(x_ref, g1_ref, sh_ref, sc_ref, wg_ref, wu_ref, wd_ref, g2_ref, gate_ref,
                      o_ref, h_scr, acc_scr):
    j = pl.program_id(1)

    @pl.when(j == 0)
    def _():
        h = _rms(x_ref[...], g1_ref[...]) * (1.0 + sc_ref[0]) + sh_ref[0]
        h_scr[...] = h.astype(BF16)

    _swiglu_step(h_scr, wg_ref, wu_ref, wd_ref, acc_scr, j)

    @pl.when(j == pl.num_programs(1) - 1)
    def _():
        o_ref[...] = x_ref[...] + gate_ref[0] * _rms(acc_scr[...], g2_ref[...])


def ffn_dense(x, g1, shift, scale, wg, wu, wd, g2, gate, rows_per_mod, tm=1024, tf=256):
    r, d = x.shape
    ff = wg.shape[1]
    tm = _row_tile(min(r, rows_per_mod), tm)
    assert ff % tf == 0
    mod_map = lambda i, j: ((i * tm) // rows_per_mod, 0, 0)
    return pl.pallas_call(
        _ffn_dense_kernel,
        grid=(r // tm, ff // tf),
        in_specs=[
            pl.BlockSpec((tm, d), lambda i, j: (i, 0)),
            pl.BlockSpec((1, d), lambda i, j: (0, 0)),
            pl.BlockSpec((1, 1, d), mod_map),
            pl.BlockSpec((1, 1, d), mod_map),
            pl.BlockSpec((d, tf), lambda i, j: (0, j)),
            pl.BlockSpec((d, tf), lambda i, j: (0, j)),
            pl.BlockSpec((tf, d), lambda i, j: (j, 0)),
            pl.BlockSpec((1, d), lambda i, j: (0, 0)),
            pl.BlockSpec((1, 1, d), mod_map),
        ],
        out_specs=pl.BlockSpec((tm, d), lambda i, j: (i, 0)),
        out_shape=jax.ShapeDtypeStruct((r, d), F32),
        scratch_shapes=[pltpu.VMEM((tm, d), BF16), pltpu.VMEM((tm, d), F32)],
        compiler_params=_cparams("parallel", "arbitrary"),
        name="ffn_dense",
    )(x, g1.reshape(1, d), shift, scale, wg.astype(BF16), wu.astype(BF16), wd.astype(BF16),
      g2.reshape(1, d), gate)


def _ffn_routed_kernel(te_ref, nu_ref, x_ref, wg_ref, wu_ref, wd_ref, o_ref, h_scr, acc_scr):
    i = pl.program_id(0)
    j = pl.program_id(1)
    last = pl.num_programs(1) - 1
    used = i < nu_ref[0]

    @pl.when(used)
    def _():
        @pl.when(j == 0)
        def _():
            h_scr[...] = x_ref[...].astype(BF16)

        _swiglu_step(h_scr, wg_ref.at[0], wu_ref.at[0], wd_ref.at[0], acc_scr, j)

        @pl.when(j == last)
        def _():
            o_ref[...] = acc_scr[...]

    @pl.when(jnp.logical_and(jnp.logical_not(used), j == last))
    def _():
        o_ref[...] = jnp.zeros_like(o_ref)


def ffn_routed(xs, tile_expert, n_used, wg, wu, wd, tm, tf=512):
    p, d = xs.shape
    ff = wg.shape[2]
    assert ff % tf == 0 and p % tm == 0
    grid_spec = pltpu.PrefetchScalarGridSpec(
        num_scalar_prefetch=2,
        grid=(p // tm, ff // tf),
        in_specs=[
            pl.BlockSpec((tm, d), lambda i, j, te, nu: (i, 0)),
            pl.BlockSpec((1, d, tf), lambda i, j, te, nu: (te[i], 0, j)),
            pl.BlockSpec((1, d, tf), lambda i, j, te, nu: (te[i], 0, j)),
            pl.BlockSpec((1, tf, d), lambda i, j, te, nu: (te[i], j, 0)),
        ],
        out_specs=pl.BlockSpec((tm, d), lambda i, j, te, nu: (i, 0)),
        scratch_shapes=[pltpu.VMEM((tm, d), BF16), pltpu.VMEM((tm, d), F32)],
    )
    return pl.pallas_call(
        _ffn_routed_kernel,
        grid_spec=grid_spec,
        out_shape=jax.ShapeDtypeStruct((p, d), F32),
        compiler_params=_cparams("arbitrary", "arbitrary"),
        name="ffn_routed",
    )(tile_expert, n_used, xs, wg.astype(BF16), wu.astype(BF16), wd.astype(BF16))


ROUTE_IDX0 = N_EXPERTS
ROUTE_P0 = N_EXPERTS + TOP_K


def _router_kernel(x_ref, g_ref, sh_ref, sc_ref, wr_ref, h_ref, r_ref):
    h = _rms(x_ref[...], g_ref[...]) * (1.0 + sc_ref[0]) + sh_ref[0]
    h_ref[...] = h
    logits = _hdot(h, wr_ref[...])
    lane = lax.broadcasted_iota(jnp.int32, logits.shape, 1)
    neg = jnp.float32(-jnp.inf)
    big = jnp.int32(LANES)
    lg = jnp.where(lane < N_EXPERTS, logits, neg)
    m1 = jnp.max(lg, axis=-1, keepdims=True)
    i1 = jnp.min(jnp.where(lg == m1, lane, big), axis=-1, keepdims=True)
    lg2 = jnp.where(lane == i1, neg, lg)
    m2 = jnp.max(lg2, axis=-1, keepdims=True)
    i2 = jnp.min(jnp.where(lg2 == m2, lane, big), axis=-1, keepdims=True)
    e2 = jnp.exp(m2 - m1)
    p1 = 1.0 / (1.0 + e2)
    p2 = e2 / (1.0 + e2)
    out = jnp.where(lane == ROUTE_IDX0, i1.astype(F32), 0.0)
    out = jnp.where(lane == ROUTE_IDX0 + 1, i2.astype(F32), out)
    out = jnp.where(lane == ROUTE_P0, p1, out)
    out = jnp.where(lane == ROUTE_P0 + 1, p2, out)
    r_ref[...] = out


def router(x, g, shift, scale, w_router, rows_per_mod, tm=512):
    r, d = x.shape
    tm = _row_tile(min(r, rows_per_mod), tm)
    wr = jnp.zeros((d, LANES), F32).at[:, :N_EXPERTS].set(w_router)
    mod_map = lambda i: ((i * tm) // rows_per_mod, 0, 0)
    return pl.pallas_call(
        _router_kernel,
        grid=(r // tm,),
        in_specs=[
            pl.BlockSpec((tm, d), lambda i: (i, 0)),
            pl.BlockSpec((1, d), lambda i: (0, 0)),
            pl.BlockSpec((1, 1, d), mod_map),
            pl.BlockSpec((1, 1, d), mod_map),
            pl.BlockSpec((d, LANES), lambda i: (0, 0)),
        ],
        out_specs=[pl.BlockSpec((tm, d), lambda i: (i, 0)),
                   pl.BlockSpec((tm, LANES), lambda i: (i, 0))],
        out_shape=[jax.ShapeDtypeStruct((r, d), F32), jax.ShapeDtypeStruct((r, LANES), F32)],
        compiler_params=_cparams("parallel"),
        name="router",
    )(x, g.reshape(1, d), shift, scale, wr)


def _gather_kernel(idx_ref, src_ref, o_ref, sem, *, rows):
    def issue(r, carry):
        pltpu.make_async_copy(src_ref.at[idx_ref[r]], o_ref.at[r], sem).start()
        return carry

    lax.fori_loop(0, rows, issue, 0)
    pltpu.make_async_copy(src_ref.at[pl.ds(0, rows)], o_ref, sem).wait()


def gather_rows(src, idx, rows_per_step=512):
    n = idx.shape[0]
    d = src.shape[1]
    rows = _row_tile(n, rows_per_step)
    src3 = src.reshape(src.shape[0], d // LANES, LANES)
    out = pl.pallas_call(
        functools.partial(_gather_kernel, rows=rows),
        grid=(n // rows,),
        in_specs=[
            pl.BlockSpec((rows,), lambda i: (i,), memory_space=pltpu.SMEM),
            pl.BlockSpec(memory_space=pl.ANY),
        ],
        out_specs=pl.BlockSpec((rows, d // LANES, LANES), lambda i: (i, 0, 0)),
        out_shape=jax.ShapeDtypeStruct((n, d // LANES, LANES), src.dtype),
        scratch_shapes=[pltpu.SemaphoreType.DMA(())],
        compiler_params=_cparams("arbitrary"),
        name="gather_rows",
    )(idx, src3)
    return out.reshape(n, d)


def _combine_kernel(y0_ref, y1_ref, r_ref, g_ref, gate_ref, x_ref, o_ref):
    rt = r_ref[...]
    p0 = rt[:, ROUTE_P0:ROUTE_P0 + 1]
    p1 = rt[:, ROUTE_P0 + 1:ROUTE_P0 + 2]
    y = p0 * y0_ref[...] + p1 * y1_ref[...]
    o_ref[...] = x_ref[...] + gate_ref[0] * _rms(y, g_ref[...])


def combine(yg, route, g, gate, x, rows_per_mod, tm=512):
    r, d = x.shape
    tm = _row_tile(min(r, rows_per_mod), tm)
    nt = r // tm
    mod_map = lambda i: ((i * tm) // rows_per_mod, 0, 0)
    return pl.pallas_call(
        _combine_kernel,
        grid=(nt,),
        in_specs=[
            pl.BlockSpec((tm, d), lambda i: (i, 0)),
            pl.BlockSpec((tm, d), lambda i: (i + nt, 0)),
            pl.BlockSpec((tm, LANES), lambda i: (i, 0)),
            pl.BlockSpec((1, d), lambda i: (0, 0)),
            pl.BlockSpec((1, 1, d), mod_map),
            pl.BlockSpec((tm, d), lambda i: (i, 0)),
        ],
        out_specs=pl.BlockSpec((tm, d), lambda i: (i, 0)),
        out_shape=jax.ShapeDtypeStruct((r, d), F32),
        compiler_params=_cparams("parallel"),
        name="moe_combine",
    )(yg, yg, route, g.reshape(1, d), gate, x)


def moe_block(x, g1, shift, scale, w_router, wg, wu, wd, g2, gate, rows_per_mod, tm=512):
    n, d = x.shape
    h, route = router(x, g1, shift, scale, w_router, rows_per_mod)
    eidx = route[:, ROUTE_IDX0:ROUTE_IDX0 + TOP_K].astype(jnp.int32)
    flat_e = eidx.T.reshape(-1)
    onehot = (flat_e[:, None] == jnp.arange(N_EXPERTS, dtype=jnp.int32)[None, :]).astype(jnp.int32)
    csum = jnp.cumsum(onehot, axis=0)
    counts = csum[-1]
    rank = jnp.take_along_axis(csum, flat_e[:, None], axis=1)[:, 0] - 1
    tiles_per_e = (counts + tm - 1) // tm
    tile_end = jnp.cumsum(tiles_per_e)
    tile_start = tile_end - tiles_per_e
    slot = tile_start[flat_e] * tm + rank
    n_tiles = (TOP_K * n) // tm + N_EXPERTS
    tok = jnp.tile(jnp.arange(n, dtype=jnp.int32), TOP_K)
    src = jnp.zeros((n_tiles * tm,), jnp.int32).at[slot].set(tok)
    tile_ids = jnp.arange(n_tiles, dtype=jnp.int32)
    tile_expert = jnp.minimum(jnp.sum((tile_ids[:, None] >= tile_end[None, :]).astype(jnp.int32), axis=1),
                              N_EXPERTS - 1).astype(jnp.int32)
    n_used = tile_end[-1:].astype(jnp.int32)

    xs = gather_rows(h, src)
    ys = ffn_routed(xs, tile_expert, n_used, wg, wu, wd, tm)
    yg = gather_rows(ys, slot.astype(jnp.int32))
    return combine(yg, route, g2, gate, x, rows_per_mod)


def _gmlp_kernel(u_ref, v_ref, gv_ref, ws_ref, bs_ref, o_ref, vn_scr):
    v = v_ref[...].astype(F32)
    mu = jnp.mean(v, axis=-1, keepdims=True)
    vc = v - mu
    vn = vc * lax.rsqrt(jnp.mean(vc * vc, axis=-1, keepdims=True) + EPS) * gv_ref[...]
    vn_scr[...] = vn.astype(BF16)
    tm, width = vn_scr.shape
    gw = width // GMLP_GROUPS
    for n in range(tm // GMLP_CHUNK):
        rs = slice(n * GMLP_CHUNK, (n + 1) * GMLP_CHUNK)
        for g in range(GMLP_GROUPS):
            cs = slice(g * gw, (g + 1) * gw)
            m = _bdot(ws_ref[g], vn_scr[rs, cs]) + bs_ref[g]
            o_ref[rs, cs] = (u_ref[rs, cs].astype(F32) * m).astype(BF16)


def gmlp_spatial(hw, g_v, w_s, b_s, tm=512):
    r, w2 = hw.shape
    width = w2 // 2
    gw = width // GMLP_GROUPS
    tm = _row_tile(r, tm)
    bsb = jnp.broadcast_to(b_s[:, :, None], (GMLP_GROUPS, GMLP_CHUNK, gw)).astype(F32)
    return pl.pallas_call(
        _gmlp_kernel,
        grid=(r // tm,),
        in_specs=[
            pl.BlockSpec((tm, width), lambda i: (i, 0)),
            pl.BlockSpec((tm, width), lambda i: (i, 1)),
            pl.BlockSpec((1, width), lambda i: (0, 0)),
            pl.BlockSpec((GMLP_GROUPS, GMLP_CHUNK, GMLP_CHUNK), lambda i: (0, 0, 0)),
            pl.BlockSpec((GMLP_GROUPS, GMLP_CHUNK, gw), lambda i: (0, 0, 0)),
        ],
        out_specs=pl.BlockSpec((tm, width), lambda i: (i, 0)),
        out_shape=jax.ShapeDtypeStruct((r, width), BF16),
        scratch_shapes=[pltpu.VMEM((tm, width), BF16)],
        compiler_params=_cparams("parallel"),
        name="gmlp_spatial",
    )(hw, hw, g_v.reshape(1, width), w_s.astype(BF16), bsb)


def _halo_specs(tm, width, col_block, tile_of, n_row_blocks):
    per = tm // HALO
    cur = pl.BlockSpec((tm, width), lambda i: (tile_of(i), col_block))
    prev = pl.BlockSpec((HALO, width), lambda i: (jnp.maximum(tile_of(i) * per - 1, 0), col_block))
    nxt = pl.BlockSpec((HALO, width),
                       lambda i: (jnp.minimum((tile_of(i) + 1) * per, n_row_blocks - 1), col_block))
    return [cur, prev, nxt]


def _fill_ext(ext, cur_ref, prev_ref, next_ref, tile, tm, seq_len):
    first = (tile * tm) % seq_len == 0
    last = ((tile + 1) * tm) % seq_len == 0
    ext[0:HALO, :] = jnp.where(first, 0.0, prev_ref[...])
    ext[HALO:HALO + tm, :] = cur_ref[...]
    ext[HALO + tm:HALO + tm + HALO, :] = jnp.where(last, 0.0, next_ref[...])


def _lru_kernel(xb_ref, xp_ref, xn_ref, cw_ref, cb_ref, wax_ref, ba_ref, bx_ref, lam_ref, h0_ref,
                o_ref, ext, a_scr, b_scr, carry, *, tm, seq_len, n_tiles, reverse):
    i = pl.program_id(0)
    tile = (n_tiles - 1 - i) if reverse else i
    _fill_ext(ext, xb_ref, xp_ref, xn_ref, tile, tm, seq_len)
    kk = cw_ref.shape[0]
    left = kk // 2
    xc = cb_ref[...] + cw_ref[0:1, :] * ext[pl.ds(HALO - left, tm), :]
    for k in range(1, kk):
        xc = xc + cw_ref[k:k + 1, :] * ext[pl.ds(HALO - left + k, tm), :]

    width = xc.shape[1]
    hd = width // LRU_HEADS
    lam = lam_ref[...]
    sp = jnp.maximum(-lam, 0.0) + jnp.log(1.0 + jnp.exp(-jnp.abs(lam)))
    for hh in range(LRU_HEADS):
        cs = slice(hh * hd, (hh + 1) * hd)
        xh = xc[:, cs]
        pre = _bdot(xh.astype(BF16), wax_ref[hh])
        rg = _sigmoid(pre[:, :hd] + ba_ref[:, cs])
        ig = _sigmoid(pre[:, hd:] + bx_ref[:, cs])
        a = jnp.exp(-LRU_C * rg * sp[:, cs])
        a_scr[:, cs] = a
        b_scr[:, cs] = jnp.sqrt(1.0 - a * a) * (ig * xh)

    first = (tile * tm) % seq_len == 0
    last = ((tile + 1) * tm) % seq_len == 0

    @pl.when(last if reverse else first)
    def _():
        carry[...] = jnp.broadcast_to(h0_ref[0], carry.shape)

    row = lax.broadcasted_iota(jnp.int32, (SUBLANES, width), 0)
    nblk = tm // SUBLANES

    def body(k, c):
        blk = (nblk - 1 - k) if reverse else k
        r0 = pl.multiple_of(blk * SUBLANES, SUBLANES)
        a = a_scr[pl.ds(r0, SUBLANES), :]
        b = b_scr[pl.ds(r0, SUBLANES), :]
        for s in (1, 2, 4):
            shift = (SUBLANES - s) if reverse else s
            a_sh = pltpu.roll(a, shift, 0)
            b_sh = pltpu.roll(b, shift, 0)
            m = (row < SUBLANES - s) if reverse else (row >= s)
            b = jnp.where(m, a * b_sh + b, b)
            a = jnp.where(m, a * a_sh, a)
        h = a * c + b
        o_ref[pl.ds(r0, SUBLANES), :] = h
        edge = h[0:1, :] if reverse else h[SUBLANES - 1:SUBLANES, :]
        return jnp.broadcast_to(edge, c.shape)

    carry[...] = lax.fori_loop(0, nblk, body, carry[...])


def lru_scan(z, col_block, width, conv_w, conv_b, w_a, b_a, w_x, b_x, lam, h0, seq_len, reverse, tm=512):
    r = z.shape[0]
    tm = _row_tile(seq_len, tm)
    n_tiles = r // tm
    nb = r // seq_len
    tile_of = (lambda i: n_tiles - 1 - i) if reverse else (lambda i: i)
    wax = jnp.concatenate([w_a, w_x], axis=-1).astype(BF16)
    kk = conv_w.shape[0]
    hd = width // LRU_HEADS
    const2 = lambda i: (0, 0)
    return pl.pallas_call(
        functools.partial(_lru_kernel, tm=tm, seq_len=seq_len, n_tiles=n_tiles, reverse=reverse),
        grid=(n_tiles,),
        in_specs=_halo_specs(tm, width, col_block, tile_of, r // HALO) + [
            pl.BlockSpec((kk, width), const2),
            pl.BlockSpec((1, width), const2),
            pl.BlockSpec((LRU_HEADS, hd, 2 * hd), lambda i: (0, 0, 0)),
            pl.BlockSpec((1, width), const2),
            pl.BlockSpec((1, width), const2),
            pl.BlockSpec((1, width), const2),
            pl.BlockSpec((1, 1, width), lambda i: ((tile_of(i) * tm) // seq_len, 0, 0)),
        ],
        out_specs=pl.BlockSpec((tm, width), lambda i: (tile_of(i), 0)),
        out_shape=jax.ShapeDtypeStruct((r, width), F32),
        scratch_shapes=[
            pltpu.VMEM((tm + 2 * HALO, width), F32),
            pltpu.VMEM((tm, width), F32),
            pltpu.VMEM((tm, width), F32),
            pltpu.VMEM((SUBLANES, width), F32),
        ],
        compiler_params=_cparams("arbitrary"),
        name="lru_scan_bwd" if reverse else "lru_scan_fwd",
    )(z, z, z, conv_w, conv_b.reshape(1, width), wax, b_a.reshape(1, width), b_x.reshape(1, width),
      lam.reshape(1, width), h0.reshape(nb, 1, width))


def _lru_out_prologue(gate, hf, hb):
    return (_gelu(gate) * (hf + hb)).astype(BF16)


def _pool_kernel(p_ref, pp_ref, pn_ref, wg_ref, bg_ref, sc_ref, o_ref, ext, *, tm, seq_len):
    i = pl.program_id(0)
    _fill_ext(ext, p_ref, pp_ref, pn_ref, i, tm, seq_len)
    width = o_ref.shape[1]
    gw = width // len(POOL_WINDOWS)
    t = ((i * tm) % seq_len + lax.broadcasted_iota(jnp.int32, (tm, 1), 0))
    for g, win in enumerate(POOL_WINDOWS):
        half = win // 2
        cs = slice(g * gw, (g + 1) * gw)
        s = ext[pl.ds(HALO - half, tm), cs]
        for k in range(1 - half, half):
            s = s + ext[pl.ds(HALO + k, tm), cs]
        cnt = (jnp.minimum(t + half, seq_len) - jnp.maximum(t - half, 0)).astype(F32)
        q = s / cnt - ext[pl.ds(HALO, tm), cs]
        y = _bdot(q.astype(BF16), wg_ref[g]) + bg_ref[:, cs]
        o_ref[:, cs] = (y * sc_ref[:, cs]).astype(BF16)


def pool_mix(p, w_g, b_g, scale, seq_len, tm=512):
    r, width = p.shape
    assert max(POOL_WINDOWS) // 2 <= HALO
    tm = _row_tile(seq_len, tm)
    ng, gw, _ = w_g.shape
    const2 = lambda i: (0, 0)
    return pl.pallas_call(
        functools.partial(_pool_kernel, tm=tm, seq_len=seq_len),
        grid=(r // tm,),
        in_specs=_halo_specs(tm, width, 0, lambda i: i, r // HALO) + [
            pl.BlockSpec((ng, gw, gw), lambda i: (0, 0, 0)),
            pl.BlockSpec((1, width), const2),
            pl.BlockSpec((1, width), const2),
        ],
        out_specs=pl.BlockSpec((tm, width), lambda i: (i, 0)),
        out_shape=jax.ShapeDtypeStruct((r, width), BF16),
        scratch_shapes=[pltpu.VMEM((tm + 2 * HALO, width), F32)],
        compiler_params=_cparams("parallel"),
        name="pool_mix",
    )(p, p, p, w_g.astype(BF16), b_g.reshape(1, width), scale.reshape(1, width))


def _hyconv_kernel(z_ref, zp_ref, zn_ref, cw_ref, cb_ref, o_ref, ext, *, tm, seq_len):
    i = pl.program_id(0)
    _fill_ext(ext, z_ref, zp_ref, zn_ref, i, tm, seq_len)
    kk = cw_ref.shape[0]
    left = kk // 2
    y = cb_ref[...] + cw_ref[0:1, :] * ext[pl.ds(HALO - left, tm), :]
    for k in range(1, kk):
        y = y + cw_ref[k:k + 1, :] * ext[pl.ds(HALO - left + k, tm), :]
    o_ref[0] = y.astype(BF16)


def hyena_short_conv(z, conv_w, conv_b, width, seq_len, tm=512):
    r = z.shape[0]
    nsplit = z.shape[1] // width
    tm = _row_tile(seq_len, tm)
    kk = conv_w.shape[0]
    n_row_blocks = r // HALO
    per = tm // HALO
    return pl.pallas_call(
        functools.partial(_hyconv_kernel, tm=tm, seq_len=seq_len),
        grid=(r // tm, nsplit),
        in_specs=[
            pl.BlockSpec((tm, width), lambda i, j: (i, j)),
            pl.BlockSpec((HALO, width), lambda i, j: (jnp.maximum(i * per - 1, 0), j)),
            pl.BlockSpec((HALO, width), lambda i, j: (jnp.minimum((i + 1) * per, n_row_blocks - 1), j)),
            pl.BlockSpec((kk, width), lambda i, j: (0, j)),
            pl.BlockSpec((1, width), lambda i, j: (0, j)),
        ],
        out_specs=pl.BlockSpec((1, tm, width), lambda i, j: (j, i, 0)),
        out_shape=jax.ShapeDtypeStruct((nsplit, r, width), BF16),
        scratch_shapes=[pltpu.VMEM((tm + 2 * HALO, width), F32)],
        compiler_params=_cparams("parallel", "arbitrary"),
        name="hyena_short_conv",
    )(z, z, z, conv_w, conv_b.reshape(1, -1))


def _hyfilter_kernel(z_ref, w1_ref, b1_ref, w2_ref, b2_ref, w3_ref, b3_ref, fr_ref, wo_ref, dl_ref,
                     o_ref, *, n_out):
    z = z_ref[...]
    h = jnp.sin(fr_ref[0:1, :] * (_hdot(z, w1_ref[...]) + b1_ref[...]))
    h = jnp.sin(fr_ref[1:2, :] * (_hdot(h, w2_ref[...]) + b2_ref[...]))
    h = jnp.sin(fr_ref[2:3, :] * (_hdot(h, w3_ref[...]) + b3_ref[...]))
    window = jnp.exp(-z[:, 0:1] * dl_ref[...])
    width = dl_ref.shape[1]
    for q in range(n_out):
        cs = slice(q * width, (q + 1) * width)
        o_ref[:, cs] = _hdot(h, wo_ref[:, cs]) * window


def hyena_filters(seq_len, width, f_w1, f_b1, f_w2, f_b2, f_w3, f_b3, f_freq, f_wout, tm=512):
    t = jnp.linspace(0.0, 1.0, seq_len, dtype=F32)[:, None]
    w = 2.0 * math.pi * jnp.arange(seq_len, dtype=F32)[:, None] / seq_len
    bands = jnp.linspace(1e-4, HYENA_BANDS - 1, HYENA_BANDS, dtype=F32)[None, :]
    z = jnp.concatenate([t, jnp.cos(bands * w), jnp.sin(-bands * w)], axis=-1)
    emb = z.shape[1]
    hid = f_w1.shape[1]
    zp = jnp.zeros((seq_len, LANES), F32).at[:, :emb].set(z)
    w1p = jnp.zeros((LANES, hid), F32).at[:emb].set(f_w1)
    max_decay = math.log(HYENA_DECAY_TARGET) / HYENA_FAST_DECAY
    min_decay = math.log(HYENA_DECAY_TARGET) / HYENA_SLOW_DECAY
    deltas = jnp.abs(jnp.linspace(min_decay, max_decay, width, dtype=F32)).reshape(1, width)
    n_tot = f_wout.shape[1]
    tm = _row_tile(seq_len, tm)
    c2 = lambda i: (0, 0)
    return pl.pallas_call(
        functools.partial(_hyfilter_kernel, n_out=n_tot // width),
        grid=(seq_len // tm,),
        in_specs=[
            pl.BlockSpec((tm, LANES), lambda i: (i, 0)),
            pl.BlockSpec((LANES, hid), c2), pl.BlockSpec((1, hid), c2),
            pl.BlockSpec((hid, hid), c2), pl.BlockSpec((1, hid), c2),
            pl.BlockSpec((hid, hid), c2), pl.BlockSpec((1, hid), c2),
            pl.BlockSpec((3, hid), c2),
            pl.BlockSpec((hid, n_tot), c2),
            pl.BlockSpec((1, width), c2),
        ],
        out_specs=pl.BlockSpec((tm, n_tot), lambda i: (i, 0)),
        out_shape=jax.ShapeDtypeStruct((seq_len, n_tot), F32),
        compiler_params=_cparams("parallel"),
        name="hyena_filters",
    )(zp, w1p, f_b1.reshape(1, hid), f_w2, f_b2.reshape(1, hid), f_w3, f_b3.reshape(1, hid),
      f_freq, f_wout, deltas)


def _dft_tables(seq_len):
    n = 2 * seq_len
    p_ = FFT_P
    n1 = n // p_
    t1n = n1 // 2
    f1 = np.arange(n1)[:, None]
    t1 = np.arange(n1)[None, :]
    ang1 = 2.0 * np.pi * ((f1 * t1) % n1) / n1
    c1, s1 = np.cos(ang1), np.sin(ang1)
    w1 = np.zeros((2 * n1, 2 * t1n))
    w1[0::2, :t1n] = c1[:, :t1n]
    w1[0::2, t1n:] = s1[:, :t1n]
    w1[1::2, :t1n] = -s1[:, :t1n]
    w1[1::2, t1n:] = c1[:, :t1n]
    w1k = np.zeros((2 * n1, n1))
    w1k[0::2] = c1
    w1k[1::2] = -s1
    w1i = np.zeros((2 * t1n, 2 * n1))
    ct, st = c1.T[:t1n] / n, s1.T[:t1n] / n
    w1i[:t1n, 0::2] = ct
    w1i[:t1n, 1::2] = -st
    w1i[t1n:, 0::2] = st
    w1i[t1n:, 1::2] = ct
    f2 = np.arange(p_)[:, None]
    pp = np.arange(p_)[None, :]
    ang2 = 2.0 * np.pi * ((f2 * pp) % p_) / p_
    c2, s2 = np.cos(ang2), np.sin(ang2)
    fb = np.block([[c2, s2], [-s2, c2]])
    fbi = np.block([[c2, -s2], [s2, c2]])
    angt = 2.0 * np.pi * ((np.arange(n1)[:, None] * np.arange(p_)[None, :]) % n) / n
    lane_bcast = lambda a: jnp.broadcast_to(jnp.asarray(a, dtype=F32)[:, :, None], (n1, p_, LANES))
    as_bf = lambda a: jnp.asarray(a, dtype=F32).astype(BF16)
    return dict(n1=n1, t1n=t1n, w1=as_bf(w1), w1k=as_bf(w1k), w1i=as_bf(w1i), fb=as_bf(fb), fbi=as_bf(fbi),
                twc=lane_bcast(np.cos(angt)), tws=lane_bcast(-np.sin(angt)))


def _mm_kernel(w_ref, x_ref, o_ref):
    o_ref[...] = _bdot(w_ref[...], x_ref[...]).astype(o_ref.dtype)


def left_matmul(w, x, out_dtype, tn=4096):
    m, k = w.shape
    n = x.shape[1]
    tn = min(tn, n)
    assert n % tn == 0
    return pl.pallas_call(
        _mm_kernel,
        grid=(n // tn,),
        in_specs=[pl.BlockSpec((m, k), lambda j: (0, 0)), pl.BlockSpec((k, tn), lambda j: (0, j))],
        out_specs=pl.BlockSpec((m, tn), lambda j: (0, j)),
        out_shape=jax.ShapeDtypeStruct((m, n), out_dtype),
        compiler_params=_cparams("parallel"),
        name="dft_stage1",
    )(w, x)


def _mm_gate_kernel(w_ref, x_ref, g_ref, v_ref, b_ref, o_ref):
    y = _bdot(w_ref[...], x_ref[...])
    v = v_ref[...].astype(F32)
    o_ref[...] = (g_ref[...].astype(F32) * (y + b_ref[...] * v)).astype(o_ref.dtype)


def left_matmul_gate(w, x, gate, v, bias_row, tn=4096):
    m, k = w.shape
    n = x.shape[1]
    tn = min(tn, n)
    assert n % tn == 0
    col = lambda j: (0, j)
    return pl.pallas_call(
        _mm_gate_kernel,
        grid=(n // tn,),
        in_specs=[pl.BlockSpec((m, k), lambda j: (0, 0)), pl.BlockSpec((k, tn), col),
                  pl.BlockSpec((m, tn), col), pl.BlockSpec((m, tn), col), pl.BlockSpec((1, tn), col)],
        out_specs=pl.BlockSpec((m, tn), col),
        out_shape=jax.ShapeDtypeStruct((m, n), BF16),
        compiler_params=_cparams("parallel"),
        name="dft_inverse_stage1_gate",
    )(w, x, gate, v, bias_row)


def _twiddle(re, im, tc, ts, reps):
    tc = jnp.tile(tc, (1, reps))
    ts = jnp.tile(ts, (1, reps))
    return re * tc - im * ts, re * ts + im * tc


def _spec_fwd_kernel(a_ref, tc_ref, ts_ref, fb_ref, o_ref):
    reps = a_ref.shape[-1] // LANES
    tr, ti = _twiddle(a_ref[0, 0].astype(F32), a_ref[0, 1].astype(F32), tc_ref[0], ts_ref[0], reps)
    rhs = jnp.concatenate([tr, ti], axis=0).astype(BF16)
    o_ref[0] = _bdot(fb_ref[...], rhs).astype(o_ref.dtype)


def _spec_conv_kernel(a_ref, tc_ref, ts_ref, fb_ref, fbi_ref, k_ref, o_ref):
    p_ = FFT_P
    reps = a_ref.shape[-1] // LANES
    tc, ts = tc_ref[0], ts_ref[0]
    tr, ti = _twiddle(a_ref[0, 0].astype(F32), a_ref[0, 1].astype(F32), tc, ts, reps)
    x = _bdot(fb_ref[...], jnp.concatenate([tr, ti], axis=0).astype(BF16))
    xr, xi = x[:p_], x[p_:]
    kr, ki = k_ref[0, :p_].astype(F32), k_ref[0, p_:].astype(F32)
    yr = xr * kr - xi * ki
    yi = xr * ki + xi * kr
    bv = _bdot(fbi_ref[...], jnp.concatenate([yr, yi], axis=0).astype(BF16))
    orr, oi = _twiddle(bv[:p_], bv[p_:], tc, -ts, reps)
    o_ref[0, 0] = orr.astype(o_ref.dtype)
    o_ref[0, 1] = oi.astype(o_ref.dtype)


def spectrum_forward(a4, tab, ct=1024):
    n1, _, p_, c = a4.shape
    ct = min(ct, c)
    return pl.pallas_call(
        _spec_fwd_kernel,
        grid=(c // ct, n1),
        in_specs=[
            pl.BlockSpec((1, 2, p_, ct), lambda j, f: (f, 0, 0, j)),
            pl.BlockSpec((1, p_, LANES), lambda j, f: (f, 0, 0)),
            pl.BlockSpec((1, p_, LANES), lambda j, f: (f, 0, 0)),
            pl.BlockSpec((2 * p_, 2 * p_), lambda j, f: (0, 0)),
        ],
        out_specs=pl.BlockSpec((1, 2 * p_, ct), lambda j, f: (f, 0, j)),
        out_shape=jax.ShapeDtypeStruct((n1, 2 * p_, c), BF16),
        compiler_params=_cparams("parallel", "parallel"),
        name="dft_filter_stage2",
    )(a4, tab["twc"], tab["tws"], tab["fb"])


def spectrum_conv(a4, kspec, k_col_block, tab):
    n1, _, p_, c = a4.shape
    return pl.pallas_call(
        _spec_conv_kernel,
        grid=(n1,),
        in_specs=[
            pl.BlockSpec((1, 2, p_, c), lambda f: (f, 0, 0, 0)),
            pl.BlockSpec((1, p_, LANES), lambda f: (f, 0, 0)),
            pl.BlockSpec((1, p_, LANES), lambda f: (f, 0, 0)),
            pl.BlockSpec((2 * p_, 2 * p_), lambda f: (0, 0)),
            pl.BlockSpec((2 * p_, 2 * p_), lambda f: (0, 0)),
            pl.BlockSpec((1, 2 * p_, c), lambda f: (f, 0, k_col_block)),
        ],
        out_specs=pl.BlockSpec((1, 2, p_, c), lambda f: (f, 0, 0, 0)),
        out_shape=jax.ShapeDtypeStruct(a4.shape, BF16),
        compiler_params=_cparams("parallel"),
        name="dft_stage2_conv",
    )(a4, tab["twc"], tab["tws"], tab["fb"], tab["fbi"], kspec)


def hyena_long_convs(xv, filt, bias, batch, seq_len):
    assert batch == 2, "the two sequences of the batch are packed as one complex sequence"
    c = xv.shape[-1]
    tab = _dft_tables(seq_len)
    n1, t1n, p_ = tab["n1"], tab["t1n"], FFT_P
    order = bias.shape[0]
    f4 = filt.reshape(seq_len, order, 2, c)
    kern = jnp.concatenate([f4[:, :, 0], jnp.zeros((1, order, c), F32), f4[:0:-1, :, 1]], axis=0)
    kern = kern.reshape(n1, p_ * order * c).astype(BF16)
    ak = left_matmul(tab["w1k"], kern, BF16)
    kspec = spectrum_forward(ak.reshape(n1, 2, p_, order * c), tab)
    rl = lambda a: a.reshape(batch * t1n, p_ * c)
    y = rl(xv[2])
    for o in range(order):
        a = left_matmul(tab["w1"], y, BF16)
        b4 = spectrum_conv(a.reshape(n1, 2, p_, c), kspec, o, tab)
        brow = jnp.tile(bias[o].astype(F32), p_).reshape(1, p_ * c)
        y = left_matmul_gate(tab["w1i"], b4.reshape(2 * n1, p_ * c), rl(xv[o]), y, brow)
    return y.reshape(batch * seq_len, c)


def kernel(x, c, ctx, c_ctx, ada_w, ada_b, norm_g, gmlp_w_in, gmlp_g_v, gmlp_w_s, gmlp_b_s, gmlp_w_out, lru_w_in, lru_conv_w, lru_conv_b, lru_w_a, lru_b_a, lru_w_x, lru_b_x, lru_lam, lru_w_out, hyena_w_in, hyena_conv_w, hyena_conv_b, hyena_f_w1, hyena_f_b1, hyena_f_w2, hyena_f_b2, hyena_f_w3, hyena_f_b3, hyena_f_freq, hyena_f_wout, hyena_bias, hyena_w_out, pool_w_in, pool_w_g, pool_b_g, pool_scale, pool_w_out, ffn_w_gate, ffn_w_up, ffn_w_down, moe_w_router, moe_w_gate, moe_w_up, moe_w_down):
    B, L, D = x.shape
    Lc = ctx.shape[1]
    depth = ada_w.shape[0]
    n_mixers = 4
    assert B + 1 <= SUBLANES

    cc = jnp.zeros((SUBLANES, D), F32).at[:B].set(c).at[B].set(c_ctx)
    ada = ada_all(cc, ada_w, ada_b).reshape(depth, SUBLANES, 6, D)

    xs = add_pos(x)
    xc = ctx.reshape(B * Lc, D)
    last_ctx = max([i for i in range(depth) if i % n_mixers == 1], default=-1)

    for i in range(depth):
        kind, j = i % n_mixers, i // n_mixers
        update_ctx = i < last_ctx
        read_ctx = i <= last_ctx
        lat = [ada[i, :B, q].reshape(B, 1, D) for q in range(6)]
        cx = [jnp.broadcast_to(ada[i, B:B + 1, q].reshape(1, 1, D), (B, 1, D)) for q in range(6)]
        g = norm_g[i]
        streams = [(xs, lat, L)]
        if read_ctx:
            streams.append((xc, cx, Lc))

        if kind == 0:
            outs = []
            for (s, m, sl) in streams[:1 + int(update_ctx)]:
                hw = norm_mod_matmul(s, g[0], m[0], m[1], gmlp_w_in[j], sl, BF16, act="gelu")
                a = gmlp_spatial(hw, gmlp_g_v[j], gmlp_w_s[j], gmlp_b_s[j])
                outs.append(matmul_norm_res([(a, a.shape[1], 0)], _identity_bf16, gmlp_w_out[j],
                                            g[1], m[2], s, sl))
        elif kind == 1:
            W = lru_w_in.shape[2] // 2
            sc_args = lambda d: (lru_conv_w[j], lru_conv_b[j], lru_w_a[j, d], lru_b_a[j, d],
                                 lru_w_x[j, d], lru_b_x[j, d], lru_lam[j, d])
            zc = norm_mod_matmul(xc, g[0], cx[0], cx[1], lru_w_in[j], Lc, F32)
            zero = jnp.zeros((B, W), F32)
            hf_c = lru_scan(zc, 1, W, *sc_args(0), zero, Lc, False)
            hb_c = lru_scan(zc, 1, W, *sc_args(1), zero, Lc, True)
            zl = norm_mod_matmul(xs, g[0], lat[0], lat[1], lru_w_in[j], L, F32)
            hf = lru_scan(zl, 1, W, *sc_args(0), hf_c.reshape(B, Lc, W)[:, -1], L, False)
            hb = lru_scan(zl, 1, W, *sc_args(1), hb_c.reshape(B, Lc, W)[:, 0], L, True)
            outs = [matmul_norm_res([(zl, W, 0), (hf, W, 0), (hb, W, 0)], _lru_out_prologue,
                                    lru_w_out[j], g[1], lat[2], xs, L)]
            if update_ctx:
                outs.append(matmul_norm_res([(zc, W, 0), (hf_c, W, 0), (hb_c, W, 0)], _lru_out_prologue,
                                            lru_w_out[j], g[1], cx[2], xc, Lc))
        elif kind == 2:
            W = hyena_w_out.shape[1]
            filt = hyena_filters(L, W, hyena_f_w1[j], hyena_f_b1[j], hyena_f_w2[j], hyena_f_b2[j],
                                 hyena_f_w3[j], hyena_f_b3[j], hyena_f_freq[j], hyena_f_wout[j])
            outs = []
            for (s, m, sl) in streams[:1 + int(update_ctx)]:
                nb = s.shape[0] // sl
                z = norm_mod_matmul(s, g[0], m[0], m[1], hyena_w_in[j], sl, F32, tn=1024)
                xv = hyena_short_conv(z, hyena_conv_w[j], hyena_conv_b[j], W, sl)
                fl = filt if sl == L else hyena_filters(
                    sl, W, hyena_f_w1[j], hyena_f_b1[j], hyena_f_w2[j], hyena_f_b2[j],
                    hyena_f_w3[j], hyena_f_b3[j], hyena_f_freq[j], hyena_f_wout[j])
                y = hyena_long_convs(xv, fl, hyena_bias[j], nb, sl)
                outs.append(matmul_norm_res([(y, W, 0)], _identity_bf16, hyena_w_out[j], g[1], m[2], s, sl))
        else:
            outs = []
            for (s, m, sl) in streams[:1 + int(update_ctx)]:
                p = norm_mod_matmul(s, g[0], m[0], m[1], pool_w_in[j], sl, F32)
                a = pool_mix(p, pool_w_g[j], pool_b_g[j], pool_scale[j], sl)
                outs.append(matmul_norm_res([(a, a.shape[1], 0)], _identity_bf16, pool_w_out[j],
                                            g[1], m[2], s, sl))
        xs = outs[0]
        if update_ctx:
            xc = outs[1]

        k = i // 2
        todo = [(xs, lat, L)] + ([(xc, cx, Lc)] if update_ctx else [])
        res = []
        for (s, m, sl) in todo:
            if i % 2 == 0:
                res.append(ffn_dense(s, g[2], m[3], m[4], ffn_w_gate[k], ffn_w_up[k], ffn_w_down[k],
                                     g[3], m[5], sl))
            else:
                res.append(moe_block(s, g[2], m[3], m[4], moe_w_router[k], moe_w_gate[k], moe_w_up[k],
                                     moe_w_down[k], g[3], m[5], sl))
        xs = res[0]
        if update_ctx:
            xc = res[1]
    return xs.reshape(B, L, D)
```

```python
import functools
import math

import jax
import jax.numpy as jnp
import numpy as np
from jax import lax
from jax.experimental import pallas as pl
from jax.experimental.pallas import tpu as pltpu

F32 = jnp.float32
BF16 = jnp.bfloat16
EPS = 1e-6

VMEM_LIMIT_BYTES = 52 * 1024 * 1024
LANES = 128
SUBLANES = 8

GRID_W = 64
GMLP_CHUNK = 128
GMLP_GROUPS = 8
LRU_HEADS = 8
LRU_C = 8.0
POOL_WINDOWS = (2, 4, 8, 16)
HYENA_BANDS = 16
HYENA_FAST_DECAY = 0.3
HYENA_SLOW_DECAY = 1.5
HYENA_DECAY_TARGET = 1e-2
N_EXPERTS = 8
TOP_K = 2
FFT_P = 128
HALO = SUBLANES


def _cparams(*sem):
    return pltpu.CompilerParams(dimension_semantics=sem, vmem_limit_bytes=VMEM_LIMIT_BYTES)


def _rms(x, g):
    return x * lax.rsqrt(jnp.mean(x * x, axis=-1, keepdims=True) + EPS) * g


def _gelu(x):
    return 0.5 * x * (1.0 + jnp.tanh(math.sqrt(2.0 / math.pi) * (x + 0.044715 * (x * x * x))))


def _silu(x):
    return x * (1.0 / (1.0 + jnp.exp(-x)))


def _sigmoid(x):
    return 1.0 / (1.0 + jnp.exp(-x))


def _bdot(a, b):
    return jnp.dot(a, b, preferred_element_type=F32)


def _hdot(a, b):
    return jnp.dot(a, b, preferred_element_type=F32, precision=lax.Precision.HIGHEST)


def _to_row_tiles(val, ref):
    rows = val.shape[0]
    nq = val.shape[1] // LANES
    for q in range(nq):
        ref[pl.ds(q, rows, stride=nq), :] = val[:, q * LANES:(q + 1) * LANES].astype(ref.dtype)


def _from_row_tiles(ref, dst_ref):
    rows = dst_ref.shape[0]
    nq = dst_ref.shape[1] // LANES
    for q in range(nq):
        dst_ref[:, q * LANES:(q + 1) * LANES] = ref[pl.ds(q, rows, stride=nq), :].astype(dst_ref.dtype)


def _row_tile(rows, want):
    t = min(rows, want)
    assert rows % t == 0, (rows, t)
    return t


def _ada_kernel(c_ref, w_ref, b_ref, o_ref):
    o_ref[0] = _hdot(_silu(c_ref[...]), w_ref[0]) + b_ref[0]


def ada_all(cc, ada_w, ada_b):
    depth, d, d6 = ada_w.shape
    nchunk = d6 // d
    return pl.pallas_call(
        _ada_kernel,
        grid=(depth, nchunk),
        in_specs=[
            pl.BlockSpec((SUBLANES, d), lambda i, j: (0, 0)),
            pl.BlockSpec((1, d, d), lambda i, j: (i, 0, j)),
            pl.BlockSpec((1, 1, d), lambda i, j: (i, 0, j)),
        ],
        out_specs=pl.BlockSpec((1, SUBLANES, d), lambda i, j: (i, 0, j)),
        out_shape=jax.ShapeDtypeStruct((depth, SUBLANES, d6), F32),
        compiler_params=_cparams("parallel", "parallel"),
        name="ada",
    )(cc, ada_w, ada_b.reshape(depth, 1, d6))


def _pos_kernel(x_ref, rt_ref, ct_ref, o_ref):
    half = rt_ref.shape[-1]
    x = x_ref[...]
    o_ref[:, :, :half] = x[:, :, :half] + rt_ref[...]
    o_ref[:, :, half:] = x[:, :, half:] + ct_ref[...][None]


def add_pos(x):
    b, l, d = x.shape
    rows = l // GRID_W
    quarter = d // 4
    omega = 1.0 / (10000.0 ** (jnp.arange(quarter, dtype=F32) / quarter))

    def sincos(p):
        ang = p.reshape(-1, 1) * omega[None, :]
        return jnp.concatenate([jnp.sin(ang), jnp.cos(ang)], axis=-1)

    rtab = sincos(jnp.arange(rows, dtype=F32)).reshape(rows, 1, 2 * quarter)
    ctab = sincos(jnp.arange(GRID_W, dtype=F32))
    x3 = x.reshape(b * rows, GRID_W, d)
    tr = _row_tile(rows, 16)
    nrt = rows // tr
    out = pl.pallas_call(
        _pos_kernel,
        grid=(b * nrt,),
        in_specs=[
            pl.BlockSpec((tr, GRID_W, d), lambda i: (i, 0, 0)),
            pl.BlockSpec((tr, 1, 2 * quarter), lambda i: (i % nrt, 0, 0)),
            pl.BlockSpec((GRID_W, 2 * quarter), lambda i: (0, 0)),
        ],
        out_specs=pl.BlockSpec((tr, GRID_W, d), lambda i: (i, 0, 0)),
        out_shape=jax.ShapeDtypeStruct(x3.shape, F32),
        compiler_params=_cparams("parallel"),
        name="add_pos",
    )(x3, rtab, ctab)
    return out.reshape(b * l, d)


def _nmm_kernel(x_ref, g_ref, sh_ref, sc_ref, w_ref, o_ref, h_scr, *, act):
    @pl.when(pl.program_id(1) == 0)
    def _():
        h = _rms(x_ref[...], g_ref[...]) * (1.0 + sc_ref[0]) + sh_ref[0]
        h_scr[...] = h.astype(BF16)

    y = _bdot(h_scr[...], w_ref[...])
    if act == "gelu":
        y = _gelu(y)
    o_ref[...] = y.astype(o_ref.dtype)


def norm_mod_matmul(x, g, shift, scale, w, rows_per_mod, out_dtype, act=None, tm=512, tn=2048):
    r, d = x.shape
    n = w.shape[1]
    tm = _row_tile(min(r, rows_per_mod), tm)
    tn = min(tn, n)
    assert n % tn == 0
    mod_map = lambda i, j: ((i * tm) // rows_per_mod, 0, 0)
    return pl.pallas_call(
        functools.partial(_nmm_kernel, act=act),
        grid=(r // tm, n // tn),
        in_specs=[
            pl.BlockSpec((tm, d), lambda i, j: (i, 0)),
            pl.BlockSpec((1, d), lambda i, j: (0, 0)),
            pl.BlockSpec((1, 1, d), mod_map),
            pl.BlockSpec((1, 1, d), mod_map),
            pl.BlockSpec((d, tn), lambda i, j: (0, j)),
        ],
        out_specs=pl.BlockSpec((tm, tn), lambda i, j: (i, j)),
        out_shape=jax.ShapeDtypeStruct((r, n), out_dtype),
        scratch_shapes=[pltpu.VMEM((tm, d), BF16)],
        compiler_params=_cparams("parallel", "arbitrary"),
        name="norm_mod_matmul",
    )(x, g.reshape(1, d), shift, scale, w.astype(BF16))


def _mnr_kernel(*refs, n_in, prologue):
    in_refs = refs[:n_in]
    w_ref, g_ref, gate_ref, x_ref, o_ref = refs[n_in:]
    a = prologue(*[r[...] for r in in_refs])
    y = _bdot(a, w_ref[...])
    o_ref[...] = x_ref[...] + gate_ref[0] * _rms(y, g_ref[...])


def matmul_norm_res(inputs, prologue, w, g, gate, x, rows_per_mod, tm=512):
    r, d = x.shape
    k = w.shape[0]
    tm = _row_tile(min(r, rows_per_mod), tm)
    mod_map = lambda i: ((i * tm) // rows_per_mod, 0, 0)
    in_specs = [pl.BlockSpec((tm, wd), functools.partial(lambda i, cb: (i, cb), cb=cb))
                for (_, wd, cb) in inputs]
    in_specs += [
        pl.BlockSpec((k, d), lambda i: (0, 0)),
        pl.BlockSpec((1, d), lambda i: (0, 0)),
        pl.BlockSpec((1, 1, d), mod_map),
        pl.BlockSpec((tm, d), lambda i: (i, 0)),
    ]
    return pl.pallas_call(
        functools.partial(_mnr_kernel, n_in=len(inputs), prologue=prologue),
        grid=(r // tm,),
        in_specs=in_specs,
        out_specs=pl.BlockSpec((tm, d), lambda i: (i, 0)),
        out_shape=jax.ShapeDtypeStruct((r, d), F32),
        compiler_params=_cparams("parallel"),
        name="matmul_norm_res",
    )(*[a for (a, _, _) in inputs], w.astype(BF16), g.reshape(1, d), gate, x)


def _identity_bf16(a):
    return a.astype(BF16)


def _swiglu_step(h_scr, wg_ref, wu_ref, wd_ref, acc_scr, j):
    h = h_scr[...]
    t = (_silu(_bdot(h, wg_ref[...])) * _bdot(h, wu_ref[...])).astype(BF16)
    part = _bdot(t, wd_ref[...])

    @pl.when(j == 0)
    def _():
        acc_scr[...] = part

    @pl.when(j > 0)
    def _():
        acc_scr[...] += part


def _ffn_dense_kernel(x_ref, g1_ref, sh_ref, sc_ref, wg_ref, wu_ref, wd_ref, g2_ref, gate_ref,
                      o_ref, h_scr, acc_scr):
    j = pl.program_id(1)

    @pl.when(j == 0)
    def _():
        h = _rms(x_ref[...], g1_ref[...]) * (1.0 + sc_ref[0]) + sh_ref[0]
        h_scr[...] = h.astype(BF16)

    _swiglu_step(h_scr, wg_ref, wu_ref, wd_ref, acc_scr, j)

    @pl.when(j == pl.num_programs(1) - 1)
    def _():
        o_ref[...] = x_ref[...] + gate_ref[0] * _rms(acc_scr[...], g2_ref[...])


def ffn_dense(x, g1, shift, scale, wg, wu, wd, g2, gate, rows_per_mod, tm=512, tf=1408):
    r, d = x.shape
    ff = wg.shape[1]
    tm = _row_tile(min(r, rows_per_mod), tm)
    assert ff % tf == 0
    mod_map = lambda i, j: ((i * tm) // rows_per_mod, 0, 0)
    return pl.pallas_call(
        _ffn_dense_kernel,
        grid=(r // tm, ff // tf),
        in_specs=[
            pl.BlockSpec((tm, d), lambda i, j: (i, 0)),
            pl.BlockSpec((1, d), lambda i, j: (0, 0)),
            pl.BlockSpec((1, 1, d), mod_map),
            pl.BlockSpec((1, 1, d), mod_map),
            pl.BlockSpec((d, tf), lambda i, j: (0, j)),
            pl.BlockSpec((d, tf), lambda i, j: (0, j)),
            pl.BlockSpec((tf, d), lambda i, j: (j, 0)),
            pl.BlockSpec((1, d), lambda i, j: (0, 0)),
            pl.BlockSpec((1, 1, d), mod_map),
        ],
        out_specs=pl.BlockSpec((tm, d), lambda i, j: (i, 0)),
        out_shape=jax.ShapeDtypeStruct((r, d), F32),
        scratch_shapes=[pltpu.VMEM((tm, d), BF16), pltpu.VMEM((tm, d), F32)],
        compiler_params=_cparams("parallel", "arbitrary"),
        name="ffn_dense",
    )(x, g1.reshape(1, d), shift, scale, wg.astype(BF16), wu.astype(BF16), wd.astype(BF16),
      g2.reshape(1, d), gate)


def _ffn_routed_kernel(te_ref, nu_ref, x_ref, wg_ref, wu_ref, wd_ref, o_ref, h_scr, acc_scr):
    i = pl.program_id(0)
    j = pl.program_id(1)
    last = pl.num_programs(1) - 1
    used = i < nu_ref[0]

    @pl.when(used)
    def _():
        @pl.when(j == 0)
        def _():
            _from_row_tiles(x_ref, h_scr)

        _swiglu_step(h_scr, wg_ref.at[0], wu_ref.at[0], wd_ref.at[0], acc_scr, j)

        @pl.when(j == last)
        def _():
            _to_row_tiles(acc_scr[...], o_ref)

    @pl.when(jnp.logical_and(jnp.logical_not(used), j == last))
    def _():
        o_ref[...] = jnp.zeros_like(o_ref)


def ffn_routed(xs, tile_expert, n_used, wg, wu, wd, tm, tf=512):
    dq = wg.shape[1] // LANES
    d = dq * LANES
    p = xs.shape[0] // dq
    ff = wg.shape[2]
    assert ff % tf == 0 and p % tm == 0
    grid_spec = pltpu.PrefetchScalarGridSpec(
        num_scalar_prefetch=2,
        grid=(p // tm, ff // tf),
        in_specs=[
            pl.BlockSpec((tm * dq, LANES), lambda i, j, te, nu: (i, 0)),
            pl.BlockSpec((1, d, tf), lambda i, j, te, nu: (te[i], 0, j)),
            pl.BlockSpec((1, d, tf), lambda i, j, te, nu: (te[i], 0, j)),
            pl.BlockSpec((1, tf, d), lambda i, j, te, nu: (te[i], j, 0)),
        ],
        out_specs=pl.BlockSpec((tm * dq, LANES), lambda i, j, te, nu: (i, 0)),
        scratch_shapes=[pltpu.VMEM((tm, d), BF16), pltpu.VMEM((tm, d), F32)],
    )
    return pl.pallas_call(
        _ffn_routed_kernel,
        grid_spec=grid_spec,
        out_shape=jax.ShapeDtypeStruct((p * dq, LANES), F32),
        compiler_params=_cparams("arbitrary", "arbitrary"),
        name="ffn_routed",
    )(tile_expert, n_used, xs, wg.astype(BF16), wu.astype(BF16), wd.astype(BF16))


ROUTE_IDX0 = N_EXPERTS
ROUTE_P0 = N_EXPERTS + TOP_K


def _router_kernel(x_ref, g_ref, sh_ref, sc_ref, wr_ref, h_ref, r_ref):
    h = _rms(x_ref[...], g_ref[...]) * (1.0 + sc_ref[0]) + sh_ref[0]
    _to_row_tiles(h, h_ref)
    logits = _hdot(h, wr_ref[...])
    lane = lax.broadcasted_iota(jnp.int32, logits.shape, 1)
    neg = jnp.float32(-jnp.inf)
    big = jnp.int32(LANES)
    lg = jnp.where(lane < N_EXPERTS, logits, neg)
    m1 = jnp.max(lg, axis=-1, keepdims=True)
    i1 = jnp.min(jnp.where(lg == m1, lane, big), axis=-1, keepdims=True)
    lg2 = jnp.where(lane == i1, neg, lg)
    m2 = jnp.max(lg2, axis=-1, keepdims=True)
    i2 = jnp.min(jnp.where(lg2 == m2, lane, big), axis=-1, keepdims=True)
    e2 = jnp.exp(m2 - m1)
    p1 = 1.0 / (1.0 + e2)
    p2 = e2 / (1.0 + e2)
    out = jnp.where(lane == ROUTE_IDX0, i1.astype(F32), 0.0)
    out = jnp.where(lane == ROUTE_IDX0 + 1, i2.astype(F32), out)
    out = jnp.where(lane == ROUTE_P0, p1, out)
    out = jnp.where(lane == ROUTE_P0 + 1, p2, out)
    r_ref[...] = out


def router(x, g, shift, scale, w_router, rows_per_mod, tm=512):
    r, d = x.shape
    tm = _row_tile(min(r, rows_per_mod), tm)
    wr = jnp.zeros((d, LANES), F32).at[:, :N_EXPERTS].set(w_router)
    mod_map = lambda i: ((i * tm) // rows_per_mod, 0, 0)
    return pl.pallas_call(
        _router_kernel,
        grid=(r // tm,),
        in_specs=[
            pl.BlockSpec((tm, d), lambda i: (i, 0)),
            pl.BlockSpec((1, d), lambda i: (0, 0)),
            pl.BlockSpec((1, 1, d), mod_map),
            pl.BlockSpec((1, 1, d), mod_map),
            pl.BlockSpec((d, LANES), lambda i: (0, 0)),
        ],
        out_specs=[pl.BlockSpec((tm * (d // LANES), LANES), lambda i: (i, 0)),
                   pl.BlockSpec((tm, LANES), lambda i: (i, 0))],
        out_shape=[jax.ShapeDtypeStruct((r * (d // LANES), LANES), F32), jax.ShapeDtypeStruct((r, LANES), F32)],
        compiler_params=_cparams("parallel"),
        name="router",
    )(x, g.reshape(1, d), shift, scale, wr)


def _gather_kernel(idx_ref, src_ref, o_ref, sem, *, rows, nq):
    def issue(r, carry):
        s0 = pl.multiple_of(idx_ref[r] * nq, nq)
        d0 = pl.multiple_of(r * nq, nq)
        pltpu.make_async_copy(src_ref.at[pl.ds(s0, nq)], o_ref.at[pl.ds(d0, nq)], sem).start()
        return carry

    lax.fori_loop(0, rows, issue, 0)
    pltpu.make_async_copy(src_ref.at[pl.ds(0, rows * nq)], o_ref, sem).wait()


def gather_rows(src, idx, nq, rows_per_step=512):
    n = idx.shape[0]
    rows = _row_tile(n, rows_per_step)
    return pl.pallas_call(
        functools.partial(_gather_kernel, rows=rows, nq=nq),
        grid=(n // rows,),
        in_specs=[
            pl.BlockSpec((rows,), lambda i: (i,), memory_space=pltpu.SMEM),
            pl.BlockSpec(memory_space=pl.ANY),
        ],
        out_specs=pl.BlockSpec((rows * nq, LANES), lambda i: (i, 0)),
        out_shape=jax.ShapeDtypeStruct((n * nq, LANES), src.dtype),
        scratch_shapes=[pltpu.SemaphoreType.DMA(())],
        compiler_params=_cparams("arbitrary"),
        name="gather_rows",
    )(idx, src)


def _combine_kernel(y0_ref, y1_ref, r_ref, g_ref, gate_ref, x_ref, o_ref, y_scr):
    rt = r_ref[...]
    p0 = rt[:, ROUTE_P0:ROUTE_P0 + 1]
    p1 = rt[:, ROUTE_P0 + 1:ROUTE_P0 + 2]
    _from_row_tiles(y0_ref, y_scr)
    y = p0 * y_scr[...]
    _from_row_tiles(y1_ref, y_scr)
    y = y + p1 * y_scr[...]
    o_ref[...] = x_ref[...] + gate_ref[0] * _rms(y, g_ref[...])


def combine(yg, route, g, gate, x, rows_per_mod, tm=512):
    r, d = x.shape
    tm = _row_tile(min(r, rows_per_mod), tm)
    nt = r // tm
    mod_map = lambda i: ((i * tm) // rows_per_mod, 0, 0)
    return pl.pallas_call(
        _combine_kernel,
        grid=(nt,),
        in_specs=[
            pl.BlockSpec((tm * (d // LANES), LANES), lambda i: (i, 0)),
            pl.BlockSpec((tm * (d // LANES), LANES), lambda i: (i + nt, 0)),
            pl.BlockSpec((tm, LANES), lambda i: (i, 0)),
            pl.BlockSpec((1, d), lambda i: (0, 0)),
            pl.BlockSpec((1, 1, d), mod_map),
            pl.BlockSpec((tm, d), lambda i: (i, 0)),
        ],
        out_specs=pl.BlockSpec((tm, d), lambda i: (i, 0)),
        out_shape=jax.ShapeDtypeStruct((r, d), F32),
        scratch_shapes=[pltpu.VMEM((tm, d), F32)],
        compiler_params=_cparams("parallel"),
        name="moe_combine",
    )(yg, yg, route, g.reshape(1, d), gate, x)


def moe_block(x, g1, shift, scale, w_router, wg, wu, wd, g2, gate, rows_per_mod, tm=512):
    n, d = x.shape
    h, route = router(x, g1, shift, scale, w_router, rows_per_mod)
    eidx = route[:, ROUTE_IDX0:ROUTE_IDX0 + TOP_K].astype(jnp.int32)
    flat_e = eidx.T.reshape(-1)
    onehot = (flat_e[:, None] == jnp.arange(N_EXPERTS, dtype=jnp.int32)[None, :]).astype(jnp.int32)
    csum = jnp.cumsum(onehot, axis=0)
    counts = csum[-1]
    rank = jnp.take_along_axis(csum, flat_e[:, None], axis=1)[:, 0] - 1
    tiles_per_e = (counts + tm - 1) // tm
    tile_end = jnp.cumsum(tiles_per_e)
    tile_start = tile_end - tiles_per_e
    slot = tile_start[flat_e] * tm + rank
    n_tiles = (TOP_K * n) // tm + N_EXPERTS
    tok = jnp.tile(jnp.arange(n, dtype=jnp.int32), TOP_K)
    src = jnp.zeros((n_tiles * tm,), jnp.int32).at[slot].set(tok)
    tile_ids = jnp.arange(n_tiles, dtype=jnp.int32)
    tile_expert = jnp.minimum(jnp.sum((tile_ids[:, None] >= tile_end[None, :]).astype(jnp.int32), axis=1),
                              N_EXPERTS - 1).astype(jnp.int32)
    n_used = tile_end[-1:].astype(jnp.int32)

    nq = d // LANES
    xs = gather_rows(h, src, nq)
    ys = ffn_routed(xs, tile_expert, n_used, wg, wu, wd, tm)
    yg = gather_rows(ys, slot.astype(jnp.int32), nq)
    return combine(yg, route, g2, gate, x, rows_per_mod)


def _gmlp_kernel(u_ref, v_ref, gv_ref, ws_ref, bs_ref, o_ref, vn_scr):
    v = v_ref[...].astype(F32)
    mu = jnp.mean(v, axis=-1, keepdims=True)
    vc = v - mu
    vn = vc * lax.rsqrt(jnp.mean(vc * vc, axis=-1, keepdims=True) + EPS) * gv_ref[...]
    vn_scr[...] = vn.astype(BF16)
    tm, width = vn_scr.shape
    gw = width // GMLP_GROUPS
    for n in range(tm // GMLP_CHUNK):
        rs = slice(n * GMLP_CHUNK, (n + 1) * GMLP_CHUNK)
        for g in range(GMLP_GROUPS):
            cs = slice(g * gw, (g + 1) * gw)
            m = _bdot(ws_ref[g], vn_scr[rs, cs]) + bs_ref[g]
            o_ref[rs, cs] = (u_ref[rs, cs].astype(F32) * m).astype(BF16)


def gmlp_spatial(hw, g_v, w_s, b_s, tm=512):
    r, w2 = hw.shape
    width = w2 // 2
    gw = width // GMLP_GROUPS
    tm = _row_tile(r, tm)
    bsb = jnp.broadcast_to(b_s[:, :, None], (GMLP_GROUPS, GMLP_CHUNK, gw)).astype(F32)
    return pl.pallas_call(
        _gmlp_kernel,
        grid=(r // tm,),
        in_specs=[
            pl.BlockSpec((tm, width), lambda i: (i, 0)),
            pl.BlockSpec((tm, width), lambda i: (i, 1)),
            pl.BlockSpec((1, width), lambda i: (0, 0)),
            pl.BlockSpec((GMLP_GROUPS, GMLP_CHUNK, GMLP_CHUNK), lambda i: (0, 0, 0)),
            pl.BlockSpec((GMLP_GROUPS, GMLP_CHUNK, gw), lambda i: (0, 0, 0)),
        ],
        out_specs=pl.BlockSpec((tm, width), lambda i: (i, 0)),
        out_shape=jax.ShapeDtypeStruct((r, width), BF16),
        scratch_shapes=[pltpu.VMEM((tm, width), BF16)],
        compiler_params=_cparams("parallel"),
        name="gmlp_spatial",
    )(hw, hw, g_v.reshape(1, width), w_s.astype(BF16), bsb)


def _halo_specs(tm, width, col_block, tile_of, n_row_blocks):
    per = tm // HALO
    cur = pl.BlockSpec((tm, width), lambda i: (tile_of(i), col_block))
    prev = pl.BlockSpec((HALO, width), lambda i: (jnp.maximum(tile_of(i) * per - 1, 0), col_block))
    nxt = pl.BlockSpec((HALO, width),
                       lambda i: (jnp.minimum((tile_of(i) + 1) * per, n_row_blocks - 1), col_block))
    return [cur, prev, nxt]


def _fill_ext(ext, cur_ref, prev_ref, next_ref, tile, tm, seq_len):
    first = (tile * tm) % seq_len == 0
    last = ((tile + 1) * tm) % seq_len == 0
    ext[0:HALO, :] = jnp.where(first, 0.0, prev_ref[...])
    ext[HALO:HALO + tm, :] = cur_ref[...]
    ext[HALO + tm:HALO + tm + HALO, :] = jnp.where(last, 0.0, next_ref[...])


def _lru_kernel(xb_ref, xp_ref, xn_ref, cw_ref, cb_ref, wax_ref, ba_ref, bx_ref, lam_ref, h0_ref,
                o_ref, ext, a_scr, b_scr, carry, *, tm, seq_len, n_tiles, reverse):
    i = pl.program_id(0)
    tile = (n_tiles - 1 - i) if reverse else i
    _fill_ext(ext, xb_ref, xp_ref, xn_ref, tile, tm, seq_len)
    kk = cw_ref.shape[0]
    left = kk // 2
    xc = cb_ref[...] + cw_ref[0:1, :] * ext[pl.ds(HALO - left, tm), :]
    for k in range(1, kk):
        xc = xc + cw_ref[k:k + 1, :] * ext[pl.ds(HALO - left + k, tm), :]

    width = xc.shape[1]
    hd = width // LRU_HEADS
    lam = lam_ref[...]
    sp = jnp.maximum(-lam, 0.0) + jnp.log(1.0 + jnp.exp(-jnp.abs(lam)))
    for hh in range(LRU_HEADS):
        cs = slice(hh * hd, (hh + 1) * hd)
        xh = xc[:, cs]
        pre = _bdot(xh.astype(BF16), wax_ref[hh])
        rg = _sigmoid(pre[:, :hd] + ba_ref[:, cs])
        ig = _sigmoid(pre[:, hd:] + bx_ref[:, cs])
        a = jnp.exp(-LRU_C * rg * sp[:, cs])
        a_scr[:, cs] = a
        b_scr[:, cs] = jnp.sqrt(1.0 - a * a) * (ig * xh)

    first = (tile * tm) % seq_len == 0
    last = ((tile + 1) * tm) % seq_len == 0

    @pl.when(last if reverse else first)
    def _():
        carry[...] = jnp.broadcast_to(h0_ref[0], carry.shape)

    row = lax.broadcasted_iota(jnp.int32, (SUBLANES, width), 0)
    nblk = tm // SUBLANES

    def body(k, c):
        blk = (nblk - 1 - k) if reverse else k
        r0 = pl.multiple_of(blk * SUBLANES, SUBLANES)
        a = a_scr[pl.ds(r0, SUBLANES), :]
        b = b_scr[pl.ds(r0, SUBLANES), :]
        for s in (1, 2, 4):
            shift = (SUBLANES - s) if reverse else s
            a_sh = pltpu.roll(a, shift, 0)
            b_sh = pltpu.roll(b, shift, 0)
            m = (row < SUBLANES - s) if reverse else (row >= s)
            b = jnp.where(m, a * b_sh + b, b)
            a = jnp.where(m, a * a_sh, a)
        h = a * c + b
        o_ref[pl.ds(r0, SUBLANES), :] = h
        edge = h[0:1, :] if reverse else h[SUBLANES - 1:SUBLANES, :]
        return jnp.broadcast_to(edge, c.shape)

    carry[...] = lax.fori_loop(0, nblk, body, carry[...])


def lru_scan(z, col_block, width, conv_w, conv_b, w_a, b_a, w_x, b_x, lam, h0, seq_len, reverse, tm=512):
    r = z.shape[0]
    tm = _row_tile(seq_len, tm)
    n_tiles = r // tm
    nb = r // seq_len
    tile_of = (lambda i: n_tiles - 1 - i) if reverse else (lambda i: i)
    wax = jnp.concatenate([w_a, w_x], axis=-1).astype(BF16)
    kk = conv_w.shape[0]
    hd = width // LRU_HEADS
    const2 = lambda i: (0, 0)
    return pl.pallas_call(
        functools.partial(_lru_kernel, tm=tm, seq_len=seq_len, n_tiles=n_tiles, reverse=reverse),
        grid=(n_tiles,),
        in_specs=_halo_specs(tm, width, col_block, tile_of, r // HALO) + [
            pl.BlockSpec((kk, width), const2),
            pl.BlockSpec((1, width), const2),
            pl.BlockSpec((LRU_HEADS, hd, 2 * hd), lambda i: (0, 0, 0)),
            pl.BlockSpec((1, width), const2),
            pl.BlockSpec((1, width), const2),
            pl.BlockSpec((1, width), const2),
            pl.BlockSpec((1, 1, width), lambda i: ((tile_of(i) * tm) // seq_len, 0, 0)),
        ],
        out_specs=pl.BlockSpec((tm, width), lambda i: (tile_of(i), 0)),
        out_shape=jax.ShapeDtypeStruct((r, width), F32),
        scratch_shapes=[
            pltpu.VMEM((tm + 2 * HALO, width), F32),
            pltpu.VMEM((tm, width), F32),
            pltpu.VMEM((tm, width), F32),
            pltpu.VMEM((SUBLANES, width), F32),
        ],
        compiler_params=_cparams("arbitrary"),
        name="lru_scan_bwd" if reverse else "lru_scan_fwd",
    )(z, z, z, conv_w, conv_b.reshape(1, width), wax, b_a.reshape(1, width), b_x.reshape(1, width),
      lam.reshape(1, width), h0.reshape(nb, 1, width))


def _lru_out_prologue(gate, hf, hb):
    return (_gelu(gate) * (hf + hb)).astype(BF16)


def _pool_kernel(p_ref, pp_ref, pn_ref, wg_ref, bg_ref, sc_ref, o_ref, ext, *, tm, seq_len):
    i = pl.program_id(0)
    _fill_ext(ext, p_ref, pp_ref, pn_ref, i, tm, seq_len)
    width = o_ref.shape[1]
    gw = width // len(POOL_WINDOWS)
    t = ((i * tm) % seq_len + lax.broadcasted_iota(jnp.int32, (tm, 1), 0))
    for g, win in enumerate(POOL_WINDOWS):
        half = win // 2
        cs = slice(g * gw, (g + 1) * gw)
        s = ext[pl.ds(HALO - half, tm), cs]
        for k in range(1 - half, half):
            s = s + ext[pl.ds(HALO + k, tm), cs]
        cnt = (jnp.minimum(t + half, seq_len) - jnp.maximum(t - half, 0)).astype(F32)
        q = s / cnt - ext[pl.ds(HALO, tm), cs]
        y = _bdot(q.astype(BF16), wg_ref[g]) + bg_ref[:, cs]
        o_ref[:, cs] = (y * sc_ref[:, cs]).astype(BF16)


def pool_mix(p, w_g, b_g, scale, seq_len, tm=512):
    r, width = p.shape
    assert max(POOL_WINDOWS) // 2 <= HALO
    tm = _row_tile(seq_len, tm)
    ng, gw, _ = w_g.shape
    const2 = lambda i: (0, 0)
    return pl.pallas_call(
        functools.partial(_pool_kernel, tm=tm, seq_len=seq_len),
        grid=(r // tm,),
        in_specs=_halo_specs(tm, width, 0, lambda i: i, r // HALO) + [
            pl.BlockSpec((ng, gw, gw), lambda i: (0, 0, 0)),
            pl.BlockSpec((1, width), const2),
            pl.BlockSpec((1, width), const2),
        ],
        out_specs=pl.BlockSpec((tm, width), lambda i: (i, 0)),
        out_shape=jax.ShapeDtypeStruct((r, width), BF16),
        scratch_shapes=[pltpu.VMEM((tm + 2 * HALO, width), F32)],
        compiler_params=_cparams("parallel"),
        name="pool_mix",
    )(p, p, p, w_g.astype(BF16), b_g.reshape(1, width), scale.reshape(1, width))


def _hyconv_kernel(z_ref, zp_ref, zn_ref, cw_ref, cb_ref, o_ref, ext, *, tm, seq_len):
    i = pl.program_id(0)
    _fill_ext(ext, z_ref, zp_ref, zn_ref, i, tm, seq_len)
    kk = cw_ref.shape[0]
    left = kk // 2
    y = cb_ref[...] + cw_ref[0:1, :] * ext[pl.ds(HALO - left, tm), :]
    for k in range(1, kk):
        y = y + cw_ref[k:k + 1, :] * ext[pl.ds(HALO - left + k, tm), :]
    o_ref[0] = y


def hyena_short_conv(z, conv_w, conv_b, width, seq_len, tm=512):
    r = z.shape[0]
    nsplit = z.shape[1] // width
    tm = _row_tile(seq_len, tm)
    kk = conv_w.shape[0]
    n_row_blocks = r // HALO
    per = tm // HALO
    return pl.pallas_call(
        functools.partial(_hyconv_kernel, tm=tm, seq_len=seq_len),
        grid=(r // tm, nsplit),
        in_specs=[
            pl.BlockSpec((tm, width), lambda i, j: (i, j)),
            pl.BlockSpec((HALO, width), lambda i, j: (jnp.maximum(i * per - 1, 0), j)),
            pl.BlockSpec((HALO, width), lambda i, j: (jnp.minimum((i + 1) * per, n_row_blocks - 1), j)),
            pl.BlockSpec((kk, width), lambda i, j: (0, j)),
            pl.BlockSpec((1, width), lambda i, j: (0, j)),
        ],
        out_specs=pl.BlockSpec((1, tm, width), lambda i, j: (j, i, 0)),
        out_shape=jax.ShapeDtypeStruct((nsplit, r, width), F32),
        scratch_shapes=[pltpu.VMEM((tm + 2 * HALO, width), F32)],
        compiler_params=_cparams("parallel", "arbitrary"),
        name="hyena_short_conv",
    )(z, z, z, conv_w, conv_b.reshape(1, -1))


def _hyfilter_kernel(z_ref, w1_ref, b1_ref, w2_ref, b2_ref, w3_ref, b3_ref, fr_ref, wo_ref, dl_ref,
                     o_ref, *, n_out, seq_len):
    z = z_ref[...]
    tm = z.shape[0]
    t = pl.program_id(0) * tm + lax.broadcasted_iota(jnp.int32, (tm, 1), 0)
    keep = (t != seq_len).astype(F32)
    h = jnp.sin(fr_ref[0:1, :] * (_hdot(z, w1_ref[...]) + b1_ref[...]))
    h = jnp.sin(fr_ref[1:2, :] * (_hdot(h, w2_ref[...]) + b2_ref[...]))
    h = jnp.sin(fr_ref[2:3, :] * (_hdot(h, w3_ref[...]) + b3_ref[...]))
    window = jnp.exp(-z[:, 0:1] * dl_ref[...]) * keep
    width = dl_ref.shape[1]
    hb = h.astype(BF16)
    for q in range(n_out):
        cs = slice(q * width, (q + 1) * width)
        o_ref[:, cs] = _bdot(hb, wo_ref[:, cs]) * window


def hyena_filters(seq_len, width, f_w1, f_b1, f_w2, f_b2, f_w3, f_b3, f_freq, f_wout, tm=512):
    n = 2 * seq_len
    rows = np.arange(n)
    lag = np.where(rows < seq_len, rows, np.minimum(n - rows, seq_len - 1))
    t = jnp.linspace(0.0, 1.0, seq_len, dtype=F32)[lag][:, None]
    w = 2.0 * math.pi * jnp.asarray(lag, dtype=F32)[:, None] / seq_len
    bands = jnp.linspace(1e-4, HYENA_BANDS - 1, HYENA_BANDS, dtype=F32)[None, :]
    z = jnp.concatenate([t, jnp.cos(bands * w), jnp.sin(-bands * w)], axis=-1)
    emb = z.shape[1]
    hid = f_w1.shape[1]
    zp = jnp.zeros((n, LANES), F32).at[:, :emb].set(z)
    order = f_wout.shape[1] // (2 * width)
    wo = f_wout.reshape(hid, order, 2, width).transpose(2, 0, 1, 3).reshape(2, hid, order * width)
    w1p = jnp.zeros((LANES, hid), F32).at[:emb].set(f_w1)
    max_decay = math.log(HYENA_DECAY_TARGET) / HYENA_FAST_DECAY
    min_decay = math.log(HYENA_DECAY_TARGET) / HYENA_SLOW_DECAY
    deltas = jnp.abs(jnp.linspace(min_decay, max_decay, width, dtype=F32)).reshape(1, width)
    n_tot = order * width
    tm = _row_tile(seq_len, tm)
    c2 = lambda i: (0, 0)
    return pl.pallas_call(
        functools.partial(_hyfilter_kernel, n_out=order, seq_len=seq_len),
        grid=(n // tm,),
        in_specs=[
            pl.BlockSpec((tm, LANES), lambda i: (i, 0)),
            pl.BlockSpec((LANES, hid), c2), pl.BlockSpec((1, hid), c2),
            pl.BlockSpec((hid, hid), c2), pl.BlockSpec((1, hid), c2),
            pl.BlockSpec((hid, hid), c2), pl.BlockSpec((1, hid), c2),
            pl.BlockSpec((3, hid), c2),
            pl.BlockSpec((None, hid, n_tot), lambda i: ((i * tm) // seq_len, 0, 0)),
            pl.BlockSpec((1, width), c2),
        ],
        out_specs=pl.BlockSpec((tm, n_tot), lambda i: (i, 0)),
        out_shape=jax.ShapeDtypeStruct((n, n_tot), F32),
        compiler_params=_cparams("parallel"),
        name="hyena_filters",
    )(zp, w1p, f_b1.reshape(1, hid), f_w2, f_b2.reshape(1, hid), f_w3, f_b3.reshape(1, hid),
      f_freq, wo.astype(BF16), deltas)


def _dft_tables(seq_len):
    n = 2 * seq_len
    p_ = FFT_P
    n1 = n // p_
    t1n = n1 // 2
    f1 = np.arange(n1)[:, None]
    t1 = np.arange(n1)[None, :]
    ang1 = 2.0 * np.pi * ((f1 * t1) % n1) / n1
    c1, s1 = np.cos(ang1), np.sin(ang1)
    w1 = np.zeros((2 * n1, 2 * t1n))
    w1[0::2, :t1n] = c1[:, :t1n]
    w1[0::2, t1n:] = s1[:, :t1n]
    w1[1::2, :t1n] = -s1[:, :t1n]
    w1[1::2, t1n:] = c1[:, :t1n]
    w1k = np.zeros((2 * n1, n1))
    w1k[0::2] = c1
    w1k[1::2] = -s1
    w1i = np.zeros((2 * t1n, 2 * n1))
    ct, st = c1.T[:t1n] / n, s1.T[:t1n] / n
    w1i[:t1n, 0::2] = ct
    w1i[:t1n, 1::2] = -st
    w1i[t1n:, 0::2] = st
    w1i[t1n:, 1::2] = ct
    f2 = np.arange(p_)[:, None]
    pp = np.arange(p_)[None, :]
    ang2 = 2.0 * np.pi * ((f2 * pp) % p_) / p_
    c2, s2 = np.cos(ang2), np.sin(ang2)
    fb = np.block([[c2, s2], [-s2, c2]])
    fbi = np.block([[c2, -s2], [s2, c2]])
    angt = 2.0 * np.pi * ((np.arange(n1)[:, None] * np.arange(p_)[None, :]) % n) / n
    lane_bcast = lambda a: jnp.broadcast_to(jnp.asarray(a, dtype=F32)[:, :, None], (n1, p_, LANES))
    as_bf = lambda a: jnp.asarray(a, dtype=F32).astype(BF16)
    return dict(n1=n1, t1n=t1n, w1=as_bf(w1), w1k=as_bf(w1k), w1i=as_bf(w1i), fb=as_bf(fb), fbi=as_bf(fbi),
                twc=lane_bcast(np.cos(angt)), tws=lane_bcast(-np.sin(angt)))


DFT_PB = 16


def _rows_at(ref, pp):
    m, pb, _ = ref.shape
    return ref.reshape(m * pb, LANES)[pl.ds(pp, m, stride=pb), :]


def _set_rows_at(ref, pp, val):
    m, pb, _ = ref.shape
    ref.reshape(m * pb, LANES)[pl.ds(pp, m, stride=pb), :] = val


def _dft1_kernel(w_ref, x_ref, o_ref):
    for pp in range(x_ref.shape[1]):
        rhs = _rows_at(x_ref, pp).astype(BF16)
        _set_rows_at(o_ref, pp, _bdot(w_ref[...], rhs))


def dft_stage1(w, x4, idx):
    m, k = w.shape
    _, _, p_, c = x4.shape
    pb = min(DFT_PB, p_)
    return pl.pallas_call(
        _dft1_kernel,
        grid=(c // LANES, p_ // pb),
        in_specs=[pl.BlockSpec((m, k), lambda j, q: (0, 0)),
                  pl.BlockSpec((None, k, pb, LANES), lambda j, q: (idx, 0, q, j))],
        out_specs=pl.BlockSpec((m, pb, LANES), lambda j, q: (0, q, j)),
        out_shape=jax.ShapeDtypeStruct((m, p_, c), F32),
        compiler_params=_cparams("parallel", "parallel"),
        name="dft_stage1",
    )(w, x4)


def _dft3_kernel(w_ref, b_ref, g_ref, v_ref, bias_ref, o_ref):
    for pp in range(b_ref.shape[1]):
        y = _bdot(w_ref[...], _rows_at(b_ref, pp).astype(BF16))
        _set_rows_at(o_ref, pp, _rows_at(g_ref, pp) * (y + bias_ref[...] * _rows_at(v_ref, pp)))


def dft_inverse_stage1_gate(w, b3, gate4, gate_idx, v4, v_idx, bias):
    m, k = w.shape
    _, p_, c = b3.shape
    pb = min(DFT_PB, p_)
    lead = lambda idx: (lambda j, q: (idx, 0, q, j))
    return pl.pallas_call(
        _dft3_kernel,
        grid=(c // LANES, p_ // pb),
        in_specs=[pl.BlockSpec((m, k), lambda j, q: (0, 0)),
                  pl.BlockSpec((k, pb, LANES), lambda j, q: (0, q, j)),
                  pl.BlockSpec((None, m, pb, LANES), lead(gate_idx)),
                  pl.BlockSpec((None, m, pb, LANES), lead(v_idx)),
                  pl.BlockSpec((1, LANES), lambda j, q: (0, j))],
        out_specs=pl.BlockSpec((m, pb, LANES), lambda j, q: (0, q, j)),
        out_shape=jax.ShapeDtypeStruct((m, p_, c), F32),
        compiler_params=_cparams("parallel", "parallel"),
        name="dft_inverse_stage1_gate",
    )(w, b3, gate4, v4, bias.reshape(1, c).astype(F32))


def _twiddle(re, im, tc, ts, reps):
    tc = jnp.tile(tc, (1, reps))
    ts = jnp.tile(ts, (1, reps))
    return re * tc - im * ts, re * ts + im * tc


def _spec_fwd_kernel(a_ref, tc_ref, ts_ref, fb_ref, o_ref):
    reps = a_ref.shape[-1] // LANES
    tr, ti = _twiddle(a_ref[0, 0].astype(F32), a_ref[0, 1].astype(F32), tc_ref[0], ts_ref[0], reps)
    rhs = jnp.concatenate([tr, ti], axis=0).astype(BF16)
    o_ref[0] = _bdot(fb_ref[...], rhs).astype(o_ref.dtype)


def _spec_conv_kernel(a_ref, tc_ref, ts_ref, fb_ref, fbi_ref, k_ref, o_ref):
    p_ = FFT_P
    reps = a_ref.shape[-1] // LANES
    tc, ts = tc_ref[0], ts_ref[0]
    tr, ti = _twiddle(a_ref[0, 0].astype(F32), a_ref[0, 1].astype(F32), tc, ts, reps)
    x = _bdot(fb_ref[...], jnp.concatenate([tr, ti], axis=0).astype(BF16))
    xr, xi = x[:p_], x[p_:]
    kr, ki = k_ref[0, :p_].astype(F32), k_ref[0, p_:].astype(F32)
    yr = xr * kr - xi * ki
    yi = xr * ki + xi * kr
    bv = _bdot(fbi_ref[...], jnp.concatenate([yr, yi], axis=0).astype(BF16))
    orr, oi = _twiddle(bv[:p_], bv[p_:], tc, -ts, reps)
    o_ref[0, 0] = orr.astype(o_ref.dtype)
    o_ref[0, 1] = oi.astype(o_ref.dtype)


def spectrum_forward(a4, tab, ct=1024):
    n1, _, p_, c = a4.shape
    ct = min(ct, c)
    return pl.pallas_call(
        _spec_fwd_kernel,
        grid=(c // ct, n1),
        in_specs=[
            pl.BlockSpec((1, 2, p_, ct), lambda j, f: (f, 0, 0, j)),
            pl.BlockSpec((1, p_, LANES), lambda j, f: (f, 0, 0)),
            pl.BlockSpec((1, p_, LANES), lambda j, f: (f, 0, 0)),
            pl.BlockSpec((2 * p_, 2 * p_), lambda j, f: (0, 0)),
        ],
        out_specs=pl.BlockSpec((1, 2 * p_, ct), lambda j, f: (f, 0, j)),
        out_shape=jax.ShapeDtypeStruct((n1, 2 * p_, c), BF16),
        compiler_params=_cparams("parallel", "parallel"),
        name="dft_filter_stage2",
    )(a4, tab["twc"], tab["tws"], tab["fb"])


def spectrum_conv(a4, kspec, k_col_block, tab):
    n1, _, p_, c = a4.shape
    return pl.pallas_call(
        _spec_conv_kernel,
        grid=(n1,),
        in_specs=[
            pl.BlockSpec((1, 2, p_, c), lambda f: (f, 0, 0, 0)),
            pl.BlockSpec((1, p_, LANES), lambda f: (f, 0, 0)),
            pl.BlockSpec((1, p_, LANES), lambda f: (f, 0, 0)),
            pl.BlockSpec((2 * p_, 2 * p_), lambda f: (0, 0)),
            pl.BlockSpec((2 * p_, 2 * p_), lambda f: (0, 0)),
            pl.BlockSpec((1, 2 * p_, c), lambda f: (f, 0, k_col_block)),
        ],
        out_specs=pl.BlockSpec((1, 2, p_, c), lambda f: (f, 0, 0, 0)),
        out_shape=jax.ShapeDtypeStruct(a4.shape, F32),
        compiler_params=_cparams("parallel"),
        name="dft_stage2_conv",
    )(a4, tab["twc"], tab["tws"], tab["fb"], tab["fbi"], kspec)


def hyena_long_convs(xv, filt, bias, batch, seq_len):
    assert batch == 2, "the two sequences of the batch are packed as one complex sequence"
    c = xv.shape[-1]
    tab = _dft_tables(seq_len)
    n1, t1n, p_ = tab["n1"], tab["t1n"], FFT_P
    order = bias.shape[0]
    ak = dft_stage1(tab["w1k"], filt.reshape(1, n1, p_, order * c), 0)
    kspec = spectrum_forward(ak.reshape(n1, 2, p_, order * c), tab)
    xv4 = xv.reshape(3, batch * t1n, p_, c)
    y4, y_idx = xv4, 2
    for o in range(order):
        a = dft_stage1(tab["w1"], y4, y_idx)
        b4 = spectrum_conv(a.reshape(n1, 2, p_, c), kspec, o, tab)
        y = dft_inverse_stage1_gate(tab["w1i"], b4.reshape(2 * n1, p_, c), xv4, o, y4, y_idx, bias[o])
        y4, y_idx = y[None], 0
    return y.reshape(batch * seq_len, c)


def kernel(x, c, ctx, c_ctx, ada_w, ada_b, norm_g, gmlp_w_in, gmlp_g_v, gmlp_w_s, gmlp_b_s, gmlp_w_out, lru_w_in, lru_conv_w, lru_conv_b, lru_w_a, lru_b_a, lru_w_x, lru_b_x, lru_lam, lru_w_out, hyena_w_in, hyena_conv_w, hyena_conv_b, hyena_f_w1, hyena_f_b1, hyena_f_w2, hyena_f_b2, hyena_f_w3, hyena_f_b3, hyena_f_freq, hyena_f_wout, hyena_bias, hyena_w_out, pool_w_in, pool_w_g, pool_b_g, pool_scale, pool_w_out, ffn_w_gate, ffn_w_up, ffn_w_down, moe_w_router, moe_w_gate, moe_w_up, moe_w_down):
    B, L, D = x.shape
    Lc = ctx.shape[1]
    depth = ada_w.shape[0]
    n_mixers = 4
    assert B + 1 <= SUBLANES

    cc = jnp.zeros((SUBLANES, D), F32).at[:B].set(c).at[B].set(c_ctx)
    ada = ada_all(cc, ada_w, ada_b).reshape(depth, SUBLANES, 6, D)

    xs = add_pos(x)
    xc = ctx.reshape(B * Lc, D)
    last_ctx = max([i for i in range(depth) if i % n_mixers == 1], default=-1)

    for i in range(depth):
        kind, j = i % n_mixers, i // n_mixers
        update_ctx = i < last_ctx
        read_ctx = i <= last_ctx
        lat = [ada[i, :B, q].reshape(B, 1, D) for q in range(6)]
        cx = [jnp.broadcast_to(ada[i, B:B + 1, q].reshape(1, 1, D), (B, 1, D)) for q in range(6)]
        g = norm_g[i]
        streams = [(xs, lat, L)]
        if read_ctx:
            streams.append((xc, cx, Lc))

        if kind == 0:
            outs = []
            for (s, m, sl) in streams[:1 + int(update_ctx)]:
                hw = norm_mod_matmul(s, g[0], m[0], m[1], gmlp_w_in[j], sl, BF16, act="gelu")
                a = gmlp_spatial(hw, gmlp_g_v[j], gmlp_w_s[j], gmlp_b_s[j])
                outs.append(matmul_norm_res([(a, a.shape[1], 0)], _identity_bf16, gmlp_w_out[j],
                                            g[1], m[2], s, sl))
        elif kind == 1:
            W = lru_w_in.shape[2] // 2
            sc_args = lambda d: (lru_conv_w[j], lru_conv_b[j], lru_w_a[j, d], lru_b_a[j, d],
                                 lru_w_x[j, d], lru_b_x[j, d], lru_lam[j, d])
            zc = norm_mod_matmul(xc, g[0], cx[0], cx[1], lru_w_in[j], Lc, F32)
            zero = jnp.zeros((B, W), F32)
            hf_c = lru_scan(zc, 1, W, *sc_args(0), zero, Lc, False)
            hb_c = lru_scan(zc, 1, W, *sc_args(1), zero, Lc, True)
            zl = norm_mod_matmul(xs, g[0], lat[0], lat[1], lru_w_in[j], L, F32)
            hf = lru_scan(zl, 1, W, *sc_args(0), hf_c.reshape(B, Lc, W)[:, -1], L, False)
            hb = lru_scan(zl, 1, W, *sc_args(1), hb_c.reshape(B, Lc, W)[:, 0], L, True)
            outs = [matmul_norm_res([(zl, W, 0), (hf, W, 0), (hb, W, 0)], _lru_out_prologue,
                                    lru_w_out[j], g[1], lat[2], xs, L)]
            if update_ctx:
                outs.append(matmul_norm_res([(zc, W, 0), (hf_c, W, 0), (hb_c, W, 0)], _lru_out_prologue,
                                            lru_w_out[j], g[1], cx[2], xc, Lc))
        elif kind == 2:
            W = hyena_w_out.shape[1]
            filt = hyena_filters(L, W, hyena_f_w1[j], hyena_f_b1[j], hyena_f_w2[j], hyena_f_b2[j],
                                 hyena_f_w3[j], hyena_f_b3[j], hyena_f_freq[j], hyena_f_wout[j])
            outs = []
            for (s, m, sl) in streams[:1 + int(update_ctx)]:
                nb = s.shape[0] // sl
                z = norm_mod_matmul(s, g[0], m[0], m[1], hyena_w_in[j], sl, F32, tn=1024)
                xv = hyena_short_conv(z, hyena_conv_w[j], hyena_conv_b[j], W, sl)
                fl = filt if sl == L else hyena_filters(
                    sl, W, hyena_f_w1[j], hyena_f_b1[j], hyena_f_w2[j], hyena_f_b2[j],
                    hyena_f_w3[j], hyena_f_b3[j], hyena_f_freq[j], hyena_f_wout[j])
                y = hyena_long_convs(xv, fl, hyena_bias[j], nb, sl)
                outs.append(matmul_norm_res([(y, W, 0)], _identity_bf16, hyena_w_out[j], g[1], m[2], s, sl))
        else:
            outs = []
            for (s, m, sl) in streams[:1 + int(update_ctx)]:
                p = norm_mod_matmul(s, g[0], m[0], m[1], pool_w_in[j], sl, F32)
                a = pool_mix(p, pool_w_g[j], pool_b_g[j], pool_scale[j], sl)
                outs.append(matmul_norm_res([(a, a.shape[1], 0)], _identity_bf16, pool_w_out[j],
                                            g[1], m[2], s, sl))
        xs = outs[0]
        if update_ctx:
            xc = outs[1]

        k = i // 2
        todo = [(xs, lat, L)] + ([(xc, cx, Lc)] if update_ctx else [])
        res = []
        for (s, m, sl) in todo:
            if i % 2 == 0:
                res.append(ffn_dense(s, g[2], m[3], m[4], ffn_w_gate[k], ffn_w_up[k], ffn_w_down[k],
                                     g[3], m[5], sl))
            else:
                res.append(moe_block(s, g[2], m[3], m[4], moe_w_router[k], moe_w_gate[k], moe_w_up[k],
                                     moe_w_down[k], g[3], m[5], sl))
        xs = res[0]
        if update_ctx:
            xc = res[1]
    return xs.reshape(B, L, D)
```

```python
import functools
import math

import jax
import jax.numpy as jnp
import numpy as np
from jax import lax
from jax.experimental import pallas as pl
from jax.experimental.pallas import tpu as pltpu

F32 = jnp.float32
BF16 = jnp.bfloat16
EPS = 1e-6

VMEM_LIMIT_BYTES = 52 * 1024 * 1024
LANES = 128
SUBLANES = 8

GRID_W = 64
GMLP_CHUNK = 128
GMLP_GROUPS = 8
LRU_HEADS = 8
LRU_C = 8.0
POOL_WINDOWS = (2, 4, 8, 16)
HYENA_BANDS = 16
HYENA_FAST_DECAY = 0.3
HYENA_SLOW_DECAY = 1.5
HYENA_DECAY_TARGET = 1e-2
N_EXPERTS = 8
TOP_K = 2
FFT_P = 128
HALO = SUBLANES


def _cparams(*sem):
    return pltpu.CompilerParams(dimension_semantics=sem, vmem_limit_bytes=VMEM_LIMIT_BYTES)


def _rms(x, g):
    return x * lax.rsqrt(jnp.mean(x * x, axis=-1, keepdims=True) + EPS) * g


def _gelu(x):
    return 0.5 * x * (1.0 + jnp.tanh(math.sqrt(2.0 / math.pi) * (x + 0.044715 * (x * x * x))))


def _silu(x):
    return x * (1.0 / (1.0 + jnp.exp(-x)))


def _sigmoid(x):
    return 1.0 / (1.0 + jnp.exp(-x))


def _bdot(a, b):
    return jnp.dot(a, b, preferred_element_type=F32)


def _hdot(a, b):
    return jnp.dot(a, b, preferred_element_type=F32, precision=lax.Precision.HIGHEST)


def _to_row_tiles(val, ref):
    rows = val.shape[0]
    nq = val.shape[1] // LANES
    for q in range(nq):
        ref[pl.ds(q, rows, stride=nq), :] = val[:, q * LANES:(q + 1) * LANES].astype(ref.dtype)


def _from_row_tiles(ref, dst_ref):
    rows = dst_ref.shape[0]
    nq = dst_ref.shape[1] // LANES
    for q in range(nq):
        dst_ref[:, q * LANES:(q + 1) * LANES] = ref[pl.ds(q, rows, stride=nq), :].astype(dst_ref.dtype)


def _row_tile(rows, want):
    t = min(rows, want)
    assert rows % t == 0, (rows, t)
    return t


def _ada_kernel(c_ref, w_ref, b_ref, o_ref):
    o_ref[0] = _hdot(_silu(c_ref[...]), w_ref[0]) + b_ref[0]


def ada_all(cc, ada_w, ada_b):
    depth, d, d6 = ada_w.shape
    nchunk = d6 // d
    return pl.pallas_call(
        _ada_kernel,
        grid=(depth, nchunk),
        in_specs=[
            pl.BlockSpec((SUBLANES, d), lambda i, j: (0, 0)),
            pl.BlockSpec((1, d, d), lambda i, j: (i, 0, j)),
            pl.BlockSpec((1, 1, d), lambda i, j: (i, 0, j)),
        ],
        out_specs=pl.BlockSpec((1, SUBLANES, d), lambda i, j: (i, 0, j)),
        out_shape=jax.ShapeDtypeStruct((depth, SUBLANES, d6), F32),
        compiler_params=_cparams("parallel", "parallel"),
        name="ada",
    )(cc, ada_w, ada_b.reshape(depth, 1, d6))


def _pos_kernel(x_ref, rt_ref, ct_ref, o_ref):
    half = rt_ref.shape[-1]
    x = x_ref[...]
    o_ref[:, :, :half] = x[:, :, :half] + rt_ref[...]
    o_ref[:, :, half:] = x[:, :, half:] + ct_ref[...][None]


def add_pos(x):
    b, l, d = x.shape
    rows = l // GRID_W
    quarter = d // 4
    omega = 1.0 / (10000.0 ** (jnp.arange(quarter, dtype=F32) / quarter))

    def sincos(p):
        ang = p.reshape(-1, 1) * omega[None, :]
        return jnp.concatenate([jnp.sin(ang), jnp.cos(ang)], axis=-1)

    rtab = sincos(jnp.arange(rows, dtype=F32)).reshape(rows, 1, 2 * quarter)
    ctab = sincos(jnp.arange(GRID_W, dtype=F32))
    x3 = x.reshape(b * rows, GRID_W, d)
    tr = _row_tile(rows, 16)
    nrt = rows // tr
    out = pl.pallas_call(
        _pos_kernel,
        grid=(b * nrt,),
        in_specs=[
            pl.BlockSpec((tr, GRID_W, d), lambda i: (i, 0, 0)),
            pl.BlockSpec((tr, 1, 2 * quarter), lambda i: (i % nrt, 0, 0)),
            pl.BlockSpec((GRID_W, 2 * quarter), lambda i: (0, 0)),
        ],
        out_specs=pl.BlockSpec((tr, GRID_W, d), lambda i: (i, 0, 0)),
        out_shape=jax.ShapeDtypeStruct(x3.shape, F32),
        compiler_params=_cparams("parallel"),
        name="add_pos",
    )(x3, rtab, ctab)
    return out.reshape(b * l, d)


def _nmm_kernel(x_ref, g_ref, sh_ref, sc_ref, w_ref, o_ref, h_scr, *, act):
    @pl.when(pl.program_id(1) == 0)
    def _():
        h = _rms(x_ref[...], g_ref[...]) * (1.0 + sc_ref[0]) + sh_ref[0]
        h_scr[...] = h.astype(BF16)

    y = _bdot(h_scr[...], w_ref[...])
    if act == "gelu":
        y = _gelu(y)
    o_ref[...] = y.astype(o_ref.dtype)


def norm_mod_matmul(x, g, shift, scale, w, rows_per_mod, out_dtype, act=None, tm=512, tn=2048):
    r, d = x.shape
    n = w.shape[1]
    tm = _row_tile(min(r, rows_per_mod), tm)
    tn = min(tn, n)
    assert n % tn == 0
    mod_map = lambda i, j: ((i * tm) // rows_per_mod, 0, 0)
    return pl.pallas_call(
        functools.partial(_nmm_kernel, act=act),
        grid=(r // tm, n // tn),
        in_specs=[
            pl.BlockSpec((tm, d), lambda i, j: (i, 0)),
            pl.BlockSpec((1, d), lambda i, j: (0, 0)),
            pl.BlockSpec((1, 1, d), mod_map),
            pl.BlockSpec((1, 1, d), mod_map),
            pl.BlockSpec((d, tn), lambda i, j: (0, j)),
        ],
        out_specs=pl.BlockSpec((tm, tn), lambda i, j: (i, j)),
        out_shape=jax.ShapeDtypeStruct((r, n), out_dtype),
        scratch_shapes=[pltpu.VMEM((tm, d), BF16)],
        compiler_params=_cparams("parallel", "arbitrary"),
        name="norm_mod_matmul",
    )(x, g.reshape(1, d), shift, scale, w.astype(BF16))


def _mnr_kernel(*refs, n_in, prologue):
    in_refs = refs[:n_in]
    w_ref, g_ref, gate_ref, x_ref, o_ref = refs[n_in:]
    a = prologue(*[r[...] for r in in_refs])
    y = _bdot(a, w_ref[...])
    o_ref[...] = x_ref[...] + gate_ref[0] * _rms(y, g_ref[...])


def matmul_norm_res(inputs, prologue, w, g, gate, x, rows_per_mod, tm=512):
    r, d = x.shape
    k = w.shape[0]
    tm = _row_tile(min(r, rows_per_mod), tm)
    mod_map = lambda i: ((i * tm) // rows_per_mod, 0, 0)
    in_specs = [pl.BlockSpec((tm, wd), functools.partial(lambda i, cb: (i, cb), cb=cb))
                for (_, wd, cb) in inputs]
    in_specs += [
        pl.BlockSpec((k, d), lambda i: (0, 0)),
        pl.BlockSpec((1, d), lambda i: (0, 0)),
        pl.BlockSpec((1, 1, d), mod_map),
        pl.BlockSpec((tm, d), lambda i: (i, 0)),
    ]
    return pl.pallas_call(
        functools.partial(_mnr_kernel, n_in=len(inputs), prologue=prologue),
        grid=(r // tm,),
        in_specs=in_specs,
        out_specs=pl.BlockSpec((tm, d), lambda i: (i, 0)),
        out_shape=jax.ShapeDtypeStruct((r, d), F32),
        compiler_params=_cparams("parallel"),
        name="matmul_norm_res",
    )(*[a for (a, _, _) in inputs], w.astype(BF16), g.reshape(1, d), gate, x)


def _identity_bf16(a):
    return a.astype(BF16)


def _swiglu_step(h_scr, wg_ref, wu_ref, wd_ref, acc_scr, j):
    h = h_scr[...]
    t = (_silu(_bdot(h, wg_ref[...])) * _bdot(h, wu_ref[...])).astype(BF16)
    part = _bdot(t, wd_ref[...])

    @pl.when(j == 0)
    def _():
        acc_scr[...] = part

    @pl.when(j > 0)
    def _():
        acc_scr[...] += part


def _ffn_dense_kernel(x_ref, g1_ref, sh_ref, sc_ref, wg_ref, wu_ref, wd_ref, g2_ref, gate_ref,
                      o_ref, h_scr, acc_scr):
    j = pl.program_id(1)

    @pl.when(j == 0)
    def _():
        h = _rms(x_ref[...], g1_ref[...]) * (1.0 + sc_ref[0]) + sh_ref[0]
        h_scr[...] = h.astype(BF16)

    _swiglu_step(h_scr, wg_ref, wu_ref, wd_ref, acc_scr, j)

    @pl.when(j == pl.num_programs(1) - 1)
    def _():
        o_ref[...] = x_ref[...] + gate_ref[0] * _rms(acc_scr[...], g2_ref[...])


def ffn_dense(x, g1, shift, scale, wg, wu, wd, g2, gate, rows_per_mod, tm=512, tf=1408):
    r, d = x.shape
    ff = wg.shape[1]
    tm = _row_tile(min(r, rows_per_mod), tm)
    assert ff % tf == 0
    mod_map = lambda i, j: ((i * tm) // rows_per_mod, 0, 0)
    return pl.pallas_call(
        _ffn_dense_kernel,
        grid=(r // tm, ff // tf),
        in_specs=[
            pl.BlockSpec((tm, d), lambda i, j: (i, 0)),
            pl.BlockSpec((1, d), lambda i, j: (0, 0)),
            pl.BlockSpec((1, 1, d), mod_map),
            pl.BlockSpec((1, 1, d), mod_map),
            pl.BlockSpec((d, tf), lambda i, j: (0, j)),
            pl.BlockSpec((d, tf), lambda i, j: (0, j)),
            pl.BlockSpec((tf, d), lambda i, j: (j, 0)),
            pl.BlockSpec((1, d), lambda i, j: (0, 0)),
            pl.BlockSpec((1, 1, d), mod_map),
        ],
        out_specs=pl.BlockSpec((tm, d), lambda i, j: (i, 0)),
        out_shape=jax.ShapeDtypeStruct((r, d), F32),
        scratch_shapes=[pltpu.VMEM((tm, d), BF16), pltpu.VMEM((tm, d), F32)],
        compiler_params=_cparams("parallel", "arbitrary"),
        name="ffn_dense",
    )(x, g1.reshape(1, d), shift, scale, wg.astype(BF16), wu.astype(BF16), wd.astype(BF16),
      g2.reshape(1, d), gate)


def _ffn_routed_kernel(te_ref, nu_ref, idx0_ref, idxn_ref, h_hbm, wg_ref, wu_ref, wd_ref, o_ref,
                       xbuf, sems, h_scr, acc_scr, *, tm, nq, issue_steps):
    i = pl.program_id(0)
    j = pl.program_id(1)
    last = pl.num_programs(1) - 1
    used = i < nu_ref[0]
    slot = i % 2
    per = tm // issue_steps

    def row_copy(idx_ref, r, s):
        s0 = pl.multiple_of(idx_ref[r] * nq, nq)
        d0 = pl.multiple_of(r * nq, nq)
        return pltpu.make_async_copy(h_hbm.at[pl.ds(s0, nq)], xbuf.at[s, pl.ds(d0, nq)], sems.at[s])

    @pl.when(used)
    def _():
        @pl.when(jnp.logical_and(i == 0, j == 0))
        def _():
            def issue(r, carry):
                row_copy(idx0_ref, r, 0).start()
                return carry
            lax.fori_loop(0, tm, issue, 0)

        @pl.when(j == 0)
        def _():
            pltpu.make_async_copy(h_hbm.at[pl.ds(0, tm * nq)], xbuf.at[slot], sems.at[slot]).wait()
            _from_row_tiles(xbuf.at[slot], h_scr)

        @pl.when(jnp.logical_and(i + 1 < nu_ref[0], j < issue_steps))
        def _():
            for k in range(per):
                row_copy(idxn_ref, j * per + k, 1 - slot).start()

        _swiglu_step(h_scr, wg_ref.at[0], wu_ref.at[0], wd_ref.at[0], acc_scr, j)

        @pl.when(j == last)
        def _():
            _to_row_tiles(acc_scr[...], o_ref)

    @pl.when(jnp.logical_and(jnp.logical_not(used), j == last))
    def _():
        o_ref[...] = jnp.zeros_like(o_ref)


def ffn_routed(h_rt, src, tile_expert, n_used, wg, wu, wd, tm, tf=512):
    dq = wg.shape[1] // LANES
    d = dq * LANES
    p = src.shape[0]
    ff = wg.shape[2]
    assert ff % tf == 0 and p % tm == 0
    n_tiles = p // tm
    issue_steps = min(4, ff // tf)
    assert tm % issue_steps == 0
    grid_spec = pltpu.PrefetchScalarGridSpec(
        num_scalar_prefetch=2,
        grid=(n_tiles, ff // tf),
        in_specs=[
            pl.BlockSpec((tm,), lambda i, j, te, nu: (0,), memory_space=pltpu.SMEM),
            pl.BlockSpec((tm,), lambda i, j, te, nu: (jnp.minimum(i + 1, n_tiles - 1),),
                         memory_space=pltpu.SMEM),
            pl.BlockSpec(memory_space=pl.ANY),
            pl.BlockSpec((1, d, tf), lambda i, j, te, nu: (te[i], 0, j)),
            pl.BlockSpec((1, d, tf), lambda i, j, te, nu: (te[i], 0, j)),
            pl.BlockSpec((1, tf, d), lambda i, j, te, nu: (te[i], j, 0)),
        ],
        out_specs=pl.BlockSpec((tm * dq, LANES), lambda i, j, te, nu: (i, 0)),
        scratch_shapes=[pltpu.VMEM((2, tm * dq, LANES), F32), pltpu.SemaphoreType.DMA((2,)),
                        pltpu.VMEM((tm, d), BF16), pltpu.VMEM((tm, d), F32)],
    )
    return pl.pallas_call(
        functools.partial(_ffn_routed_kernel, tm=tm, nq=dq, issue_steps=issue_steps),
        grid_spec=grid_spec,
        out_shape=jax.ShapeDtypeStruct((p * dq, LANES), F32),
        compiler_params=_cparams("arbitrary", "arbitrary"),
        name="ffn_routed",
    )(tile_expert, n_used, src, src, h_rt, wg.astype(BF16), wu.astype(BF16), wd.astype(BF16))


ROUTE_IDX0 = N_EXPERTS
ROUTE_P0 = N_EXPERTS + TOP_K


def _router_kernel(x_ref, g_ref, sh_ref, sc_ref, wr_ref, h_ref, r_ref):
    h = _rms(x_ref[...], g_ref[...]) * (1.0 + sc_ref[0]) + sh_ref[0]
    _to_row_tiles(h, h_ref)
    logits = _hdot(h, wr_ref[...])
    lane = lax.broadcasted_iota(jnp.int32, logits.shape, 1)
    neg = jnp.float32(-jnp.inf)
    big = jnp.int32(LANES)
    lg = jnp.where(lane < N_EXPERTS, logits, neg)
    m1 = jnp.max(lg, axis=-1, keepdims=True)
    i1 = jnp.min(jnp.where(lg == m1, lane, big), axis=-1, keepdims=True)
    lg2 = jnp.where(lane == i1, neg, lg)
    m2 = jnp.max(lg2, axis=-1, keepdims=True)
    i2 = jnp.min(jnp.where(lg2 == m2, lane, big), axis=-1, keepdims=True)
    e2 = jnp.exp(m2 - m1)
    p1 = 1.0 / (1.0 + e2)
    p2 = e2 / (1.0 + e2)
    out = jnp.where(lane == ROUTE_IDX0, i1.astype(F32), 0.0)
    out = jnp.where(lane == ROUTE_IDX0 + 1, i2.astype(F32), out)
    out = jnp.where(lane == ROUTE_P0, p1, out)
    out = jnp.where(lane == ROUTE_P0 + 1, p2, out)
    r_ref[...] = out


def router(x, g, shift, scale, w_router, rows_per_mod, tm=512):
    r, d = x.shape
    tm = _row_tile(min(r, rows_per_mod), tm)
    wr = jnp.zeros((d, LANES), F32).at[:, :N_EXPERTS].set(w_router)
    mod_map = lambda i: ((i * tm) // rows_per_mod, 0, 0)
    return pl.pallas_call(
        _router_kernel,
        grid=(r // tm,),
        in_specs=[
            pl.BlockSpec((tm, d), lambda i: (i, 0)),
            pl.BlockSpec((1, d), lambda i: (0, 0)),
            pl.BlockSpec((1, 1, d), mod_map),
            pl.BlockSpec((1, 1, d), mod_map),
            pl.BlockSpec((d, LANES), lambda i: (0, 0)),
        ],
        out_specs=[pl.BlockSpec((tm * (d // LANES), LANES), lambda i: (i, 0)),
                   pl.BlockSpec((tm, LANES), lambda i: (i, 0))],
        out_shape=[jax.ShapeDtypeStruct((r * (d // LANES), LANES), F32), jax.ShapeDtypeStruct((r, LANES), F32)],
        compiler_params=_cparams("parallel"),
        name="router",
    )(x, g.reshape(1, d), shift, scale, wr)


def _gather_kernel(idx_ref, src_ref, o_ref, sem, *, rows, nq):
    def issue(r, carry):
        s0 = pl.multiple_of(idx_ref[r] * nq, nq)
        d0 = pl.multiple_of(r * nq, nq)
        pltpu.make_async_copy(src_ref.at[pl.ds(s0, nq)], o_ref.at[pl.ds(d0, nq)], sem).start()
        return carry

    lax.fori_loop(0, rows, issue, 0)
    pltpu.make_async_copy(src_ref.at[pl.ds(0, rows * nq)], o_ref, sem).wait()


def gather_rows(src, idx, nq, rows_per_step=512):
    n = idx.shape[0]
    rows = _row_tile(n, rows_per_step)
    return pl.pallas_call(
        functools.partial(_gather_kernel, rows=rows, nq=nq),
        grid=(n // rows,),
        in_specs=[
            pl.BlockSpec((rows,), lambda i: (i,), memory_space=pltpu.SMEM),
            pl.BlockSpec(memory_space=pl.ANY),
        ],
        out_specs=pl.BlockSpec((rows * nq, LANES), lambda i: (i, 0)),
        out_shape=jax.ShapeDtypeStruct((n * nq, LANES), src.dtype),
        scratch_shapes=[pltpu.SemaphoreType.DMA(())],
        compiler_params=_cparams("arbitrary"),
        name="gather_rows",
    )(idx, src)


def _combine_kernel(y0_ref, y1_ref, r_ref, g_ref, gate_ref, x_ref, o_ref, y_scr):
    rt = r_ref[...]
    p0 = rt[:, ROUTE_P0:ROUTE_P0 + 1]
    p1 = rt[:, ROUTE_P0 + 1:ROUTE_P0 + 2]
    _from_row_tiles(y0_ref, y_scr)
    y = p0 * y_scr[...]
    _from_row_tiles(y1_ref, y_scr)
    y = y + p1 * y_scr[...]
    o_ref[...] = x_ref[...] + gate_ref[0] * _rms(y, g_ref[...])


def combine(yg, route, g, gate, x, rows_per_mod, tm=512):
    r, d = x.shape
    tm = _row_tile(min(r, rows_per_mod), tm)
    nt = r // tm
    mod_map = lambda i: ((i * tm) // rows_per_mod, 0, 0)
    return pl.pallas_call(
        _combine_kernel,
        grid=(nt,),
        in_specs=[
            pl.BlockSpec((tm * (d // LANES), LANES), lambda i: (i, 0)),
            pl.BlockSpec((tm * (d // LANES), LANES), lambda i: (i + nt, 0)),
            pl.BlockSpec((tm, LANES), lambda i: (i, 0)),
            pl.BlockSpec((1, d), lambda i: (0, 0)),
            pl.BlockSpec((1, 1, d), mod_map),
            pl.BlockSpec((tm, d), lambda i: (i, 0)),
        ],
        out_specs=pl.BlockSpec((tm, d), lambda i: (i, 0)),
        out_shape=jax.ShapeDtypeStruct((r, d), F32),
        scratch_shapes=[pltpu.VMEM((tm, d), F32)],
        compiler_params=_cparams("parallel"),
        name="moe_combine",
    )(yg, yg, route, g.reshape(1, d), gate, x)


def moe_block(x, g1, shift, scale, w_router, wg, wu, wd, g2, gate, rows_per_mod, tm=512):
    n, d = x.shape
    h, route = router(x, g1, shift, scale, w_router, rows_per_mod)
    eidx = route[:, ROUTE_IDX0:ROUTE_IDX0 + TOP_K].astype(jnp.int32)
    flat_e = eidx.T.reshape(-1)
    onehot = (flat_e[:, None] == jnp.arange(N_EXPERTS, dtype=jnp.int32)[None, :]).astype(jnp.int32)
    csum = jnp.cumsum(onehot, axis=0)
    counts = csum[-1]
    rank = jnp.take_along_axis(csum, flat_e[:, None], axis=1)[:, 0] - 1
    tiles_per_e = (counts + tm - 1) // tm
    tile_end = jnp.cumsum(tiles_per_e)
    tile_start = tile_end - tiles_per_e
    slot = tile_start[flat_e] * tm + rank
    n_tiles = (TOP_K * n) // tm + N_EXPERTS
    tok = jnp.tile(jnp.arange(n, dtype=jnp.int32), TOP_K)
    src = jnp.zeros((n_tiles * tm,), jnp.int32).at[slot].set(tok)
    tile_ids = jnp.arange(n_tiles, dtype=jnp.int32)
    tile_expert = jnp.minimum(jnp.sum((tile_ids[:, None] >= tile_end[None, :]).astype(jnp.int32), axis=1),
                              N_EXPERTS - 1).astype(jnp.int32)
    n_used = tile_end[-1:].astype(jnp.int32)

    nq = d // LANES
    ys = ffn_routed(h, src, tile_expert, n_used, wg, wu, wd, tm)
    yg = gather_rows(ys, slot.astype(jnp.int32), nq)
    return combine(yg, route, g2, gate, x, rows_per_mod)


def _gmlp_kernel(u_ref, v_ref, gv_ref, ws_ref, bs_ref, o_ref, vn_scr):
    v = v_ref[...].astype(F32)
    mu = jnp.mean(v, axis=-1, keepdims=True)
    vc = v - mu
    vn = vc * lax.rsqrt(jnp.mean(vc * vc, axis=-1, keepdims=True) + EPS) * gv_ref[...]
    vn_scr[...] = vn.astype(BF16)
    tm, width = vn_scr.shape
    gw = width // GMLP_GROUPS
    for n in range(tm // GMLP_CHUNK):
        rs = slice(n * GMLP_CHUNK, (n + 1) * GMLP_CHUNK)
        for g in range(GMLP_GROUPS):
            cs = slice(g * gw, (g + 1) * gw)
            m = _bdot(ws_ref[g], vn_scr[rs, cs]) + bs_ref[g]
            o_ref[rs, cs] = (u_ref[rs, cs].astype(F32) * m).astype(BF16)


def gmlp_spatial(hw, g_v, w_s, b_s, tm=512):
    r, w2 = hw.shape
    width = w2 // 2
    gw = width // GMLP_GROUPS
    tm = _row_tile(r, tm)
    bsb = jnp.broadcast_to(b_s[:, :, None], (GMLP_GROUPS, GMLP_CHUNK, gw)).astype(F32)
    return pl.pallas_call(
        _gmlp_kernel,
        grid=(r // tm,),
        in_specs=[
            pl.BlockSpec((tm, width), lambda i: (i, 0)),
            pl.BlockSpec((tm, width), lambda i: (i, 1)),
            pl.BlockSpec((1, width), lambda i: (0, 0)),
            pl.BlockSpec((GMLP_GROUPS, GMLP_CHUNK, GMLP_CHUNK), lambda i: (0, 0, 0)),
            pl.BlockSpec((GMLP_GROUPS, GMLP_CHUNK, gw), lambda i: (0, 0, 0)),
        ],
        out_specs=pl.BlockSpec((tm, width), lambda i: (i, 0)),
        out_shape=jax.ShapeDtypeStruct((r, width), BF16),
        scratch_shapes=[pltpu.VMEM((tm, width), BF16)],
        compiler_params=_cparams("parallel"),
        name="gmlp_spatial",
    )(hw, hw, g_v.reshape(1, width), w_s.astype(BF16), bsb)


def _halo_specs(tm, width, col_block, tile_of, n_row_blocks):
    per = tm // HALO
    cur = pl.BlockSpec((tm, width), lambda i: (tile_of(i), col_block))
    prev = pl.BlockSpec((HALO, width), lambda i: (jnp.maximum(tile_of(i) * per - 1, 0), col_block))
    nxt = pl.BlockSpec((HALO, width),
                       lambda i: (jnp.minimum((tile_of(i) + 1) * per, n_row_blocks - 1), col_block))
    return [cur, prev, nxt]


def _fill_ext(ext, cur_ref, prev_ref, next_ref, tile, tm, seq_len):
    first = (tile * tm) % seq_len == 0
    last = ((tile + 1) * tm) % seq_len == 0
    ext[0:HALO, :] = jnp.where(first, 0.0, prev_ref[...])
    ext[HALO:HALO + tm, :] = cur_ref[...]
    ext[HALO + tm:HALO + tm + HALO, :] = jnp.where(last, 0.0, next_ref[...])


def _lru_kernel(xb_ref, xp_ref, xn_ref, cw_ref, cb_ref, wax_ref, ba_ref, bx_ref, lam_ref, h0_ref,
                o_ref, ext, a_scr, b_scr, carry, *, tm, seq_len, n_tiles, reverse):
    i = pl.program_id(0)
    tile = (n_tiles - 1 - i) if reverse else i
    _fill_ext(ext, xb_ref, xp_ref, xn_ref, tile, tm, seq_len)
    kk = cw_ref.shape[0]
    left = kk // 2
    xc = cb_ref[...] + cw_ref[0:1, :] * ext[pl.ds(HALO - left, tm), :]
    for k in range(1, kk):
        xc = xc + cw_ref[k:k + 1, :] * ext[pl.ds(HALO - left + k, tm), :]

    width = xc.shape[1]
    hd = width // LRU_HEADS
    lam = lam_ref[...]
    sp = jnp.maximum(-lam, 0.0) + jnp.log(1.0 + jnp.exp(-jnp.abs(lam)))
    for hh in range(LRU_HEADS):
        cs = slice(hh * hd, (hh + 1) * hd)
        xh = xc[:, cs]
        pre = _bdot(xh.astype(BF16), wax_ref[hh])
        rg = _sigmoid(pre[:, :hd] + ba_ref[:, cs])
        ig = _sigmoid(pre[:, hd:] + bx_ref[:, cs])
        a = jnp.exp(-LRU_C * rg * sp[:, cs])
        a_scr[:, cs] = a
        b_scr[:, cs] = jnp.sqrt(1.0 - a * a) * (ig * xh)

    first = (tile * tm) % seq_len == 0
    last = ((tile + 1) * tm) % seq_len == 0

    @pl.when(last if reverse else first)
    def _():
        carry[...] = jnp.broadcast_to(h0_ref[0], carry.shape)

    row = lax.broadcasted_iota(jnp.int32, (SUBLANES, width), 0)
    nblk = tm // SUBLANES

    def body(k, c):
        blk = (nblk - 1 - k) if reverse else k
        r0 = pl.multiple_of(blk * SUBLANES, SUBLANES)
        a = a_scr[pl.ds(r0, SUBLANES), :]
        b = b_scr[pl.ds(r0, SUBLANES), :]
        for s in (1, 2, 4):
            shift = (SUBLANES - s) if reverse else s
            a_sh = pltpu.roll(a, shift, 0)
            b_sh = pltpu.roll(b, shift, 0)
            m = (row < SUBLANES - s) if reverse else (row >= s)
            b = jnp.where(m, a * b_sh + b, b)
            a = jnp.where(m, a * a_sh, a)
        h = a * c + b
        o_ref[pl.ds(r0, SUBLANES), :] = h
        edge = h[0:1, :] if reverse else h[SUBLANES - 1:SUBLANES, :]
        return jnp.broadcast_to(edge, c.shape)

    carry[...] = lax.fori_loop(0, nblk, body, carry[...])


def lru_scan(z, col_block, width, conv_w, conv_b, w_a, b_a, w_x, b_x, lam, h0, seq_len, reverse, tm=512):
    r = z.shape[0]
    tm = _row_tile(seq_len, tm)
    n_tiles = r // tm
    nb = r // seq_len
    tile_of = (lambda i: n_tiles - 1 - i) if reverse else (lambda i: i)
    wax = jnp.concatenate([w_a, w_x], axis=-1).astype(BF16)
    kk = conv_w.shape[0]
    hd = width // LRU_HEADS
    const2 = lambda i: (0, 0)
    return pl.pallas_call(
        functools.partial(_lru_kernel, tm=tm, seq_len=seq_len, n_tiles=n_tiles, reverse=reverse),
        grid=(n_tiles,),
        in_specs=_halo_specs(tm, width, col_block, tile_of, r // HALO) + [
            pl.BlockSpec((kk, width), const2),
            pl.BlockSpec((1, width), const2),
            pl.BlockSpec((LRU_HEADS, hd, 2 * hd), lambda i: (0, 0, 0)),
            pl.BlockSpec((1, width), const2),
            pl.BlockSpec((1, width), const2),
            pl.BlockSpec((1, width), const2),
            pl.BlockSpec((1, 1, width), lambda i: ((tile_of(i) * tm) // seq_len, 0, 0)),
        ],
        out_specs=pl.BlockSpec((tm, width), lambda i: (tile_of(i), 0)),
        out_shape=jax.ShapeDtypeStruct((r, width), F32),
        scratch_shapes=[
            pltpu.VMEM((tm + 2 * HALO, width), F32),
            pltpu.VMEM((tm, width), F32),
            pltpu.VMEM((tm, width), F32),
            pltpu.VMEM((SUBLANES, width), F32),
        ],
        compiler_params=_cparams("arbitrary"),
        name="lru_scan_bwd" if reverse else "lru_scan_fwd",
    )(z, z, z, conv_w, conv_b.reshape(1, width), wax, b_a.reshape(1, width), b_x.reshape(1, width),
      lam.reshape(1, width), h0.reshape(nb, 1, width))


def _lru_out_prologue(gate, hf, hb):
    return (_gelu(gate) * (hf + hb)).astype(BF16)


def _pool_kernel(p_ref, pp_ref, pn_ref, wg_ref, bg_ref, sc_ref, o_ref, ext, *, tm, seq_len):
    i = pl.program_id(0)
    _fill_ext(ext, p_ref, pp_ref, pn_ref, i, tm, seq_len)
    width = o_ref.shape[1]
    gw = width // len(POOL_WINDOWS)
    t = ((i * tm) % seq_len + lax.broadcasted_iota(jnp.int32, (tm, 1), 0))
    for g, win in enumerate(POOL_WINDOWS):
        half = win // 2
        cs = slice(g * gw, (g + 1) * gw)
        s = ext[pl.ds(HALO - half, tm), cs]
        for k in range(1 - half, half):
            s = s + ext[pl.ds(HALO + k, tm), cs]
        cnt = (jnp.minimum(t + half, seq_len) - jnp.maximum(t - half, 0)).astype(F32)
        q = s / cnt - ext[pl.ds(HALO, tm), cs]
        y = _bdot(q.astype(BF16), wg_ref[g]) + bg_ref[:, cs]
        o_ref[:, cs] = (y * sc_ref[:, cs]).astype(BF16)


def pool_mix(p, w_g, b_g, scale, seq_len, tm=512):
    r, width = p.shape
    assert max(POOL_WINDOWS) // 2 <= HALO
    tm = _row_tile(seq_len, tm)
    ng, gw, _ = w_g.shape
    const2 = lambda i: (0, 0)
    return pl.pallas_call(
        functools.partial(_pool_kernel, tm=tm, seq_len=seq_len),
        grid=(r // tm,),
        in_specs=_halo_specs(tm, width, 0, lambda i: i, r // HALO) + [
            pl.BlockSpec((ng, gw, gw), lambda i: (0, 0, 0)),
            pl.BlockSpec((1, width), const2),
            pl.BlockSpec((1, width), const2),
        ],
        out_specs=pl.BlockSpec((tm, width), lambda i: (i, 0)),
        out_shape=jax.ShapeDtypeStruct((r, width), BF16),
        scratch_shapes=[pltpu.VMEM((tm + 2 * HALO, width), F32)],
        compiler_params=_cparams("parallel"),
        name="pool_mix",
    )(p, p, p, w_g.astype(BF16), b_g.reshape(1, width), scale.reshape(1, width))


def _hyconv_kernel(z_ref, zp_ref, zn_ref, cw_ref, cb_ref, o_ref, ext, *, tm, seq_len):
    i = pl.program_id(0)
    _fill_ext(ext, z_ref, zp_ref, zn_ref, i, tm, seq_len)
    kk = cw_ref.shape[0]
    left = kk // 2
    y = cb_ref[...] + cw_ref[0:1, :] * ext[pl.ds(HALO - left, tm), :]
    for k in range(1, kk):
        y = y + cw_ref[k:k + 1, :] * ext[pl.ds(HALO - left + k, tm), :]
    o_ref[0] = y


def hyena_short_conv(z, conv_w, conv_b, width, seq_len, tm=512):
    r = z.shape[0]
    nsplit = z.shape[1] // width
    tm = _row_tile(seq_len, tm)
    kk = conv_w.shape[0]
    n_row_blocks = r // HALO
    per = tm // HALO
    return pl.pallas_call(
        functools.partial(_hyconv_kernel, tm=tm, seq_len=seq_len),
        grid=(r // tm, nsplit),
        in_specs=[
            pl.BlockSpec((tm, width), lambda i, j: (i, j)),
            pl.BlockSpec((HALO, width), lambda i, j: (jnp.maximum(i * per - 1, 0), j)),
            pl.BlockSpec((HALO, width), lambda i, j: (jnp.minimum((i + 1) * per, n_row_blocks - 1), j)),
            pl.BlockSpec((kk, width), lambda i, j: (0, j)),
            pl.BlockSpec((1, width), lambda i, j: (0, j)),
        ],
        out_specs=pl.BlockSpec((1, tm, width), lambda i, j: (j, i, 0)),
        out_shape=jax.ShapeDtypeStruct((nsplit, r, width), F32),
        scratch_shapes=[pltpu.VMEM((tm + 2 * HALO, width), F32)],
        compiler_params=_cparams("parallel", "arbitrary"),
        name="hyena_short_conv",
    )(z, z, z, conv_w, conv_b.reshape(1, -1))


def _hyfilter_kernel(z_ref, w1_ref, b1_ref, w2_ref, b2_ref, w3_ref, b3_ref, fr_ref, wo_ref, dl_ref,
                     o_ref, *, n_out, seq_len):
    z = z_ref[...]
    tm = z.shape[0]
    t = pl.program_id(0) * tm + lax.broadcasted_iota(jnp.int32, (tm, 1), 0)
    keep = (t != seq_len).astype(F32)
    h = jnp.sin(fr_ref[0:1, :] * (_hdot(z, w1_ref[...]) + b1_ref[...]))
    h = jnp.sin(fr_ref[1:2, :] * (_hdot(h, w2_ref[...]) + b2_ref[...]))
    h = jnp.sin(fr_ref[2:3, :] * (_hdot(h, w3_ref[...]) + b3_ref[...]))
    window = jnp.exp(-z[:, 0:1] * dl_ref[...]) * keep
    width = dl_ref.shape[1]
    hb = h.astype(BF16)
    for q in range(n_out):
        cs = slice(q * width, (q + 1) * width)
        o_ref[:, cs] = _bdot(hb, wo_ref[:, cs]) * window


def hyena_filters(seq_len, width, f_w1, f_b1, f_w2, f_b2, f_w3, f_b3, f_freq, f_wout, tm=512):
    n = 2 * seq_len
    rows = np.arange(n)
    lag = np.where(rows < seq_len, rows, np.minimum(n - rows, seq_len - 1))
    t = jnp.linspace(0.0, 1.0, seq_len, dtype=F32)[lag][:, None]
    w = 2.0 * math.pi * jnp.asarray(lag, dtype=F32)[:, None] / seq_len
    bands = jnp.linspace(1e-4, HYENA_BANDS - 1, HYENA_BANDS, dtype=F32)[None, :]
    z = jnp.concatenate([t, jnp.cos(bands * w), jnp.sin(-bands * w)], axis=-1)
    emb = z.shape[1]
    hid = f_w1.shape[1]
    zp = jnp.zeros((n, LANES), F32).at[:, :emb].set(z)
    order = f_wout.shape[1] // (2 * width)
    wo = f_wout.reshape(hid, order, 2, width).transpose(2, 0, 1, 3).reshape(2, hid, order * width)
    w1p = jnp.zeros((LANES, hid), F32).at[:emb].set(f_w1)
    max_decay = math.log(HYENA_DECAY_TARGET) / HYENA_FAST_DECAY
    min_decay = math.log(HYENA_DECAY_TARGET) / HYENA_SLOW_DECAY
    deltas = jnp.abs(jnp.linspace(min_decay, max_decay, width, dtype=F32)).reshape(1, width)
    n_tot = order * width
    tm = _row_tile(seq_len, tm)
    c2 = lambda i: (0, 0)
    return pl.pallas_call(
        functools.partial(_hyfilter_kernel, n_out=order, seq_len=seq_len),
        grid=(n // tm,),
        in_specs=[
            pl.BlockSpec((tm, LANES), lambda i: (i, 0)),
            pl.BlockSpec((LANES, hid), c2), pl.BlockSpec((1, hid), c2),
            pl.BlockSpec((hid, hid), c2), pl.BlockSpec((1, hid), c2),
            pl.BlockSpec((hid, hid), c2), pl.BlockSpec((1, hid), c2),
            pl.BlockSpec((3, hid), c2),
            pl.BlockSpec((None, hid, n_tot), lambda i: ((i * tm) // seq_len, 0, 0)),
            pl.BlockSpec((1, width), c2),
        ],
        out_specs=pl.BlockSpec((tm, n_tot), lambda i: (i, 0)),
        out_shape=jax.ShapeDtypeStruct((n, n_tot), F32),
        compiler_params=_cparams("parallel"),
        name="hyena_filters",
    )(zp, w1p, f_b1.reshape(1, hid), f_w2, f_b2.reshape(1, hid), f_w3, f_b3.reshape(1, hid),
      f_freq, wo.astype(BF16), deltas)


def _dft_tables(seq_len):
    n = 2 * seq_len
    p_ = FFT_P
    n1 = n // p_
    t1n = n1 // 2
    f1 = np.arange(n1)[:, None]
    t1 = np.arange(n1)[None, :]
    ang1 = 2.0 * np.pi * ((f1 * t1) % n1) / n1
    c1, s1 = np.cos(ang1), np.sin(ang1)
    w1 = np.zeros((2 * n1, 2 * t1n))
    w1[0::2, :t1n] = c1[:, :t1n]
    w1[0::2, t1n:] = s1[:, :t1n]
    w1[1::2, :t1n] = -s1[:, :t1n]
    w1[1::2, t1n:] = c1[:, :t1n]
    w1k = np.zeros((2 * n1, n1))
    w1k[0::2] = c1
    w1k[1::2] = -s1
    w1i = np.zeros((2 * t1n, 2 * n1))
    ct, st = c1.T[:t1n] / n, s1.T[:t1n] / n
    w1i[:t1n, 0::2] = ct
    w1i[:t1n, 1::2] = -st
    w1i[t1n:, 0::2] = st
    w1i[t1n:, 1::2] = ct
    f2 = np.arange(p_)[:, None]
    pp = np.arange(p_)[None, :]
    ang2 = 2.0 * np.pi * ((f2 * pp) % p_) / p_
    c2, s2 = np.cos(ang2), np.sin(ang2)
    fb = np.block([[c2, s2], [-s2, c2]])
    fbi = np.block([[c2, -s2], [s2, c2]])
    angt = 2.0 * np.pi * ((np.arange(n1)[:, None] * np.arange(p_)[None, :]) % n) / n
    lane_bcast = lambda a: jnp.broadcast_to(jnp.asarray(a, dtype=F32)[:, :, None], (n1, p_, LANES))
    as_bf = lambda a: jnp.asarray(a, dtype=F32).astype(BF16)
    return dict(n1=n1, t1n=t1n, w1=as_bf(w1), w1k=as_bf(w1k), w1i=as_bf(w1i), fb=as_bf(fb), fbi=as_bf(fbi),
                twc=lane_bcast(np.cos(angt)), tws=lane_bcast(-np.sin(angt)))


DFT_PB = 16


def _rows_at(ref, pp):
    m, pb, _ = ref.shape
    return ref.reshape(m * pb, LANES)[pl.ds(pp, m, stride=pb), :]


def _set_rows_at(ref, pp, val):
    m, pb, _ = ref.shape
    ref.reshape(m * pb, LANES)[pl.ds(pp, m, stride=pb), :] = val


def _dft1_kernel(w_ref, x_ref, o_ref):
    for pp in range(x_ref.shape[1]):
        rhs = _rows_at(x_ref, pp).astype(BF16)
        _set_rows_at(o_ref, pp, _bdot(w_ref[...], rhs))


def dft_stage1(w, x4, idx):
    m, k = w.shape
    _, _, p_, c = x4.shape
    pb = min(DFT_PB, p_)
    return pl.pallas_call(
        _dft1_kernel,
        grid=(c // LANES, p_ // pb),
        in_specs=[pl.BlockSpec((m, k), lambda j, q: (0, 0)),
                  pl.BlockSpec((None, k, pb, LANES), lambda j, q: (idx, 0, q, j))],
        out_specs=pl.BlockSpec((m, pb, LANES), lambda j, q: (0, q, j)),
        out_shape=jax.ShapeDtypeStruct((m, p_, c), F32),
        compiler_params=_cparams("parallel", "parallel"),
        name="dft_stage1",
    )(w, x4)


def _dft3_kernel(w_ref, b_ref, g_ref, v_ref, bias_ref, o_ref):
    for pp in range(b_ref.shape[1]):
        y = _bdot(w_ref[...], _rows_at(b_ref, pp).astype(BF16))
        _set_rows_at(o_ref, pp, _rows_at(g_ref, pp) * (y + bias_ref[...] * _rows_at(v_ref, pp)))


def dft_inverse_stage1_gate(w, b3, gate4, gate_idx, v4, v_idx, bias):
    m, k = w.shape
    _, p_, c = b3.shape
    pb = min(DFT_PB, p_)
    lead = lambda idx: (lambda j, q: (idx, 0, q, j))
    return pl.pallas_call(
        _dft3_kernel,
        grid=(c // LANES, p_ // pb),
        in_specs=[pl.BlockSpec((m, k), lambda j, q: (0, 0)),
                  pl.BlockSpec((k, pb, LANES), lambda j, q: (0, q, j)),
                  pl.BlockSpec((None, m, pb, LANES), lead(gate_idx)),
                  pl.BlockSpec((None, m, pb, LANES), lead(v_idx)),
                  pl.BlockSpec((1, LANES), lambda j, q: (0, j))],
        out_specs=pl.BlockSpec((m, pb, LANES), lambda j, q: (0, q, j)),
        out_shape=jax.ShapeDtypeStruct((m, p_, c), F32),
        compiler_params=_cparams("parallel", "parallel"),
        name="dft_inverse_stage1_gate",
    )(w, b3, gate4, v4, bias.reshape(1, c).astype(F32))


def _twiddle(re, im, tc, ts, reps):
    tc = jnp.tile(tc, (1, reps))
    ts = jnp.tile(ts, (1, reps))
    return re * tc - im * ts, re * ts + im * tc


def _spec_fwd_kernel(a_ref, tc_ref, ts_ref, fb_ref, o_ref):
    reps = a_ref.shape[-1] // LANES
    tr, ti = _twiddle(a_ref[0, 0].astype(F32), a_ref[0, 1].astype(F32), tc_ref[0], ts_ref[0], reps)
    rhs = jnp.concatenate([tr, ti], axis=0).astype(BF16)
    o_ref[0] = _bdot(fb_ref[...], rhs).astype(o_ref.dtype)


def _spec_conv_kernel(a_ref, tc_ref, ts_ref, fb_ref, fbi_ref, k_ref, o_ref):
    p_ = FFT_P
    reps = a_ref.shape[-1] // LANES
    tc, ts = tc_ref[0], ts_ref[0]
    tr, ti = _twiddle(a_ref[0, 0].astype(F32), a_ref[0, 1].astype(F32), tc, ts, reps)
    x = _bdot(fb_ref[...], jnp.concatenate([tr, ti], axis=0).astype(BF16))
    xr, xi = x[:p_], x[p_:]
    kr, ki = k_ref[0, :p_].astype(F32), k_ref[0, p_:].astype(F32)
    yr = xr * kr - xi * ki
    yi = xr * ki + xi * kr
    bv = _bdot(fbi_ref[...], jnp.concatenate([yr, yi], axis=0).astype(BF16))
    orr, oi = _twiddle(bv[:p_], bv[p_:], tc, -ts, reps)
    o_ref[0, 0] = orr.astype(o_ref.dtype)
    o_ref[0, 1] = oi.astype(o_ref.dtype)


def spectrum_forward(a4, tab, ct=1024):
    n1, _, p_, c = a4.shape
    ct = min(ct, c)
    return pl.pallas_call(
        _spec_fwd_kernel,
        grid=(c // ct, n1),
        in_specs=[
            pl.BlockSpec((1, 2, p_, ct), lambda j, f: (f, 0, 0, j)),
            pl.BlockSpec((1, p_, LANES), lambda j, f: (f, 0, 0)),
            pl.BlockSpec((1, p_, LANES), lambda j, f: (f, 0, 0)),
            pl.BlockSpec((2 * p_, 2 * p_), lambda j, f: (0, 0)),
        ],
        out_specs=pl.BlockSpec((1, 2 * p_, ct), lambda j, f: (f, 0, j)),
        out_shape=jax.ShapeDtypeStruct((n1, 2 * p_, c), BF16),
        compiler_params=_cparams("parallel", "parallel"),
        name="dft_filter_stage2",
    )(a4, tab["twc"], tab["tws"], tab["fb"])


def spectrum_conv(a4, kspec, k_col_block, tab):
    n1, _, p_, c = a4.shape
    return pl.pallas_call(
        _spec_conv_kernel,
        grid=(n1,),
        in_specs=[
            pl.BlockSpec((1, 2, p_, c), lambda f: (f, 0, 0, 0)),
            pl.BlockSpec((1, p_, LANES), lambda f: (f, 0, 0)),
            pl.BlockSpec((1, p_, LANES), lambda f: (f, 0, 0)),
            pl.BlockSpec((2 * p_, 2 * p_), lambda f: (0, 0)),
            pl.BlockSpec((2 * p_, 2 * p_), lambda f: (0, 0)),
            pl.BlockSpec((1, 2 * p_, c), lambda f: (f, 0, k_col_block)),
        ],
        out_specs=pl.BlockSpec((1, 2, p_, c), lambda f: (f, 0, 0, 0)),
        out_shape=jax.ShapeDtypeStruct(a4.shape, F32),
        compiler_params=_cparams("parallel"),
        name="dft_stage2_conv",
    )(a4, tab["twc"], tab["tws"], tab["fb"], tab["fbi"], kspec)


def hyena_long_convs(xv, filt, bias, batch, seq_len):
    assert batch == 2, "the two sequences of the batch are packed as one complex sequence"
    c = xv.shape[-1]
    tab = _dft_tables(seq_len)
    n1, t1n, p_ = tab["n1"], tab["t1n"], FFT_P
    order = bias.shape[0]
    ak = dft_stage1(tab["w1k"], filt.reshape(1, n1, p_, order * c), 0)
    kspec = spectrum_forward(ak.reshape(n1, 2, p_, order * c), tab)
    xv4 = xv.reshape(3, batch * t1n, p_, c)
    y4, y_idx = xv4, 2
    for o in range(order):
        a = dft_stage1(tab["w1"], y4, y_idx)
        b4 = spectrum_conv(a.reshape(n1, 2, p_, c), kspec, o, tab)
        y = dft_inverse_stage1_gate(tab["w1i"], b4.reshape(2 * n1, p_, c), xv4, o, y4, y_idx, bias[o])
        y4, y_idx = y[None], 0
    return y.reshape(batch * seq_len, c)


def kernel(x, c, ctx, c_ctx, ada_w, ada_b, norm_g, gmlp_w_in, gmlp_g_v, gmlp_w_s, gmlp_b_s, gmlp_w_out, lru_w_in, lru_conv_w, lru_conv_b, lru_w_a, lru_b_a, lru_w_x, lru_b_x, lru_lam, lru_w_out, hyena_w_in, hyena_conv_w, hyena_conv_b, hyena_f_w1, hyena_f_b1, hyena_f_w2, hyena_f_b2, hyena_f_w3, hyena_f_b3, hyena_f_freq, hyena_f_wout, hyena_bias, hyena_w_out, pool_w_in, pool_w_g, pool_b_g, pool_scale, pool_w_out, ffn_w_gate, ffn_w_up, ffn_w_down, moe_w_router, moe_w_gate, moe_w_up, moe_w_down):
    B, L, D = x.shape
    Lc = ctx.shape[1]
    depth = ada_w.shape[0]
    n_mixers = 4
    assert B + 1 <= SUBLANES

    cc = jnp.zeros((SUBLANES, D), F32).at[:B].set(c).at[B].set(c_ctx)
    ada = ada_all(cc, ada_w, ada_b).reshape(depth, SUBLANES, 6, D)

    xs = add_pos(x)
    xc = ctx.reshape(B * Lc, D)
    last_ctx = max([i for i in range(depth) if i % n_mixers == 1], default=-1)

    for i in range(depth):
        kind, j = i % n_mixers, i // n_mixers
        update_ctx = i < last_ctx
        read_ctx = i <= last_ctx
        lat = [ada[i, :B, q].reshape(B, 1, D) for q in range(6)]
        cx = [jnp.broadcast_to(ada[i, B:B + 1, q].reshape(1, 1, D), (B, 1, D)) for q in range(6)]
        g = norm_g[i]
        streams = [(xs, lat, L)]
        if read_ctx:
            streams.append((xc, cx, Lc))

        if kind == 0:
            outs = []
            for (s, m, sl) in streams[:1 + int(update_ctx)]:
                hw = norm_mod_matmul(s, g[0], m[0], m[1], gmlp_w_in[j], sl, BF16, act="gelu")
                a = gmlp_spatial(hw, gmlp_g_v[j], gmlp_w_s[j], gmlp_b_s[j])
                outs.append(matmul_norm_res([(a, a.shape[1], 0)], _identity_bf16, gmlp_w_out[j],
                                            g[1], m[2], s, sl))
        elif kind == 1:
            W = lru_w_in.shape[2] // 2
            sc_args = lambda d: (lru_conv_w[j], lru_conv_b[j], lru_w_a[j, d], lru_b_a[j, d],
                                 lru_w_x[j, d], lru_b_x[j, d], lru_lam[j, d])
            zc = norm_mod_matmul(xc, g[0], cx[0], cx[1], lru_w_in[j], Lc, F32)
            zero = jnp.zeros((B, W), F32)
            hf_c = lru_scan(zc, 1, W, *sc_args(0), zero, Lc, False)
            hb_c = lru_scan(zc, 1, W, *sc_args(1), zero, Lc, True)
            zl = norm_mod_matmul(xs, g[0], lat[0], lat[1], lru_w_in[j], L, F32)
            hf = lru_scan(zl, 1, W, *sc_args(0), hf_c.reshape(B, Lc, W)[:, -1], L, False)
            hb = lru_scan(zl, 1, W, *sc_args(1), hb_c.reshape(B, Lc, W)[:, 0], L, True)
            outs = [matmul_norm_res([(zl, W, 0), (hf, W, 0), (hb, W, 0)], _lru_out_prologue,
                                    lru_w_out[j], g[1], lat[2], xs, L)]
            if update_ctx:
                outs.append(matmul_norm_res([(zc, W, 0), (hf_c, W, 0), (hb_c, W, 0)], _lru_out_prologue,
                                            lru_w_out[j], g[1], cx[2], xc, Lc))
        elif kind == 2:
            W = hyena_w_out.shape[1]
            filt = hyena_filters(L, W, hyena_f_w1[j], hyena_f_b1[j], hyena_f_w2[j], hyena_f_b2[j],
                                 hyena_f_w3[j], hyena_f_b3[j], hyena_f_freq[j], hyena_f_wout[j])
            outs = []
            for (s, m, sl) in streams[:1 + int(update_ctx)]:
                nb = s.shape[0] // sl
                z = norm_mod_matmul(s, g[0], m[0], m[1], hyena_w_in[j], sl, F32, tn=1024)
                xv = hyena_short_conv(z, hyena_conv_w[j], hyena_conv_b[j], W, sl)
                fl = filt if sl == L else hyena_filters(
                    sl, W, hyena_f_w1[j], hyena_f_b1[j], hyena_f_w2[j], hyena_f_b2[j],
                    hyena_f_w3[j], hyena_f_b3[j], hyena_f_freq[j], hyena_f_wout[j])
                y = hyena_long_convs(xv, fl, hyena_bias[j], nb, sl)
                outs.append(matmul_norm_res([(y, W, 0)], _identity_bf16, hyena_w_out[j], g[1], m[2], s, sl))
        else:
            outs = []
            for (s, m, sl) in streams[:1 + int(update_ctx)]:
                p = norm_mod_matmul(s, g[0], m[0], m[1], pool_w_in[j], sl, F32)
                a = pool_mix(p, pool_w_g[j], pool_b_g[j], pool_scale[j], sl)
                outs.append(matmul_norm_res([(a, a.shape[1], 0)], _identity_bf16, pool_w_out[j],
                                            g[1], m[2], s, sl))
        xs = outs[0]
        if update_ctx:
            xc = outs[1]

        k = i // 2
        todo = [(xs, lat, L)] + ([(xc, cx, Lc)] if update_ctx else [])
        res = []
        for (s, m, sl) in todo:
            if i % 2 == 0:
                res.append(ffn_dense(s, g[2], m[3], m[4], ffn_w_gate[k], ffn_w_up[k], ffn_w_down[k],
                                     g[3], m[5], sl))
            else:
                res.append(moe_block(s, g[2], m[3], m[4], moe_w_router[k], moe_w_gate[k], moe_w_up[k],
                                     moe_w_down[k], g[3], m[5], sl))
        xs = res[0]
        if update_ctx:
            xc = res[1]
    return xs.reshape(B, L, D)
```

```python
import functools
import math

import jax
import jax.numpy as jnp
import numpy as np
from jax import lax
from jax.experimental import pallas as pl
from jax.experimental.pallas import tpu as pltpu

F32 = jnp.float32
BF16 = jnp.bfloat16
EPS = 1e-6

VMEM_LIMIT_BYTES = 52 * 1024 * 1024
LANES = 128
SUBLANES = 8

GRID_W = 64
GMLP_CHUNK = 128
GMLP_GROUPS = 8
LRU_HEADS = 8
LRU_C = 8.0
POOL_WINDOWS = (2, 4, 8, 16)
HYENA_BANDS = 16
HYENA_FAST_DECAY = 0.3
HYENA_SLOW_DECAY = 1.5
HYENA_DECAY_TARGET = 1e-2
N_EXPERTS = 8
TOP_K = 2
FFT_P = 128
HALO = SUBLANES


def _cparams(*sem):
    return pltpu.CompilerParams(dimension_semantics=sem, vmem_limit_bytes=VMEM_LIMIT_BYTES)


def _rms(x, g):
    return x * lax.rsqrt(jnp.mean(x * x, axis=-1, keepdims=True) + EPS) * g


def _gelu(x):
    return 0.5 * x * (1.0 + jnp.tanh(math.sqrt(2.0 / math.pi) * (x + 0.044715 * (x * x * x))))


def _silu(x):
    return x * (1.0 / (1.0 + jnp.exp(-x)))


def _sigmoid(x):
    return 1.0 / (1.0 + jnp.exp(-x))


def _bdot(a, b):
    return jnp.dot(a, b, preferred_element_type=F32)


def _hdot(a, b):
    return jnp.dot(a, b, preferred_element_type=F32, precision=lax.Precision.HIGHEST)


def _to_row_tiles(val, ref):
    rows = val.shape[0]
    nq = val.shape[1] // LANES
    for q in range(nq):
        ref[pl.ds(q, rows, stride=nq), :] = val[:, q * LANES:(q + 1) * LANES].astype(ref.dtype)


def _from_row_tiles(ref, dst_ref):
    rows = dst_ref.shape[0]
    nq = dst_ref.shape[1] // LANES
    for q in range(nq):
        dst_ref[:, q * LANES:(q + 1) * LANES] = ref[pl.ds(q, rows, stride=nq), :].astype(dst_ref.dtype)


def _row_tile(rows, want):
    t = min(rows, want)
    assert rows % t == 0, (rows, t)
    return t


def _ada_kernel(c_ref, w_ref, b_ref, o_ref):
    o_ref[0] = _hdot(_silu(c_ref[...]), w_ref[0]) + b_ref[0]


def ada_all(cc, ada_w, ada_b):
    depth, d, d6 = ada_w.shape
    nchunk = d6 // d
    return pl.pallas_call(
        _ada_kernel,
        grid=(depth, nchunk),
        in_specs=[
            pl.BlockSpec((SUBLANES, d), lambda i, j: (0, 0)),
            pl.BlockSpec((1, d, d), lambda i, j: (i, 0, j)),
            pl.BlockSpec((1, 1, d), lambda i, j: (i, 0, j)),
        ],
        out_specs=pl.BlockSpec((1, SUBLANES, d), lambda i, j: (i, 0, j)),
        out_shape=jax.ShapeDtypeStruct((depth, SUBLANES, d6), F32),
        compiler_params=_cparams("parallel", "parallel"),
        name="ada",
    )(cc, ada_w, ada_b.reshape(depth, 1, d6))


def _pos_kernel(x_ref, rt_ref, ct_ref, o_ref):
    half = rt_ref.shape[-1]
    x = x_ref[...]
    o_ref[:, :, :half] = x[:, :, :half] + rt_ref[...]
    o_ref[:, :, half:] = x[:, :, half:] + ct_ref[...][None]


def add_pos(x):
    b, l, d = x.shape
    rows = l // GRID_W
    quarter = d // 4
    omega = 1.0 / (10000.0 ** (jnp.arange(quarter, dtype=F32) / quarter))

    def sincos(p):
        ang = p.reshape(-1, 1) * omega[None, :]
        return jnp.concatenate([jnp.sin(ang), jnp.cos(ang)], axis=-1)

    rtab = sincos(jnp.arange(rows, dtype=F32)).reshape(rows, 1, 2 * quarter)
    ctab = sincos(jnp.arange(GRID_W, dtype=F32))
    x3 = x.reshape(b * rows, GRID_W, d)
    tr = _row_tile(rows, 16)
    nrt = rows // tr
    out = pl.pallas_call(
        _pos_kernel,
        grid=(b * nrt,),
        in_specs=[
            pl.BlockSpec((tr, GRID_W, d), lambda i: (i, 0, 0)),
            pl.BlockSpec((tr, 1, 2 * quarter), lambda i: (i % nrt, 0, 0)),
            pl.BlockSpec((GRID_W, 2 * quarter), lambda i: (0, 0)),
        ],
        out_specs=pl.BlockSpec((tr, GRID_W, d), lambda i: (i, 0, 0)),
        out_shape=jax.ShapeDtypeStruct(x3.shape, F32),
        compiler_params=_cparams("parallel"),
        name="add_pos",
    )(x3, rtab, ctab)
    return out.reshape(b * l, d)


def _nmm_kernel(x_ref, g_ref, sh_ref, sc_ref, w_ref, o_ref, h_scr, *, act):
    @pl.when(pl.program_id(1) == 0)
    def _():
        h = _rms(x_ref[...], g_ref[...]) * (1.0 + sc_ref[0]) + sh_ref[0]
        h_scr[...] = h.astype(BF16)

    y = _bdot(h_scr[...], w_ref[...])
    if act == "gelu":
        y = _gelu(y)
    o_ref[...] = y.astype(o_ref.dtype)


def norm_mod_matmul(x, g, shift, scale, w, rows_per_mod, out_dtype, act=None, tm=512, tn=2048):
    r, d = x.shape
    n = w.shape[1]
    tm = _row_tile(min(r, rows_per_mod), tm)
    tn = min(tn, n)
    assert n % tn == 0
    mod_map = lambda i, j: ((i * tm) // rows_per_mod, 0, 0)
    return pl.pallas_call(
        functools.partial(_nmm_kernel, act=act),
        grid=(r // tm, n // tn),
        in_specs=[
            pl.BlockSpec((tm, d), lambda i, j: (i, 0)),
            pl.BlockSpec((1, d), lambda i, j: (0, 0)),
            pl.BlockSpec((1, 1, d), mod_map),
            pl.BlockSpec((1, 1, d), mod_map),
            pl.BlockSpec((d, tn), lambda i, j: (0, j)),
        ],
        out_specs=pl.BlockSpec((tm, tn), lambda i, j: (i, j)),
        out_shape=jax.ShapeDtypeStruct((r, n), out_dtype),
        scratch_shapes=[pltpu.VMEM((tm, d), BF16)],
        compiler_params=_cparams("parallel", "arbitrary"),
        name="norm_mod_matmul",
    )(x, g.reshape(1, d), shift, scale, w.astype(BF16))


HALO_BF16 = 2 * SUBLANES


def _nmm_conv_kernel(x_ref, xp_ref, xn_ref, g_ref, sh_ref, sc_ref, w_ref, cw_ref, cb_ref, o_ref,
                     h_scr, z_scr, *, tm, seq_len):
    i = pl.program_id(0)
    hl = HALO_BF16

    @pl.when(pl.program_id(1) == 0)
    def _():
        def nm(x):
            return (_rms(x, g_ref[...]) * (1.0 + sc_ref[0]) + sh_ref[0]).astype(BF16)
        h_scr[0:hl, :] = nm(xp_ref[...])
        h_scr[hl:hl + tm, :] = nm(x_ref[...])
        h_scr[hl + tm:hl + tm + hl, :] = nm(xn_ref[...])

    z = _bdot(h_scr[...], w_ref[...])
    first = (i * tm) % seq_len == 0
    last = ((i + 1) * tm) % seq_len == 0
    row = lax.broadcasted_iota(jnp.int32, (tm + 2 * hl, 1), 0)
    outside = jnp.logical_or(jnp.logical_and(first, row < hl), jnp.logical_and(last, row >= hl + tm))
    z_scr[...] = jnp.where(outside, 0.0, z)
    kk = cw_ref.shape[0]
    left = kk // 2
    y = cb_ref[...] + cw_ref[0:1, :] * z_scr[pl.ds(hl - left, tm), :]
    for k in range(1, kk):
        y = y + cw_ref[k:k + 1, :] * z_scr[pl.ds(hl - left + k, tm), :]
    o_ref[0] = y


def norm_mod_matmul_conv(x, g, shift, scale, w, conv_w, conv_b, width, seq_len, tm=512):
    r, d = x.shape
    n = w.shape[1]
    nsplit = n // width
    tm = _row_tile(seq_len, tm)
    hl = HALO_BF16
    per = tm // hl
    n_row_blocks = r // hl
    kk = conv_w.shape[0]
    mod_map = lambda i, j: ((i * tm) // seq_len, 0, 0)
    return pl.pallas_call(
        functools.partial(_nmm_conv_kernel, tm=tm, seq_len=seq_len),
        grid=(r // tm, nsplit),
        in_specs=[
            pl.BlockSpec((tm, d), lambda i, j: (i, 0)),
            pl.BlockSpec((hl, d), lambda i, j: (jnp.maximum(i * per - 1, 0), 0)),
            pl.BlockSpec((hl, d), lambda i, j: (jnp.minimum((i + 1) * per, n_row_blocks - 1), 0)),
            pl.BlockSpec((1, d), lambda i, j: (0, 0)),
            pl.BlockSpec((1, 1, d), mod_map),
            pl.BlockSpec((1, 1, d), mod_map),
            pl.BlockSpec((d, width), lambda i, j: (0, j)),
            pl.BlockSpec((kk, width), lambda i, j: (0, j)),
            pl.BlockSpec((1, width), lambda i, j: (0, j)),
        ],
        out_specs=pl.BlockSpec((1, tm, width), lambda i, j: (j, i, 0)),
        out_shape=jax.ShapeDtypeStruct((nsplit, r, width), F32),
        scratch_shapes=[pltpu.VMEM((tm + 2 * hl, d), BF16), pltpu.VMEM((tm + 2 * hl, width), F32)],
        compiler_params=_cparams("parallel", "arbitrary"),
        name="norm_mod_matmul_conv",
    )(x, x, x, g.reshape(1, d), shift, scale, w.astype(BF16), conv_w, conv_b.reshape(1, n))


def _mnr_kernel(*refs, n_in, prologue):
    in_refs = refs[:n_in]
    w_ref, g_ref, gate_ref, x_ref, o_ref = refs[n_in:]
    a = prologue(*[r[...] for r in in_refs])
    y = _bdot(a, w_ref[...])
    o_ref[...] = x_ref[...] + gate_ref[0] * _rms(y, g_ref[...])


def matmul_norm_res(inputs, prologue, w, g, gate, x, rows_per_mod, tm=512):
    r, d = x.shape
    k = w.shape[0]
    tm = _row_tile(min(r, rows_per_mod), tm)
    mod_map = lambda i: ((i * tm) // rows_per_mod, 0, 0)
    in_specs = [pl.BlockSpec((tm, wd), functools.partial(lambda i, cb: (i, cb), cb=cb))
                for (_, wd, cb) in inputs]
    in_specs += [
        pl.BlockSpec((k, d), lambda i: (0, 0)),
        pl.BlockSpec((1, d), lambda i: (0, 0)),
        pl.BlockSpec((1, 1, d), mod_map),
        pl.BlockSpec((tm, d), lambda i: (i, 0)),
    ]
    return pl.pallas_call(
        functools.partial(_mnr_kernel, n_in=len(inputs), prologue=prologue),
        grid=(r // tm,),
        in_specs=in_specs,
        out_specs=pl.BlockSpec((tm, d), lambda i: (i, 0)),
        out_shape=jax.ShapeDtypeStruct((r, d), F32),
        compiler_params=_cparams("parallel"),
        name="matmul_norm_res",
    )(*[a for (a, _, _) in inputs], w.astype(BF16), g.reshape(1, d), gate, x)


def _identity_bf16(a):
    return a.astype(BF16)


def _swiglu_step(h_scr, wg_ref, wu_ref, wd_ref, acc_scr, j):
    h = h_scr[...]
    t = (_silu(_bdot(h, wg_ref[...])) * _bdot(h, wu_ref[...])).astype(BF16)
    part = _bdot(t, wd_ref[...])

    @pl.when(j == 0)
    def _():
        acc_scr[...] = part

    @pl.when(j > 0)
    def _():
        acc_scr[...] += part


def _ffn_dense_kernel(x_ref, g1_ref, sh_ref, sc_ref, wg_ref, wu_ref, wd_ref, g2_ref, gate_ref,
                      o_ref, h_scr, acc_scr):
    j = pl.program_id(1)

    @pl.when(j == 0)
    def _():
        h = _rms(x_ref[...], g1_ref[...]) * (1.0 + sc_ref[0]) + sh_ref[0]
        h_scr[...] = h.astype(BF16)

    _swiglu_step(h_scr, wg_ref, wu_ref, wd_ref, acc_scr, j)

    @pl.when(j == pl.num_programs(1) - 1)
    def _():
        o_ref[...] = x_ref[...] + gate_ref[0] * _rms(acc_scr[...], g2_ref[...])


def ffn_dense(x, g1, shift, scale, wg, wu, wd, g2, gate, rows_per_mod, tm=512, tf=1408):
    r, d = x.shape
    ff = wg.shape[1]
    tm = _row_tile(min(r, rows_per_mod), tm)
    assert ff % tf == 0
    mod_map = lambda i, j: ((i * tm) // rows_per_mod, 0, 0)
    return pl.pallas_call(
        _ffn_dense_kernel,
        grid=(r // tm, ff // tf),
        in_specs=[
            pl.BlockSpec((tm, d), lambda i, j: (i, 0)),
            pl.BlockSpec((1, d), lambda i, j: (0, 0)),
            pl.BlockSpec((1, 1, d), mod_map),
            pl.BlockSpec((1, 1, d), mod_map),
            pl.BlockSpec((d, tf), lambda i, j: (0, j)),
            pl.BlockSpec((d, tf), lambda i, j: (0, j)),
            pl.BlockSpec((tf, d), lambda i, j: (j, 0)),
            pl.BlockSpec((1, d), lambda i, j: (0, 0)),
            pl.BlockSpec((1, 1, d), mod_map),
        ],
        out_specs=pl.BlockSpec((tm, d), lambda i, j: (i, 0)),
        out_shape=jax.ShapeDtypeStruct((r, d), F32),
        scratch_shapes=[pltpu.VMEM((tm, d), BF16), pltpu.VMEM((tm, d), F32)],
        compiler_params=_cparams("parallel", "arbitrary"),
        name="ffn_dense",
    )(x, g1.reshape(1, d), shift, scale, wg.astype(BF16), wu.astype(BF16), wd.astype(BF16),
      g2.reshape(1, d), gate)


def _ffn_routed_kernel(te_ref, nu_ref, idx0_ref, idxn_ref, h_hbm, wg_ref, wu_ref, wd_ref, o_ref,
                       xbuf, sems, h_scr, acc_scr, *, tm, nq, issue_steps):
    i = pl.program_id(0)
    j = pl.program_id(1)
    last = pl.num_programs(1) - 1
    used = i < nu_ref[0]
    slot = i % 2
    per = tm // issue_steps

    def row_copy(idx_ref, r, s):
        s0 = pl.multiple_of(idx_ref[r] * nq, nq)
        d0 = pl.multiple_of(r * nq, nq)
        return pltpu.make_async_copy(h_hbm.at[pl.ds(s0, nq)], xbuf.at[s, pl.ds(d0, nq)], sems.at[s])

    @pl.when(used)
    def _():
        @pl.when(jnp.logical_and(i == 0, j == 0))
        def _():
            def issue(r, carry):
                row_copy(idx0_ref, r, 0).start()
                return carry
            lax.fori_loop(0, tm, issue, 0)

        @pl.when(j == 0)
        def _():
            pltpu.make_async_copy(h_hbm.at[pl.ds(0, tm * nq)], xbuf.at[slot], sems.at[slot]).wait()
            _from_row_tiles(xbuf.at[slot], h_scr)

        @pl.when(jnp.logical_and(i + 1 < nu_ref[0], j < issue_steps))
        def _():
            for k in range(per):
                row_copy(idxn_ref, j * per + k, 1 - slot).start()

        _swiglu_step(h_scr, wg_ref.at[0], wu_ref.at[0], wd_ref.at[0], acc_scr, j)

        @pl.when(j == last)
        def _():
            _to_row_tiles(acc_scr[...], o_ref)

    @pl.when(jnp.logical_and(jnp.logical_not(used), j == last))
    def _():
        o_ref[...] = jnp.zeros_like(o_ref)


def ffn_routed(h_rt, src, tile_expert, n_used, wg, wu, wd, tm, tf=512):
    dq = wg.shape[1] // LANES
    d = dq * LANES
    p = src.shape[0]
    ff = wg.shape[2]
    assert ff % tf == 0 and p % tm == 0
    n_tiles = p // tm
    issue_steps = min(4, ff // tf)
    assert tm % issue_steps == 0
    grid_spec = pltpu.PrefetchScalarGridSpec(
        num_scalar_prefetch=2,
        grid=(n_tiles, ff // tf),
        in_specs=[
            pl.BlockSpec((tm,), lambda i, j, te, nu: (0,), memory_space=pltpu.SMEM),
            pl.BlockSpec((tm,), lambda i, j, te, nu: (jnp.minimum(i + 1, n_tiles - 1),),
                         memory_space=pltpu.SMEM),
            pl.BlockSpec(memory_space=pl.ANY),
            pl.BlockSpec((1, d, tf), lambda i, j, te, nu: (te[i], 0, j)),
            pl.BlockSpec((1, d, tf), lambda i, j, te, nu: (te[i], 0, j)),
            pl.BlockSpec((1, tf, d), lambda i, j, te, nu: (te[i], j, 0)),
        ],
        out_specs=pl.BlockSpec((tm * dq, LANES), lambda i, j, te, nu: (i, 0)),
        scratch_shapes=[pltpu.VMEM((2, tm * dq, LANES), F32), pltpu.SemaphoreType.DMA((2,)),
                        pltpu.VMEM((tm, d), BF16), pltpu.VMEM((tm, d), F32)],
    )
    return pl.pallas_call(
        functools.partial(_ffn_routed_kernel, tm=tm, nq=dq, issue_steps=issue_steps),
        grid_spec=grid_spec,
        out_shape=jax.ShapeDtypeStruct((p * dq, LANES), F32),
        compiler_params=_cparams("arbitrary", "arbitrary"),
        name="ffn_routed",
    )(tile_expert, n_used, src, src, h_rt, wg.astype(BF16), wu.astype(BF16), wd.astype(BF16))


ROUTE_IDX0 = N_EXPERTS
ROUTE_P0 = N_EXPERTS + TOP_K


def _router_kernel(x_ref, g_ref, sh_ref, sc_ref, wr_ref, h_ref, r_ref):
    h = _rms(x_ref[...], g_ref[...]) * (1.0 + sc_ref[0]) + sh_ref[0]
    _to_row_tiles(h, h_ref)
    logits = _hdot(h, wr_ref[...])
    lane = lax.broadcasted_iota(jnp.int32, logits.shape, 1)
    neg = jnp.float32(-jnp.inf)
    big = jnp.int32(LANES)
    lg = jnp.where(lane < N_EXPERTS, logits, neg)
    m1 = jnp.max(lg, axis=-1, keepdims=True)
    i1 = jnp.min(jnp.where(lg == m1, lane, big), axis=-1, keepdims=True)
    lg2 = jnp.where(lane == i1, neg, lg)
    m2 = jnp.max(lg2, axis=-1, keepdims=True)
    i2 = jnp.min(jnp.where(lg2 == m2, lane, big), axis=-1, keepdims=True)
    e2 = jnp.exp(m2 - m1)
    p1 = 1.0 / (1.0 + e2)
    p2 = e2 / (1.0 + e2)
    out = jnp.where(lane == ROUTE_IDX0, i1.astype(F32), 0.0)
    out = jnp.where(lane == ROUTE_IDX0 + 1, i2.astype(F32), out)
    out = jnp.where(lane == ROUTE_P0, p1, out)
    out = jnp.where(lane == ROUTE_P0 + 1, p2, out)
    r_ref[...] = out


def router(x, g, shift, scale, w_router, rows_per_mod, tm=512):
    r, d = x.shape
    tm = _row_tile(min(r, rows_per_mod), tm)
    wr = jnp.zeros((d, LANES), F32).at[:, :N_EXPERTS].set(w_router)
    mod_map = lambda i: ((i * tm) // rows_per_mod, 0, 0)
    return pl.pallas_call(
        _router_kernel,
        grid=(r // tm,),
        in_specs=[
            pl.BlockSpec((tm, d), lambda i: (i, 0)),
            pl.BlockSpec((1, d), lambda i: (0, 0)),
            pl.BlockSpec((1, 1, d), mod_map),
            pl.BlockSpec((1, 1, d), mod_map),
            pl.BlockSpec((d, LANES), lambda i: (0, 0)),
        ],
        out_specs=[pl.BlockSpec((tm * (d // LANES), LANES), lambda i: (i, 0)),
                   pl.BlockSpec((tm, LANES), lambda i: (i, 0))],
        out_shape=[jax.ShapeDtypeStruct((r * (d // LANES), LANES), F32), jax.ShapeDtypeStruct((r, LANES), F32)],
        compiler_params=_cparams("parallel"),
        name="router",
    )(x, g.reshape(1, d), shift, scale, wr)


GATHER_UNROLL = 8


def _gather_kernel(idx_ref, src_ref, o_ref, sem, *, rows, nq):
    def issue(rb, carry):
        for k in range(GATHER_UNROLL):
            r = GATHER_UNROLL * rb + k
            s0 = pl.multiple_of(idx_ref[r] * nq, nq)
            d0 = pl.multiple_of(r * nq, nq)
            pltpu.make_async_copy(src_ref.at[pl.ds(s0, nq)], o_ref.at[pl.ds(d0, nq)], sem).start(priority=k % 2)
        return carry

    lax.fori_loop(0, rows // GATHER_UNROLL, issue, 0)
    pltpu.make_async_copy(src_ref.at[pl.ds(0, rows * nq)], o_ref, sem).wait()


def gather_rows(src, idx, nq, rows_per_step=512):
    n = idx.shape[0]
    rows = _row_tile(n, rows_per_step)
    return pl.pallas_call(
        functools.partial(_gather_kernel, rows=rows, nq=nq),
        grid=(n // rows,),
        in_specs=[
            pl.BlockSpec((rows,), lambda i: (i,), memory_space=pltpu.SMEM),
            pl.BlockSpec(memory_space=pl.ANY),
        ],
        out_specs=pl.BlockSpec((rows * nq, LANES), lambda i: (i, 0)),
        out_shape=jax.ShapeDtypeStruct((n * nq, LANES), src.dtype),
        scratch_shapes=[pltpu.SemaphoreType.DMA(())],
        compiler_params=_cparams("arbitrary"),
        name="gather_rows",
    )(idx, src)


def _combine_kernel(y0_ref, y1_ref, r_ref, g_ref, gate_ref, x_ref, o_ref, y_scr):
    rt = r_ref[...]
    p0 = rt[:, ROUTE_P0:ROUTE_P0 + 1]
    p1 = rt[:, ROUTE_P0 + 1:ROUTE_P0 + 2]
    _from_row_tiles(y0_ref, y_scr)
    y = p0 * y_scr[...]
    _from_row_tiles(y1_ref, y_scr)
    y = y + p1 * y_scr[...]
    o_ref[...] = x_ref[...] + gate_ref[0] * _rms(y, g_ref[...])


def combine(yg, route, g, gate, x, rows_per_mod, tm=512):
    r, d = x.shape
    tm = _row_tile(min(r, rows_per_mod), tm)
    nt = r // tm
    mod_map = lambda i: ((i * tm) // rows_per_mod, 0, 0)
    return pl.pallas_call(
        _combine_kernel,
        grid=(nt,),
        in_specs=[
            pl.BlockSpec((tm * (d // LANES), LANES), lambda i: (i, 0)),
            pl.BlockSpec((tm * (d // LANES), LANES), lambda i: (i + nt, 0)),
            pl.BlockSpec((tm, LANES), lambda i: (i, 0)),
            pl.BlockSpec((1, d), lambda i: (0, 0)),
            pl.BlockSpec((1, 1, d), mod_map),
            pl.BlockSpec((tm, d), lambda i: (i, 0)),
        ],
        out_specs=pl.BlockSpec((tm, d), lambda i: (i, 0)),
        out_shape=jax.ShapeDtypeStruct((r, d), F32),
        scratch_shapes=[pltpu.VMEM((tm, d), F32)],
        compiler_params=_cparams("parallel"),
        name="moe_combine",
    )(yg, yg, route, g.reshape(1, d), gate, x)


def moe_block(x, g1, shift, scale, w_router, wg, wu, wd, g2, gate, rows_per_mod, tm=512):
    n, d = x.shape
    h, route = router(x, g1, shift, scale, w_router, rows_per_mod)
    eidx = route[:, ROUTE_IDX0:ROUTE_IDX0 + TOP_K].astype(jnp.int32)
    flat_e = eidx.T.reshape(-1)
    onehot = (flat_e[:, None] == jnp.arange(N_EXPERTS, dtype=jnp.int32)[None, :]).astype(jnp.int32)
    csum = jnp.cumsum(onehot, axis=0)
    counts = csum[-1]
    rank = jnp.take_along_axis(csum, flat_e[:, None], axis=1)[:, 0] - 1
    tiles_per_e = (counts + tm - 1) // tm
    tile_end = jnp.cumsum(tiles_per_e)
    tile_start = tile_end - tiles_per_e
    slot = tile_start[flat_e] * tm + rank
    n_tiles = (TOP_K * n) // tm + N_EXPERTS
    tok = jnp.tile(jnp.arange(n, dtype=jnp.int32), TOP_K)
    src = jnp.zeros((n_tiles * tm,), jnp.int32).at[slot].set(tok)
    tile_ids = jnp.arange(n_tiles, dtype=jnp.int32)
    tile_expert = jnp.minimum(jnp.sum((tile_ids[:, None] >= tile_end[None, :]).astype(jnp.int32), axis=1),
                              N_EXPERTS - 1).astype(jnp.int32)
    n_used = tile_end[-1:].astype(jnp.int32)

    nq = d // LANES
    ys = ffn_routed(h, src, tile_expert, n_used, wg, wu, wd, tm)
    yg = gather_rows(ys, slot.astype(jnp.int32), nq)
    return combine(yg, route, g2, gate, x, rows_per_mod)


def _gmlp_kernel(u_ref, v_ref, gv_ref, ws_ref, bs_ref, o_ref, vn_scr):
    v = v_ref[...].astype(F32)
    mu = jnp.mean(v, axis=-1, keepdims=True)
    vc = v - mu
    vn = vc * lax.rsqrt(jnp.mean(vc * vc, axis=-1, keepdims=True) + EPS) * gv_ref[...]
    vn_scr[...] = vn.astype(BF16)
    tm, width = vn_scr.shape
    gw = width // GMLP_GROUPS
    for n in range(tm // GMLP_CHUNK):
        rs = slice(n * GMLP_CHUNK, (n + 1) * GMLP_CHUNK)
        for g in range(GMLP_GROUPS):
            cs = slice(g * gw, (g + 1) * gw)
            m = _bdot(ws_ref[g], vn_scr[rs, cs]) + bs_ref[g]
            o_ref[rs, cs] = (u_ref[rs, cs].astype(F32) * m).astype(BF16)


def gmlp_spatial(hw, g_v, w_s, b_s, tm=512):
    r, w2 = hw.shape
    width = w2 // 2
    gw = width // GMLP_GROUPS
    tm = _row_tile(r, tm)
    bsb = jnp.broadcast_to(b_s[:, :, None], (GMLP_GROUPS, GMLP_CHUNK, gw)).astype(F32)
    return pl.pallas_call(
        _gmlp_kernel,
        grid=(r // tm,),
        in_specs=[
            pl.BlockSpec((tm, width), lambda i: (i, 0)),
            pl.BlockSpec((tm, width), lambda i: (i, 1)),
            pl.BlockSpec((1, width), lambda i: (0, 0)),
            pl.BlockSpec((GMLP_GROUPS, GMLP_CHUNK, GMLP_CHUNK), lambda i: (0, 0, 0)),
            pl.BlockSpec((GMLP_GROUPS, GMLP_CHUNK, gw), lambda i: (0, 0, 0)),
        ],
        out_specs=pl.BlockSpec((tm, width), lambda i: (i, 0)),
        out_shape=jax.ShapeDtypeStruct((r, width), BF16),
        scratch_shapes=[pltpu.VMEM((tm, width), BF16)],
        compiler_params=_cparams("parallel"),
        name="gmlp_spatial",
    )(hw, hw, g_v.reshape(1, width), w_s.astype(BF16), bsb)


def _halo_specs(tm, width, col_block, tile_of, n_row_blocks):
    per = tm // HALO
    cur = pl.BlockSpec((tm, width), lambda i: (tile_of(i), col_block))
    prev = pl.BlockSpec((HALO, width), lambda i: (jnp.maximum(tile_of(i) * per - 1, 0), col_block))
    nxt = pl.BlockSpec((HALO, width),
                       lambda i: (jnp.minimum((tile_of(i) + 1) * per, n_row_blocks - 1), col_block))
    return [cur, prev, nxt]


def _fill_ext(ext, cur_ref, prev_ref, next_ref, tile, tm, seq_len):
    first = (tile * tm) % seq_len == 0
    last = ((tile + 1) * tm) % seq_len == 0
    ext[0:HALO, :] = jnp.where(first, 0.0, prev_ref[...])
    ext[HALO:HALO + tm, :] = cur_ref[...]
    ext[HALO + tm:HALO + tm + HALO, :] = jnp.where(last, 0.0, next_ref[...])


def _lru_kernel(xb_ref, xp_ref, xn_ref, cw_ref, cb_ref, wax_ref, ba_ref, bx_ref, lam_ref, h0_ref,
                o_ref, ext, a_scr, b_scr, carry, *, tm, seq_len, n_tiles, reverse):
    i = pl.program_id(0)
    tile = (n_tiles - 1 - i) if reverse else i
    _fill_ext(ext, xb_ref, xp_ref, xn_ref, tile, tm, seq_len)
    kk = cw_ref.shape[0]
    left = kk // 2
    xc = cb_ref[...] + cw_ref[0:1, :] * ext[pl.ds(HALO - left, tm), :]
    for k in range(1, kk):
        xc = xc + cw_ref[k:k + 1, :] * ext[pl.ds(HALO - left + k, tm), :]

    width = xc.shape[1]
    hd = width // LRU_HEADS
    lam = lam_ref[...]
    sp = jnp.maximum(-lam, 0.0) + jnp.log(1.0 + jnp.exp(-jnp.abs(lam)))
    for hh in range(LRU_HEADS):
        cs = slice(hh * hd, (hh + 1) * hd)
        xh = xc[:, cs]
        pre = _bdot(xh.astype(BF16), wax_ref[hh])
        rg = _sigmoid(pre[:, :hd] + ba_ref[:, cs])
        ig = _sigmoid(pre[:, hd:] + bx_ref[:, cs])
        a = jnp.exp(-LRU_C * rg * sp[:, cs])
        a_scr[:, cs] = a
        b_scr[:, cs] = jnp.sqrt(1.0 - a * a) * (ig * xh)

    first = (tile * tm) % seq_len == 0
    last = ((tile + 1) * tm) % seq_len == 0

    @pl.when(last if reverse else first)
    def _():
        carry[...] = jnp.broadcast_to(h0_ref[0], carry.shape)

    row = lax.broadcasted_iota(jnp.int32, (SUBLANES, width), 0)
    nblk = tm // SUBLANES

    def body(k, c):
        blk = (nblk - 1 - k) if reverse else k
        r0 = pl.multiple_of(blk * SUBLANES, SUBLANES)
        a = a_scr[pl.ds(r0, SUBLANES), :]
        b = b_scr[pl.ds(r0, SUBLANES), :]
        for s in (1, 2, 4):
            shift = (SUBLANES - s) if reverse else s
            a_sh = pltpu.roll(a, shift, 0)
            b_sh = pltpu.roll(b, shift, 0)
            m = (row < SUBLANES - s) if reverse else (row >= s)
            b = jnp.where(m, a * b_sh + b, b)
            a = jnp.where(m, a * a_sh, a)
        h = a * c + b
        o_ref[pl.ds(r0, SUBLANES), :] = h
        edge = h[0:1, :] if reverse else h[SUBLANES - 1:SUBLANES, :]
        return jnp.broadcast_to(edge, c.shape)

    carry[...] = lax.fori_loop(0, nblk, body, carry[...])


def lru_scan(z, col_block, width, conv_w, conv_b, w_a, b_a, w_x, b_x, lam, h0, seq_len, reverse, tm=512):
    r = z.shape[0]
    tm = _row_tile(seq_len, tm)
    n_tiles = r // tm
    nb = r // seq_len
    tile_of = (lambda i: n_tiles - 1 - i) if reverse else (lambda i: i)
    wax = jnp.concatenate([w_a, w_x], axis=-1).astype(BF16)
    kk = conv_w.shape[0]
    hd = width // LRU_HEADS
    const2 = lambda i: (0, 0)
    return pl.pallas_call(
        functools.partial(_lru_kernel, tm=tm, seq_len=seq_len, n_tiles=n_tiles, reverse=reverse),
        grid=(n_tiles,),
        in_specs=_halo_specs(tm, width, col_block, tile_of, r // HALO) + [
            pl.BlockSpec((kk, width), const2),
            pl.BlockSpec((1, width), const2),
            pl.BlockSpec((LRU_HEADS, hd, 2 * hd), lambda i: (0, 0, 0)),
            pl.BlockSpec((1, width), const2),
            pl.BlockSpec((1, width), const2),
            pl.BlockSpec((1, width), const2),
            pl.BlockSpec((1, 1, width), lambda i: ((tile_of(i) * tm) // seq_len, 0, 0)),
        ],
        out_specs=pl.BlockSpec((tm, width), lambda i: (tile_of(i), 0)),
        out_shape=jax.ShapeDtypeStruct((r, width), F32),
        scratch_shapes=[
            pltpu.VMEM((tm + 2 * HALO, width), F32),
            pltpu.VMEM((tm, width), F32),
            pltpu.VMEM((tm, width), F32),
            pltpu.VMEM((SUBLANES, width), F32),
        ],
        compiler_params=_cparams("arbitrary"),
        name="lru_scan_bwd" if reverse else "lru_scan_fwd",
    )(z, z, z, conv_w, conv_b.reshape(1, width), wax, b_a.reshape(1, width), b_x.reshape(1, width),
      lam.reshape(1, width), h0.reshape(nb, 1, width))


def _lru_out_prologue(gate, hf, hb):
    return (_gelu(gate) * (hf + hb)).astype(BF16)


def _pool_kernel(p_ref, pp_ref, pn_ref, wg_ref, bg_ref, sc_ref, o_ref, ext, *, tm, seq_len):
    i = pl.program_id(0)
    _fill_ext(ext, p_ref, pp_ref, pn_ref, i, tm, seq_len)
    width = o_ref.shape[1]
    gw = width // len(POOL_WINDOWS)
    t = ((i * tm) % seq_len + lax.broadcasted_iota(jnp.int32, (tm, 1), 0))
    for g, win in enumerate(POOL_WINDOWS):
        half = win // 2
        cs = slice(g * gw, (g + 1) * gw)
        s = ext[pl.ds(HALO - half, tm), cs]
        for k in range(1 - half, half):
            s = s + ext[pl.ds(HALO + k, tm), cs]
        cnt = (jnp.minimum(t + half, seq_len) - jnp.maximum(t - half, 0)).astype(F32)
        q = s / cnt - ext[pl.ds(HALO, tm), cs]
        y = _bdot(q.astype(BF16), wg_ref[g]) + bg_ref[:, cs]
        o_ref[:, cs] = (y * sc_ref[:, cs]).astype(BF16)


def pool_mix(p, w_g, b_g, scale, seq_len, tm=512):
    r, width = p.shape
    assert max(POOL_WINDOWS) // 2 <= HALO
    tm = _row_tile(seq_len, tm)
    ng, gw, _ = w_g.shape
    const2 = lambda i: (0, 0)
    return pl.pallas_call(
        functools.partial(_pool_kernel, tm=tm, seq_len=seq_len),
        grid=(r // tm,),
        in_specs=_halo_specs(tm, width, 0, lambda i: i, r // HALO) + [
            pl.BlockSpec((ng, gw, gw), lambda i: (0, 0, 0)),
            pl.BlockSpec((1, width), const2),
            pl.BlockSpec((1, width), const2),
        ],
        out_specs=pl.BlockSpec((tm, width), lambda i: (i, 0)),
        out_shape=jax.ShapeDtypeStruct((r, width), BF16),
        scratch_shapes=[pltpu.VMEM((tm + 2 * HALO, width), F32)],
        compiler_params=_cparams("parallel"),
        name="pool_mix",
    )(p, p, p, w_g.astype(BF16), b_g.reshape(1, width), scale.reshape(1, width))


def _hyfilter_kernel(z_ref, w1_ref, b1_ref, w2_ref, b2_ref, w3_ref, b3_ref, fr_ref, wo_ref, dl_ref,
                     o_ref, *, n_out, seq_len):
    z = z_ref[...]
    tm = z.shape[0]
    t = pl.program_id(0) * tm + lax.broadcasted_iota(jnp.int32, (tm, 1), 0)
    keep = (t != seq_len).astype(F32)
    h = jnp.sin(fr_ref[0:1, :] * (_hdot(z, w1_ref[...]) + b1_ref[...]))
    h = jnp.sin(fr_ref[1:2, :] * (_hdot(h, w2_ref[...]) + b2_ref[...]))
    h = jnp.sin(fr_ref[2:3, :] * (_hdot(h, w3_ref[...]) + b3_ref[...]))
    window = jnp.exp(-z[:, 0:1] * dl_ref[...]) * keep
    width = dl_ref.shape[1]
    hb = h.astype(BF16)
    for q in range(n_out):
        cs = slice(q * width, (q + 1) * width)
        o_ref[:, cs] = _bdot(hb, wo_ref[:, cs]) * window


def hyena_filters(seq_len, width, f_w1, f_b1, f_w2, f_b2, f_w3, f_b3, f_freq, f_wout, tm=512):
    n = 2 * seq_len
    rows = np.arange(n)
    lag = np.where(rows < seq_len, rows, np.minimum(n - rows, seq_len - 1))
    t = (jnp.asarray(lag, dtype=F32) / (seq_len - 1))[:, None]
    w = 2.0 * math.pi * jnp.asarray(lag, dtype=F32)[:, None] / seq_len
    bands = jnp.linspace(1e-4, HYENA_BANDS - 1, HYENA_BANDS, dtype=F32)[None, :]
    z = jnp.concatenate([t, jnp.cos(bands * w), jnp.sin(-bands * w)], axis=-1)
    emb = z.shape[1]
    hid = f_w1.shape[1]
    zp = jnp.zeros((n, LANES), F32).at[:, :emb].set(z)
    order = f_wout.shape[1] // (2 * width)
    wo = f_wout.reshape(hid, order, 2, width).transpose(2, 0, 1, 3).reshape(2, hid, order * width)
    w1p = jnp.zeros((LANES, hid), F32).at[:emb].set(f_w1)
    max_decay = math.log(HYENA_DECAY_TARGET) / HYENA_FAST_DECAY
    min_decay = math.log(HYENA_DECAY_TARGET) / HYENA_SLOW_DECAY
    deltas = jnp.abs(jnp.linspace(min_decay, max_decay, width, dtype=F32)).reshape(1, width)
    n_tot = order * width
    tm = _row_tile(seq_len, tm)
    c2 = lambda i: (0, 0)
    return pl.pallas_call(
        functools.partial(_hyfilter_kernel, n_out=order, seq_len=seq_len),
        grid=(n // tm,),
        in_specs=[
            pl.BlockSpec((tm, LANES), lambda i: (i, 0)),
            pl.BlockSpec((LANES, hid), c2), pl.BlockSpec((1, hid), c2),
            pl.BlockSpec((hid, hid), c2), pl.BlockSpec((1, hid), c2),
            pl.BlockSpec((hid, hid), c2), pl.BlockSpec((1, hid), c2),
            pl.BlockSpec((3, hid), c2),
            pl.BlockSpec((None, hid, n_tot), lambda i: ((i * tm) // seq_len, 0, 0)),
            pl.BlockSpec((1, width), c2),
        ],
        out_specs=pl.BlockSpec((tm, n_tot), lambda i: (i, 0)),
        out_shape=jax.ShapeDtypeStruct((n, n_tot), F32),
        compiler_params=_cparams("parallel"),
        name="hyena_filters",
    )(zp, w1p, f_b1.reshape(1, hid), f_w2, f_b2.reshape(1, hid), f_w3, f_b3.reshape(1, hid),
      f_freq, wo.astype(BF16), deltas)


def _dft_tables(seq_len):
    n = 2 * seq_len
    p_ = FFT_P
    n1 = n // p_
    t1n = n1 // 2
    f1 = np.arange(n1)[:, None]
    t1 = np.arange(n1)[None, :]
    ang1 = 2.0 * np.pi * ((f1 * t1) % n1) / n1
    c1, s1 = np.cos(ang1), np.sin(ang1)
    w1 = np.zeros((2 * n1, 2 * t1n))
    w1[:n1, :t1n] = c1[:, :t1n]
    w1[:n1, t1n:] = s1[:, :t1n]
    w1[n1:, :t1n] = -s1[:, :t1n]
    w1[n1:, t1n:] = c1[:, :t1n]
    nh = n1 // 2 + 1
    nhp = -(-nh // SUBLANES) * SUBLANES
    w1k = np.zeros((2 * nhp, n1))
    w1k[:nh] = c1[:nh]
    w1k[nhp:nhp + nh] = -s1[:nh]
    w1i = np.zeros((2 * t1n, 2 * n1))
    ct, st = c1.T[:t1n] / n, s1.T[:t1n] / n
    w1i[:t1n, :n1] = ct
    w1i[:t1n, n1:] = -st
    w1i[t1n:, :n1] = st
    w1i[t1n:, n1:] = ct
    f2 = np.arange(p_)[:, None]
    pp = np.arange(p_)[None, :]
    ang2 = 2.0 * np.pi * ((f2 * pp) % p_) / p_
    c2, s2 = np.cos(ang2), np.sin(ang2)
    fb = np.block([[c2, s2], [-s2, c2]])
    fbi = np.block([[c2, -s2], [s2, c2]])
    fb = np.stack([fb, np.concatenate([fb[:p_][::-1], fb[p_:][::-1]], axis=0)])
    fbi = np.stack([fbi, np.concatenate([fbi[:, :p_][:, ::-1], fbi[:, p_:][:, ::-1]], axis=1)])
    angt = 2.0 * np.pi * ((np.arange(n1)[:, None] * np.arange(p_)[None, :]) % n) / n
    lane_bcast = lambda a: jnp.broadcast_to(jnp.asarray(a, dtype=F32)[:, :, None], (n1, p_, LANES))
    as_bf = lambda a: jnp.asarray(a, dtype=F32).astype(BF16)
    return dict(n1=n1, t1n=t1n, nh=nh, w1=as_bf(w1), w1k=as_bf(w1k), w1i=as_bf(w1i), fb=as_bf(fb), fbi=as_bf(fbi),
                twc=lane_bcast(np.cos(angt)), tws=lane_bcast(-np.sin(angt)))


DFT_PB = 16


def _rows_at(ref, pp):
    m, pb, _ = ref.shape
    return ref.reshape(m * pb, LANES)[pl.ds(pp, m, stride=pb), :]


def _set_rows_at(ref, pp, val):
    m, pb, _ = ref.shape
    ref.reshape(m * pb, LANES)[pl.ds(pp, m, stride=pb), :] = val


def _pack_complex(re, im):
    r = lax.bitcast_convert_type(re.astype(BF16).astype(F32), jnp.uint32)
    i = lax.bitcast_convert_type(im.astype(BF16).astype(F32), jnp.uint32)
    return r | (i >> 16)


def _unpack_complex(w):
    re = lax.bitcast_convert_type(w & jnp.uint32(0xFFFF0000), F32)
    im = lax.bitcast_convert_type(w << 16, F32)
    return re, im


def _dft1_kernel(w_ref, x_ref, o_ref):
    half = w_ref.shape[0] // 2
    for pp in range(x_ref.shape[1]):
        y = _bdot(w_ref[...], _rows_at(x_ref, pp).astype(BF16))
        _set_rows_at(o_ref, pp, _pack_complex(y[:half], y[half:]))


def dft_stage1(w, x4, idx):
    m, k = w.shape
    m = m // 2
    _, _, p_, c = x4.shape
    pb = min(DFT_PB, p_)
    return pl.pallas_call(
        _dft1_kernel,
        grid=(c // LANES, p_ // pb),
        in_specs=[pl.BlockSpec((2 * m, k), lambda j, q: (0, 0)),
                  pl.BlockSpec((None, k, pb, LANES), lambda j, q: (idx, 0, q, j))],
        out_specs=pl.BlockSpec((m, pb, LANES), lambda j, q: (0, q, j)),
        out_shape=jax.ShapeDtypeStruct((m, p_, c), jnp.uint32),
        compiler_params=_cparams("parallel", "parallel"),
        name="dft_stage1",
    )(w, x4)


def _dft3_kernel(w_ref, b_ref, g_ref, v_ref, bias_ref, o_ref):
    for pp in range(b_ref.shape[1]):
        re, im = _unpack_complex(_rows_at(b_ref, pp))
        y = _bdot(w_ref[...], jnp.concatenate([re, im], axis=0).astype(BF16))
        _set_rows_at(o_ref, pp, _rows_at(g_ref, pp) * (y + bias_ref[...] * _rows_at(v_ref, pp)))


def dft_inverse_stage1_gate(w, b3, gate4, gate_idx, v4, v_idx, bias):
    m, k = w.shape
    k = k // 2
    _, p_, c = b3.shape
    pb = min(DFT_PB, p_)
    lead = lambda idx: (lambda j, q: (idx, 0, q, j))
    return pl.pallas_call(
        _dft3_kernel,
        grid=(c // LANES, p_ // pb),
        in_specs=[pl.BlockSpec((m, 2 * k), lambda j, q: (0, 0)),
                  pl.BlockSpec((k, pb, LANES), lambda j, q: (0, q, j)),
                  pl.BlockSpec((None, m, pb, LANES), lead(gate_idx)),
                  pl.BlockSpec((None, m, pb, LANES), lead(v_idx)),
                  pl.BlockSpec((1, LANES), lambda j, q: (0, j))],
        out_specs=pl.BlockSpec((m, pb, LANES), lambda j, q: (0, q, j)),
        out_shape=jax.ShapeDtypeStruct((m, p_, c), F32),
        compiler_params=_cparams("parallel", "parallel"),
        name="dft_inverse_stage1_gate",
    )(w, b3, gate4, v4, bias.reshape(1, c).astype(F32))


def _twiddle(re, im, tc, ts, reps):
    tc = jnp.tile(tc, (1, reps))
    ts = jnp.tile(ts, (1, reps))
    return re * tc - im * ts, re * ts + im * tc


def _spec_fwd_kernel(a_ref, tc_ref, ts_ref, fb_ref, o_ref):
    reps = a_ref.shape[-1] // LANES
    re, im = _unpack_complex(a_ref[0])
    tr, ti = _twiddle(re, im, tc_ref[0], ts_ref[0], reps)
    rhs = jnp.concatenate([tr, ti], axis=0).astype(BF16)
    o_ref[0] = _bdot(fb_ref[...], rhs).astype(o_ref.dtype)


def _spec_conv_kernel(a_ref, tc_ref, ts_ref, fb_ref, fbi_ref, k_ref, o_ref, *, n1):
    p_ = FFT_P
    reps = a_ref.shape[-1] // LANES
    conj = jnp.where(pl.program_id(0) > n1 // 2, -1.0, 1.0).astype(F32)
    tc, ts = tc_ref[0], ts_ref[0]
    re, im = _unpack_complex(a_ref[0])
    tr, ti = _twiddle(re, im, tc, ts, reps)
    x = _bdot(fb_ref[...], jnp.concatenate([tr, ti], axis=0).astype(BF16))
    xr, xi = x[:p_], x[p_:]
    kr, ki = k_ref[0, :p_].astype(F32), conj * k_ref[0, p_:].astype(F32)
    yr = xr * kr - xi * ki
    yi = xr * ki + xi * kr
    bv = _bdot(fbi_ref[...], jnp.concatenate([yr, yi], axis=0).astype(BF16))
    orr, oi = _twiddle(bv[:p_], bv[p_:], tc, -ts, reps)
    o_ref[0] = _pack_complex(orr, oi)


def spectrum_forward(a3, tab, ct=1024):
    _, p_, c = a3.shape
    nh = tab["nh"]
    ct = min(ct, c)
    return pl.pallas_call(
        _spec_fwd_kernel,
        grid=(c // ct, nh),
        in_specs=[
            pl.BlockSpec((1, p_, ct), lambda j, f: (f, 0, j)),
            pl.BlockSpec((1, p_, LANES), lambda j, f: (f, 0, 0)),
            pl.BlockSpec((1, p_, LANES), lambda j, f: (f, 0, 0)),
            pl.BlockSpec((None, 2 * p_, 2 * p_), lambda j, f: (0, 0, 0)),
        ],
        out_specs=pl.BlockSpec((1, 2 * p_, ct), lambda j, f: (f, 0, j)),
        out_shape=jax.ShapeDtypeStruct((nh, 2 * p_, c), BF16),
        compiler_params=_cparams("parallel", "parallel"),
        name="dft_filter_stage2",
    )(a3, tab["twc"], tab["tws"], tab["fb"])


def spectrum_conv(a3, kspec, k_col_block, tab):
    n1, p_, c = a3.shape
    half = n1 // 2
    mirrored = lambda f: (f > half).astype(jnp.int32)
    return pl.pallas_call(
        functools.partial(_spec_conv_kernel, n1=n1),
        grid=(n1,),
        in_specs=[
            pl.BlockSpec((1, p_, c), lambda f: (f, 0, 0)),
            pl.BlockSpec((1, p_, LANES), lambda f: (f, 0, 0)),
            pl.BlockSpec((1, p_, LANES), lambda f: (f, 0, 0)),
            pl.BlockSpec((None, 2 * p_, 2 * p_), lambda f: (mirrored(f), 0, 0)),
            pl.BlockSpec((None, 2 * p_, 2 * p_), lambda f: (mirrored(f), 0, 0)),
            pl.BlockSpec((1, 2 * p_, c), lambda f: (jnp.where(f > half, n1 - f, f), 0, k_col_block)),
        ],
        out_specs=pl.BlockSpec((1, p_, c), lambda f: (f, 0, 0)),
        out_shape=jax.ShapeDtypeStruct(a3.shape, jnp.uint32),
        compiler_params=_cparams("arbitrary"),
        name="dft_stage2_conv",
    )(a3, tab["twc"], tab["tws"], tab["fb"], tab["fbi"], kspec)


def hyena_long_convs(xv, filt, bias, batch, seq_len):
    assert batch == 2, "the two sequences of the batch are packed as one complex sequence"
    c = xv.shape[-1]
    tab = _dft_tables(seq_len)
    n1, t1n, p_ = tab["n1"], tab["t1n"], FFT_P
    order = bias.shape[0]
    ak = dft_stage1(tab["w1k"], filt.reshape(1, n1, p_, order * c), 0)
    kspec = spectrum_forward(ak, tab)
    xv4 = xv.reshape(3, batch * t1n, p_, c)
    y4, y_idx = xv4, 2
    for o in range(order):
        a = dft_stage1(tab["w1"], y4, y_idx)
        b3 = spectrum_conv(a, kspec, o, tab)
        y = dft_inverse_stage1_gate(tab["w1i"], b3, xv4, o, y4, y_idx, bias[o])
        y4, y_idx = y[None], 0
    return y.reshape(batch * seq_len, c)


def kernel(x, c, ctx, c_ctx, ada_w, ada_b, norm_g, gmlp_w_in, gmlp_g_v, gmlp_w_s, gmlp_b_s, gmlp_w_out, lru_w_in, lru_conv_w, lru_conv_b, lru_w_a, lru_b_a, lru_w_x, lru_b_x, lru_lam, lru_w_out, hyena_w_in, hyena_conv_w, hyena_conv_b, hyena_f_w1, hyena_f_b1, hyena_f_w2, hyena_f_b2, hyena_f_w3, hyena_f_b3, hyena_f_freq, hyena_f_wout, hyena_bias, hyena_w_out, pool_w_in, pool_w_g, pool_b_g, pool_scale, pool_w_out, ffn_w_gate, ffn_w_up, ffn_w_down, moe_w_router, moe_w_gate, moe_w_up, moe_w_down):
    B, L, D = x.shape
    Lc = ctx.shape[1]
    depth = ada_w.shape[0]
    n_mixers = 4
    assert B + 1 <= SUBLANES

    cc = jnp.zeros((SUBLANES, D), F32).at[:B].set(c).at[B].set(c_ctx)
    ada = ada_all(cc, ada_w, ada_b).reshape(depth, SUBLANES, 6, D)

    xs = add_pos(x)
    xc = ctx.reshape(B * Lc, D)
    last_ctx = max([i for i in range(depth) if i % n_mixers == 1], default=-1)

    for i in range(depth):
        kind, j = i % n_mixers, i // n_mixers
        update_ctx = i < last_ctx
        read_ctx = i <= last_ctx
        lat = [ada[i, :B, q].reshape(B, 1, D) for q in range(6)]
        cx = [jnp.broadcast_to(ada[i, B:B + 1, q].reshape(1, 1, D), (B, 1, D)) for q in range(6)]
        g = norm_g[i]
        streams = [(xs, lat, L)]
        if read_ctx:
            streams.append((xc, cx, Lc))

        if kind == 0:
            outs = []
            for (s, m, sl) in streams[:1 + int(update_ctx)]:
                hw = norm_mod_matmul(s, g[0], m[0], m[1], gmlp_w_in[j], sl, BF16, act="gelu")
                a = gmlp_spatial(hw, gmlp_g_v[j], gmlp_w_s[j], gmlp_b_s[j])
                outs.append(matmul_norm_res([(a, a.shape[1], 0)], _identity_bf16, gmlp_w_out[j],
                                            g[1], m[2], s, sl))
        elif kind == 1:
            W = lru_w_in.shape[2] // 2
            sc_args = lambda d: (lru_conv_w[j], lru_conv_b[j], lru_w_a[j, d], lru_b_a[j, d],
                                 lru_w_x[j, d], lru_b_x[j, d], lru_lam[j, d])
            zc = norm_mod_matmul(xc, g[0], cx[0], cx[1], lru_w_in[j], Lc, F32)
            zero = jnp.zeros((B, W), F32)
            hf_c = lru_scan(zc, 1, W, *sc_args(0), zero, Lc, False)
            hb_c = lru_scan(zc, 1, W, *sc_args(1), zero, Lc, True)
            zl = norm_mod_matmul(xs, g[0], lat[0], lat[1], lru_w_in[j], L, F32)
            hf = lru_scan(zl, 1, W, *sc_args(0), hf_c.reshape(B, Lc, W)[:, -1], L, False)
            hb = lru_scan(zl, 1, W, *sc_args(1), hb_c.reshape(B, Lc, W)[:, 0], L, True)
            outs = [matmul_norm_res([(zl, W, 0), (hf, W, 0), (hb, W, 0)], _lru_out_prologue,
                                    lru_w_out[j], g[1], lat[2], xs, L)]
            if update_ctx:
                outs.append(matmul_norm_res([(zc, W, 0), (hf_c, W, 0), (hb_c, W, 0)], _lru_out_prologue,
                                            lru_w_out[j], g[1], cx[2], xc, Lc))
        elif kind == 2:
            W = hyena_w_out.shape[1]
            filt = hyena_filters(L, W, hyena_f_w1[j], hyena_f_b1[j], hyena_f_w2[j], hyena_f_b2[j],
                                 hyena_f_w3[j], hyena_f_b3[j], hyena_f_freq[j], hyena_f_wout[j])
            outs = []
            for (s, m, sl) in streams[:1 + int(update_ctx)]:
                nb = s.shape[0] // sl
                xv = norm_mod_matmul_conv(s, g[0], m[0], m[1], hyena_w_in[j], hyena_conv_w[j],
                                          hyena_conv_b[j], W, sl)
                fl = filt if sl == L else hyena_filters(
                    sl, W, hyena_f_w1[j], hyena_f_b1[j], hyena_f_w2[j], hyena_f_b2[j],
                    hyena_f_w3[j], hyena_f_b3[j], hyena_f_freq[j], hyena_f_wout[j])
                y = hyena_long_convs(xv, fl, hyena_bias[j], nb, sl)
                outs.append(matmul_norm_res([(y, W, 0)], _identity_bf16, hyena_w_out[j], g[1], m[2], s, sl))
        else:
            outs = []
            for (s, m, sl) in streams[:1 + int(update_ctx)]:
                p = norm_mod_matmul(s, g[0], m[0], m[1], pool_w_in[j], sl, F32)
                a = pool_mix(p, pool_w_g[j], pool_b_g[j], pool_scale[j], sl)
                outs.append(matmul_norm_res([(a, a.shape[1], 0)], _identity_bf16, pool_w_out[j],
                                            g[1], m[2], s, sl))
        xs = outs[0]
        if update_ctx:
            xc = outs[1]

        k = i // 2
        todo = [(xs, lat, L)] + ([(xc, cx, Lc)] if update_ctx else [])
        res = []
        for (s, m, sl) in todo:
            if i % 2 == 0:
                res.append(ffn_dense(s, g[2], m[3], m[4], ffn_w_gate[k], ffn_w_up[k], ffn_w_down[k],
                                     g[3], m[5], sl))
            else:
                res.append(moe_block(s, g[2], m[3], m[4], moe_w_router[k], moe_w_gate[k], moe_w_up[k],
                                     moe_w_down[k], g[3], m[5], sl))
        xs = res[0]
        if update_ctx:
            xc = res[1]
    return xs.reshape(B, L, D)
```

```python
import functools
import math

import jax
import jax.numpy as jnp
import numpy as np
from jax import lax
from jax.experimental import pallas as pl
from jax.experimental.pallas import tpu as pltpu

F32 = jnp.float32
BF16 = jnp.bfloat16
EPS = 1e-6

VMEM_LIMIT_BYTES = 52 * 1024 * 1024
LANES = 128
SUBLANES = 8

GRID_W = 64
GMLP_CHUNK = 128
GMLP_GROUPS = 8
LRU_HEADS = 8
LRU_C = 8.0
POOL_WINDOWS = (2, 4, 8, 16)
HYENA_BANDS = 16
HYENA_FAST_DECAY = 0.3
HYENA_SLOW_DECAY = 1.5
HYENA_DECAY_TARGET = 1e-2
N_EXPERTS = 8
TOP_K = 2
FFT_P = 128
HALO = SUBLANES


def _cparams(*sem):
    return pltpu.CompilerParams(dimension_semantics=sem, vmem_limit_bytes=VMEM_LIMIT_BYTES)


def _rms(x, g):
    return x * lax.rsqrt(jnp.mean(x * x, axis=-1, keepdims=True) + EPS) * g


def _gelu(x):
    return 0.5 * x * (1.0 + jnp.tanh(math.sqrt(2.0 / math.pi) * (x + 0.044715 * (x * x * x))))


def _silu(x):
    return x * (1.0 / (1.0 + jnp.exp(-x)))


def _sigmoid(x):
    return 1.0 / (1.0 + jnp.exp(-x))


def _bdot(a, b):
    return jnp.dot(a, b, preferred_element_type=F32)


def _hdot(a, b):
    return jnp.dot(a, b, preferred_element_type=F32, precision=lax.Precision.HIGHEST)


def _to_row_tiles(val, ref):
    rows = val.shape[0]
    nq = val.shape[1] // LANES
    for q in range(nq):
        ref[pl.ds(q, rows, stride=nq), :] = val[:, q * LANES:(q + 1) * LANES].astype(ref.dtype)


def _from_row_tiles(ref, dst_ref):
    rows = dst_ref.shape[0]
    nq = dst_ref.shape[1] // LANES
    for q in range(nq):
        dst_ref[:, q * LANES:(q + 1) * LANES] = ref[pl.ds(q, rows, stride=nq), :].astype(dst_ref.dtype)


def _row_tile(rows, want):
    t = min(rows, want)
    assert rows % t == 0, (rows, t)
    return t


def _ada_kernel(c_ref, w_ref, b_ref, o_ref):
    o_ref[0] = _hdot(_silu(c_ref[...]), w_ref[0]) + b_ref[0]


def ada_all(cc, ada_w, ada_b):
    depth, d, d6 = ada_w.shape
    nchunk = d6 // d
    return pl.pallas_call(
        _ada_kernel,
        grid=(depth, nchunk),
        in_specs=[
            pl.BlockSpec((SUBLANES, d), lambda i, j: (0, 0)),
            pl.BlockSpec((1, d, d), lambda i, j: (i, 0, j)),
            pl.BlockSpec((1, 1, d), lambda i, j: (i, 0, j)),
        ],
        out_specs=pl.BlockSpec((1, SUBLANES, d), lambda i, j: (i, 0, j)),
        out_shape=jax.ShapeDtypeStruct((depth, SUBLANES, d6), F32),
        compiler_params=_cparams("parallel", "parallel"),
        name="ada",
    )(cc, ada_w, ada_b.reshape(depth, 1, d6))


def _pos_kernel(x_ref, rt_ref, ct_ref, o_ref):
    half = rt_ref.shape[-1]
    x = x_ref[...]
    o_ref[:, :, :half] = x[:, :, :half] + rt_ref[...]
    o_ref[:, :, half:] = x[:, :, half:] + ct_ref[...][None]


def add_pos(x):
    b, l, d = x.shape
    rows = l // GRID_W
    quarter = d // 4
    omega = 1.0 / (10000.0 ** (jnp.arange(quarter, dtype=F32) / quarter))

    def sincos(p):
        ang = p.reshape(-1, 1) * omega[None, :]
        return jnp.concatenate([jnp.sin(ang), jnp.cos(ang)], axis=-1)

    rtab = sincos(jnp.arange(rows, dtype=F32)).reshape(rows, 1, 2 * quarter)
    ctab = sincos(jnp.arange(GRID_W, dtype=F32))
    x3 = x.reshape(b * rows, GRID_W, d)
    tr = _row_tile(rows, 16)
    nrt = rows // tr
    out = pl.pallas_call(
        _pos_kernel,
        grid=(b * nrt,),
        in_specs=[
            pl.BlockSpec((tr, GRID_W, d), lambda i: (i, 0, 0)),
            pl.BlockSpec((tr, 1, 2 * quarter), lambda i: (i % nrt, 0, 0)),
            pl.BlockSpec((GRID_W, 2 * quarter), lambda i: (0, 0)),
        ],
        out_specs=pl.BlockSpec((tr, GRID_W, d), lambda i: (i, 0, 0)),
        out_shape=jax.ShapeDtypeStruct(x3.shape, F32),
        compiler_params=_cparams("parallel"),
        name="add_pos",
    )(x3, rtab, ctab)
    return out.reshape(b * l, d)


def _nmm_kernel(x_ref, g_ref, sh_ref, sc_ref, w_ref, o_ref, h_scr, *, act):
    @pl.when(pl.program_id(1) == 0)
    def _():
        h = _rms(x_ref[...], g_ref[...]) * (1.0 + sc_ref[0]) + sh_ref[0]
        h_scr[...] = h.astype(BF16)

    y = _bdot(h_scr[...], w_ref[...])
    if act == "gelu":
        y = _gelu(y)
    o_ref[...] = y.astype(o_ref.dtype)


def norm_mod_matmul(x, g, shift, scale, w, rows_per_mod, out_dtype, act=None, tm=512, tn=2048):
    r, d = x.shape
    n = w.shape[1]
    tm = _row_tile(min(r, rows_per_mod), tm)
    tn = min(tn, n)
    assert n % tn == 0
    mod_map = lambda i, j: ((i * tm) // rows_per_mod, 0, 0)
    return pl.pallas_call(
        functools.partial(_nmm_kernel, act=act),
        grid=(r // tm, n // tn),
        in_specs=[
            pl.BlockSpec((tm, d), lambda i, j: (i, 0)),
            pl.BlockSpec((1, d), lambda i, j: (0, 0)),
            pl.BlockSpec((1, 1, d), mod_map),
            pl.BlockSpec((1, 1, d), mod_map),
            pl.BlockSpec((d, tn), lambda i, j: (0, j)),
        ],
        out_specs=pl.BlockSpec((tm, tn), lambda i, j: (i, j)),
        out_shape=jax.ShapeDtypeStruct((r, n), out_dtype),
        scratch_shapes=[pltpu.VMEM((tm, d), BF16)],
        compiler_params=_cparams("parallel", "arbitrary"),
        name="norm_mod_matmul",
    )(x, g.reshape(1, d), shift, scale, w.astype(BF16))


HALO_BF16 = 2 * SUBLANES


def _nmm_conv_kernel(x_ref, xp_ref, xn_ref, g_ref, sh_ref, sc_ref, w_ref, cw_ref, cb_ref, o_ref,
                     h_scr, z_scr, *, tm, seq_len):
    i = pl.program_id(0)
    hl = HALO_BF16

    @pl.when(pl.program_id(1) == 0)
    def _():
        def nm(x):
            return (_rms(x, g_ref[...]) * (1.0 + sc_ref[0]) + sh_ref[0]).astype(BF16)
        h_scr[0:hl, :] = nm(xp_ref[...])
        h_scr[hl:hl + tm, :] = nm(x_ref[...])
        h_scr[hl + tm:hl + tm + hl, :] = nm(xn_ref[...])

    z = _bdot(h_scr[...], w_ref[...])
    first = (i * tm) % seq_len == 0
    last = ((i + 1) * tm) % seq_len == 0
    row = lax.broadcasted_iota(jnp.int32, (tm + 2 * hl, 1), 0)
    outside = jnp.logical_or(jnp.logical_and(first, row < hl), jnp.logical_and(last, row >= hl + tm))
    z_scr[...] = jnp.where(outside, 0.0, z)
    kk = cw_ref.shape[0]
    left = kk // 2
    y = cb_ref[...] + cw_ref[0:1, :] * z_scr[pl.ds(hl - left, tm), :]
    for k in range(1, kk):
        y = y + cw_ref[k:k + 1, :] * z_scr[pl.ds(hl - left + k, tm), :]
    o_ref[0] = y


def norm_mod_matmul_conv(x, g, shift, scale, w, conv_w, conv_b, width, seq_len, tm=512):
    r, d = x.shape
    n = w.shape[1]
    nsplit = n // width
    tm = _row_tile(seq_len, tm)
    hl = HALO_BF16
    per = tm // hl
    n_row_blocks = r // hl
    kk = conv_w.shape[0]
    mod_map = lambda i, j: ((i * tm) // seq_len, 0, 0)
    return pl.pallas_call(
        functools.partial(_nmm_conv_kernel, tm=tm, seq_len=seq_len),
        grid=(r // tm, nsplit),
        in_specs=[
            pl.BlockSpec((tm, d), lambda i, j: (i, 0)),
            pl.BlockSpec((hl, d), lambda i, j: (jnp.maximum(i * per - 1, 0), 0)),
            pl.BlockSpec((hl, d), lambda i, j: (jnp.minimum((i + 1) * per, n_row_blocks - 1), 0)),
            pl.BlockSpec((1, d), lambda i, j: (0, 0)),
            pl.BlockSpec((1, 1, d), mod_map),
            pl.BlockSpec((1, 1, d), mod_map),
            pl.BlockSpec((d, width), lambda i, j: (0, j)),
            pl.BlockSpec((kk, width), lambda i, j: (0, j)),
            pl.BlockSpec((1, width), lambda i, j: (0, j)),
        ],
        out_specs=pl.BlockSpec((1, tm, width), lambda i, j: (j, i, 0)),
        out_shape=jax.ShapeDtypeStruct((nsplit, r, width), F32),
        scratch_shapes=[pltpu.VMEM((tm + 2 * hl, d), BF16), pltpu.VMEM((tm + 2 * hl, width), F32)],
        compiler_params=_cparams("parallel", "arbitrary"),
        name="norm_mod_matmul_conv",
    )(x, x, x, g.reshape(1, d), shift, scale, w.astype(BF16), conv_w, conv_b.reshape(1, n))


def _mnr_kernel(*refs, n_in, prologue):
    in_refs = refs[:n_in]
    w_ref, g_ref, gate_ref, x_ref, o_ref = refs[n_in:]
    a = prologue(*[r[...] for r in in_refs])
    y = _bdot(a, w_ref[...])
    o_ref[...] = x_ref[...] + gate_ref[0] * _rms(y, g_ref[...])


def matmul_norm_res(inputs, prologue, w, g, gate, x, rows_per_mod, tm=512):
    r, d = x.shape
    k = w.shape[0]
    tm = _row_tile(min(r, rows_per_mod), tm)
    mod_map = lambda i: ((i * tm) // rows_per_mod, 0, 0)
    in_specs = [pl.BlockSpec((tm, wd), functools.partial(lambda i, cb: (i, cb), cb=cb))
                for (_, wd, cb) in inputs]
    in_specs += [
        pl.BlockSpec((k, d), lambda i: (0, 0)),
        pl.BlockSpec((1, d), lambda i: (0, 0)),
        pl.BlockSpec((1, 1, d), mod_map),
        pl.BlockSpec((tm, d), lambda i: (i, 0)),
    ]
    return pl.pallas_call(
        functools.partial(_mnr_kernel, n_in=len(inputs), prologue=prologue),
        grid=(r // tm,),
        in_specs=in_specs,
        out_specs=pl.BlockSpec((tm, d), lambda i: (i, 0)),
        out_shape=jax.ShapeDtypeStruct((r, d), F32),
        compiler_params=_cparams("parallel"),
        name="matmul_norm_res",
    )(*[a for (a, _, _) in inputs], w.astype(BF16), g.reshape(1, d), gate, x)


def _identity_bf16(a):
    return a.astype(BF16)


def _swiglu_step(h_scr, wg_ref, wu_ref, wd_ref, acc_scr, j):
    h = h_scr[...]
    t = (_silu(_bdot(h, wg_ref[...].astype(BF16))) * _bdot(h, wu_ref[...].astype(BF16))).astype(BF16)
    part = _bdot(t, wd_ref[...].astype(BF16))

    @pl.when(j == 0)
    def _():
        acc_scr[...] = part

    @pl.when(j > 0)
    def _():
        acc_scr[...] += part


def _ffn_dense_kernel(x_ref, g1_ref, sh_ref, sc_ref, wg_ref, wu_ref, wd_ref, g2_ref, gate_ref,
                      o_ref, h_scr, acc_scr):
    j = pl.program_id(1)

    @pl.when(j == 0)
    def _():
        h = _rms(x_ref[...], g1_ref[...]) * (1.0 + sc_ref[0]) + sh_ref[0]
        h_scr[...] = h.astype(BF16)

    _swiglu_step(h_scr, wg_ref, wu_ref, wd_ref, acc_scr, j)

    @pl.when(j == pl.num_programs(1) - 1)
    def _():
        o_ref[...] = x_ref[...] + gate_ref[0] * _rms(acc_scr[...], g2_ref[...])


def ffn_dense(x, g1, shift, scale, wg, wu, wd, g2, gate, rows_per_mod, tm=512, tf=1408):
    r, d = x.shape
    ff = wg.shape[1]
    tm = _row_tile(min(r, rows_per_mod), tm)
    assert ff % tf == 0
    mod_map = lambda i, j: ((i * tm) // rows_per_mod, 0, 0)
    return pl.pallas_call(
        _ffn_dense_kernel,
        grid=(r // tm, ff // tf),
        in_specs=[
            pl.BlockSpec((tm, d), lambda i, j: (i, 0)),
            pl.BlockSpec((1, d), lambda i, j: (0, 0)),
            pl.BlockSpec((1, 1, d), mod_map),
            pl.BlockSpec((1, 1, d), mod_map),
            pl.BlockSpec((d, tf), lambda i, j: (0, j)),
            pl.BlockSpec((d, tf), lambda i, j: (0, j)),
            pl.BlockSpec((tf, d), lambda i, j: (j, 0)),
            pl.BlockSpec((1, d), lambda i, j: (0, 0)),
            pl.BlockSpec((1, 1, d), mod_map),
        ],
        out_specs=pl.BlockSpec((tm, d), lambda i, j: (i, 0)),
        out_shape=jax.ShapeDtypeStruct((r, d), F32),
        scratch_shapes=[pltpu.VMEM((tm, d), BF16), pltpu.VMEM((tm, d), F32)],
        compiler_params=_cparams("parallel", "arbitrary"),
        name="ffn_dense",
    )(x, g1.reshape(1, d), shift, scale, wg.astype(BF16), wu.astype(BF16), wd.astype(BF16),
      g2.reshape(1, d), gate)


def _ffn_routed_kernel(te_ref, nu_ref, idx0_ref, idxn_ref, h_hbm, wg_ref, wu_ref, wd_ref, o_ref,
                       xbuf, sems, h_scr, acc_scr, *, tm, nq, issue_steps):
    i = pl.program_id(0)
    j = pl.program_id(1)
    last = pl.num_programs(1) - 1
    used = i < nu_ref[0]
    slot = i % 2
    per = tm // issue_steps

    def row_copy(idx_ref, r, s):
        s0 = pl.multiple_of(idx_ref[r] * nq, nq)
        d0 = pl.multiple_of(r * nq, nq)
        return pltpu.make_async_copy(h_hbm.at[pl.ds(s0, nq)], xbuf.at[s, pl.ds(d0, nq)], sems.at[s])

    @pl.when(used)
    def _():
        @pl.when(jnp.logical_and(i == 0, j == 0))
        def _():
            def issue(r, carry):
                row_copy(idx0_ref, r, 0).start()
                return carry
            lax.fori_loop(0, tm, issue, 0)

        @pl.when(j == 0)
        def _():
            pltpu.make_async_copy(h_hbm.at[pl.ds(0, tm * nq)], xbuf.at[slot], sems.at[slot]).wait()
            _from_row_tiles(xbuf.at[slot], h_scr)

        @pl.when(jnp.logical_and(i + 1 < nu_ref[0], j < issue_steps))
        def _():
            for k in range(per):
                row_copy(idxn_ref, j * per + k, 1 - slot).start(priority=k % 2)

        _swiglu_step(h_scr, wg_ref.at[0], wu_ref.at[0], wd_ref.at[0], acc_scr, j)

        @pl.when(j == last)
        def _():
            _to_row_tiles(acc_scr[...], o_ref)

    @pl.when(jnp.logical_and(jnp.logical_not(used), j == last))
    def _():
        o_ref[...] = jnp.zeros_like(o_ref)


def ffn_routed(h_rt, src, tile_expert, n_used, wg, wu, wd, tm, tf=512):
    dq = wg.shape[1] // LANES
    d = dq * LANES
    p = src.shape[0]
    ff = wg.shape[2]
    assert ff % tf == 0 and p % tm == 0
    n_tiles = p // tm
    issue_steps = min(4, ff // tf)
    assert tm % issue_steps == 0
    grid_spec = pltpu.PrefetchScalarGridSpec(
        num_scalar_prefetch=2,
        grid=(n_tiles, ff // tf),
        in_specs=[
            pl.BlockSpec((tm,), lambda i, j, te, nu: (0,), memory_space=pltpu.SMEM),
            pl.BlockSpec((tm,), lambda i, j, te, nu: (jnp.minimum(i + 1, n_tiles - 1),),
                         memory_space=pltpu.SMEM),
            pl.BlockSpec(memory_space=pl.ANY),
            pl.BlockSpec((1, d, tf), lambda i, j, te, nu: (te[i], 0, j)),
            pl.BlockSpec((1, d, tf), lambda i, j, te, nu: (te[i], 0, j)),
            pl.BlockSpec((1, tf, d), lambda i, j, te, nu: (te[i], j, 0)),
        ],
        out_specs=pl.BlockSpec((tm * dq, LANES), lambda i, j, te, nu: (i, 0)),
        scratch_shapes=[pltpu.VMEM((2, tm * dq, LANES), F32), pltpu.SemaphoreType.DMA((2,)),
                        pltpu.VMEM((tm, d), BF16), pltpu.VMEM((tm, d), F32)],
    )
    return pl.pallas_call(
        functools.partial(_ffn_routed_kernel, tm=tm, nq=dq, issue_steps=issue_steps),
        grid_spec=grid_spec,
        out_shape=jax.ShapeDtypeStruct((p * dq, LANES), F32),
        compiler_params=_cparams("arbitrary", "arbitrary"),
        name="ffn_routed",
    )(tile_expert, n_used, src, src, h_rt, wg, wu, wd)


ROUTE_IDX0 = N_EXPERTS
ROUTE_P0 = N_EXPERTS + TOP_K


def _router_kernel(x_ref, g_ref, sh_ref, sc_ref, wr_ref, h_ref, r_ref):
    h = _rms(x_ref[...], g_ref[...]) * (1.0 + sc_ref[0]) + sh_ref[0]
    _to_row_tiles(h, h_ref)
    logits = _hdot(h, wr_ref[...])
    lane = lax.broadcasted_iota(jnp.int32, logits.shape, 1)
    neg = jnp.float32(-jnp.inf)
    big = jnp.int32(LANES)
    lg = jnp.where(lane < N_EXPERTS, logits, neg)
    m1 = jnp.max(lg, axis=-1, keepdims=True)
    i1 = jnp.min(jnp.where(lg == m1, lane, big), axis=-1, keepdims=True)
    lg2 = jnp.where(lane == i1, neg, lg)
    m2 = jnp.max(lg2, axis=-1, keepdims=True)
    i2 = jnp.min(jnp.where(lg2 == m2, lane, big), axis=-1, keepdims=True)
    e2 = jnp.exp(m2 - m1)
    p1 = 1.0 / (1.0 + e2)
    p2 = e2 / (1.0 + e2)
    out = jnp.where(lane == ROUTE_IDX0, i1.astype(F32), 0.0)
    out = jnp.where(lane == ROUTE_IDX0 + 1, i2.astype(F32), out)
    out = jnp.where(lane == ROUTE_P0, p1, out)
    out = jnp.where(lane == ROUTE_P0 + 1, p2, out)
    r_ref[...] = out


def router(x, g, shift, scale, w_router, rows_per_mod, tm=512):
    r, d = x.shape
    tm = _row_tile(min(r, rows_per_mod), tm)
    wr = jnp.zeros((d, LANES), F32).at[:, :N_EXPERTS].set(w_router)
    mod_map = lambda i: ((i * tm) // rows_per_mod, 0, 0)
    return pl.pallas_call(
        _router_kernel,
        grid=(r // tm,),
        in_specs=[
            pl.BlockSpec((tm, d), lambda i: (i, 0)),
            pl.BlockSpec((1, d), lambda i: (0, 0)),
            pl.BlockSpec((1, 1, d), mod_map),
            pl.BlockSpec((1, 1, d), mod_map),
            pl.BlockSpec((d, LANES), lambda i: (0, 0)),
        ],
        out_specs=[pl.BlockSpec((tm * (d // LANES), LANES), lambda i: (i, 0)),
                   pl.BlockSpec((tm, LANES), lambda i: (i, 0))],
        out_shape=[jax.ShapeDtypeStruct((r * (d // LANES), LANES), F32), jax.ShapeDtypeStruct((r, LANES), F32)],
        compiler_params=_cparams("parallel"),
        name="router",
    )(x, g.reshape(1, d), shift, scale, wr)


GATHER_UNROLL = 8


def _gather_kernel(idx_ref, src_ref, o_ref, sem, *, rows, nq):
    def issue(rb, carry):
        for k in range(GATHER_UNROLL):
            r = GATHER_UNROLL * rb + k
            s0 = pl.multiple_of(idx_ref[r] * nq, nq)
            d0 = pl.multiple_of(r * nq, nq)
            pltpu.make_async_copy(src_ref.at[pl.ds(s0, nq)], o_ref.at[pl.ds(d0, nq)], sem).start(priority=k % 2)
        return carry

    lax.fori_loop(0, rows // GATHER_UNROLL, issue, 0)
    pltpu.make_async_copy(src_ref.at[pl.ds(0, rows * nq)], o_ref, sem).wait()


def gather_rows(src, idx, nq, rows_per_step=512):
    n = idx.shape[0]
    rows = _row_tile(n, rows_per_step)
    return pl.pallas_call(
        functools.partial(_gather_kernel, rows=rows, nq=nq),
        grid=(n // rows,),
        in_specs=[
            pl.BlockSpec((rows,), lambda i: (i,), memory_space=pltpu.SMEM),
            pl.BlockSpec(memory_space=pl.ANY),
        ],
        out_specs=pl.BlockSpec((rows * nq, LANES), lambda i: (i, 0)),
        out_shape=jax.ShapeDtypeStruct((n * nq, LANES), src.dtype),
        scratch_shapes=[pltpu.SemaphoreType.DMA(())],
        compiler_params=_cparams("arbitrary"),
        name="gather_rows",
    )(idx, src)


def _combine_kernel(y0_ref, y1_ref, r_ref, g_ref, gate_ref, x_ref, o_ref, y_scr):
    rt = r_ref[...]
    p0 = rt[:, ROUTE_P0:ROUTE_P0 + 1]
    p1 = rt[:, ROUTE_P0 + 1:ROUTE_P0 + 2]
    _from_row_tiles(y0_ref, y_scr)
    y = p0 * y_scr[...]
    _from_row_tiles(y1_ref, y_scr)
    y = y + p1 * y_scr[...]
    o_ref[...] = x_ref[...] + gate_ref[0] * _rms(y, g_ref[...])


def combine(yg, route, g, gate, x, rows_per_mod, tm=512):
    r, d = x.shape
    tm = _row_tile(min(r, rows_per_mod), tm)
    nt = r // tm
    mod_map = lambda i: ((i * tm) // rows_per_mod, 0, 0)
    return pl.pallas_call(
        _combine_kernel,
        grid=(nt,),
        in_specs=[
            pl.BlockSpec((tm * (d // LANES), LANES), lambda i: (i, 0)),
            pl.BlockSpec((tm * (d // LANES), LANES), lambda i: (i + nt, 0)),
            pl.BlockSpec((tm, LANES), lambda i: (i, 0)),
            pl.BlockSpec((1, d), lambda i: (0, 0)),
            pl.BlockSpec((1, 1, d), mod_map),
            pl.BlockSpec((tm, d), lambda i: (i, 0)),
        ],
        out_specs=pl.BlockSpec((tm, d), lambda i: (i, 0)),
        out_shape=jax.ShapeDtypeStruct((r, d), F32),
        scratch_shapes=[pltpu.VMEM((tm, d), F32)],
        compiler_params=_cparams("parallel"),
        name="moe_combine",
    )(yg, yg, route, g.reshape(1, d), gate, x)


def moe_block(x, g1, shift, scale, w_router, wg_all, wu_all, wd_all, layer, g2, gate, rows_per_mod, tm=1024):
    n, d = x.shape
    ff = wg_all.shape[-1]
    wg = wg_all.reshape(-1, d, ff)
    wu = wu_all.reshape(-1, d, ff)
    wd = wd_all.reshape(-1, ff, d)
    h, route = router(x, g1, shift, scale, w_router, rows_per_mod)
    eidx = route[:, ROUTE_IDX0:ROUTE_IDX0 + TOP_K].astype(jnp.int32)
    flat_e = eidx.T.reshape(-1)
    onehot = (flat_e[:, None] == jnp.arange(N_EXPERTS, dtype=jnp.int32)[None, :]).astype(jnp.int32)
    csum = jnp.cumsum(onehot, axis=0)
    counts = csum[-1]
    rank = jnp.take_along_axis(csum, flat_e[:, None], axis=1)[:, 0] - 1
    tiles_per_e = (counts + tm - 1) // tm
    tile_end = jnp.cumsum(tiles_per_e)
    tile_start = tile_end - tiles_per_e
    slot = tile_start[flat_e] * tm + rank
    n_tiles = (TOP_K * n) // tm + N_EXPERTS
    tok = jnp.tile(jnp.arange(n, dtype=jnp.int32), TOP_K)
    src = jnp.zeros((n_tiles * tm,), jnp.int32).at[slot].set(tok)
    tile_ids = jnp.arange(n_tiles, dtype=jnp.int32)
    tile_expert = jnp.minimum(jnp.sum((tile_ids[:, None] >= tile_end[None, :]).astype(jnp.int32), axis=1),
                              N_EXPERTS - 1).astype(jnp.int32)
    n_used = tile_end[-1:].astype(jnp.int32)

    nq = d // LANES
    ys = ffn_routed(h, src, tile_expert + layer * N_EXPERTS, n_used, wg, wu, wd, tm)
    yg = gather_rows(ys, slot.astype(jnp.int32), nq)
    return combine(yg, route, g2, gate, x, rows_per_mod)


def _gmlp_kernel(u_ref, v_ref, gv_ref, ws_ref, bs_ref, o_ref, vn_scr):
    v = v_ref[...].astype(F32)
    mu = jnp.mean(v, axis=-1, keepdims=True)
    vc = v - mu
    vn = vc * lax.rsqrt(jnp.mean(vc * vc, axis=-1, keepdims=True) + EPS) * gv_ref[...]
    vn_scr[...] = vn.astype(BF16)
    tm, width = vn_scr.shape
    gw = width // GMLP_GROUPS
    for n in range(tm // GMLP_CHUNK):
        rs = slice(n * GMLP_CHUNK, (n + 1) * GMLP_CHUNK)
        for g in range(GMLP_GROUPS):
            cs = slice(g * gw, (g + 1) * gw)
            m = _bdot(ws_ref[g], vn_scr[rs, cs]) + bs_ref[g]
            o_ref[rs, cs] = (u_ref[rs, cs].astype(F32) * m).astype(BF16)


def gmlp_spatial(hw, g_v, w_s, b_s, tm=512):
    r, w2 = hw.shape
    width = w2 // 2
    gw = width // GMLP_GROUPS
    tm = _row_tile(r, tm)
    bsb = jnp.broadcast_to(b_s[:, :, None], (GMLP_GROUPS, GMLP_CHUNK, gw)).astype(F32)
    return pl.pallas_call(
        _gmlp_kernel,
        grid=(r // tm,),
        in_specs=[
            pl.BlockSpec((tm, width), lambda i: (i, 0)),
            pl.BlockSpec((tm, width), lambda i: (i, 1)),
            pl.BlockSpec((1, width), lambda i: (0, 0)),
            pl.BlockSpec((GMLP_GROUPS, GMLP_CHUNK, GMLP_CHUNK), lambda i: (0, 0, 0)),
            pl.BlockSpec((GMLP_GROUPS, GMLP_CHUNK, gw), lambda i: (0, 0, 0)),
        ],
        out_specs=pl.BlockSpec((tm, width), lambda i: (i, 0)),
        out_shape=jax.ShapeDtypeStruct((r, width), BF16),
        scratch_shapes=[pltpu.VMEM((tm, width), BF16)],
        compiler_params=_cparams("parallel"),
        name="gmlp_spatial",
    )(hw, hw, g_v.reshape(1, width), w_s.astype(BF16), bsb)


def _halo_specs(tm, width, col_block, tile_of, n_row_blocks):
    per = tm // HALO
    cur = pl.BlockSpec((tm, width), lambda i: (tile_of(i), col_block))
    prev = pl.BlockSpec((HALO, width), lambda i: (jnp.maximum(tile_of(i) * per - 1, 0), col_block))
    nxt = pl.BlockSpec((HALO, width),
                       lambda i: (jnp.minimum((tile_of(i) + 1) * per, n_row_blocks - 1), col_block))
    return [cur, prev, nxt]


def _fill_ext(ext, cur_ref, prev_ref, next_ref, tile, tm, seq_len):
    first = (tile * tm) % seq_len == 0
    last = ((tile + 1) * tm) % seq_len == 0
    ext[0:HALO, :] = jnp.where(first, 0.0, prev_ref[...])
    ext[HALO:HALO + tm, :] = cur_ref[...]
    ext[HALO + tm:HALO + tm + HALO, :] = jnp.where(last, 0.0, next_ref[...])


def _lru_kernel(xb_ref, xp_ref, xn_ref, cw_ref, cb_ref, wax_ref, ba_ref, bx_ref, lam_ref, h0_ref,
                o_ref, ext, a_scr, b_scr, carry, *, tm, seq_len, n_tiles, reverse):
    i = pl.program_id(0)
    tile = (n_tiles - 1 - i) if reverse else i
    _fill_ext(ext, xb_ref, xp_ref, xn_ref, tile, tm, seq_len)
    kk = cw_ref.shape[0]
    left = kk // 2
    xc = cb_ref[...] + cw_ref[0:1, :] * ext[pl.ds(HALO - left, tm), :]
    for k in range(1, kk):
        xc = xc + cw_ref[k:k + 1, :] * ext[pl.ds(HALO - left + k, tm), :]

    width = xc.shape[1]
    hd = width // LRU_HEADS
    lam = lam_ref[...]
    sp = jnp.maximum(-lam, 0.0) + jnp.log(1.0 + jnp.exp(-jnp.abs(lam)))
    for hh in range(LRU_HEADS):
        cs = slice(hh * hd, (hh + 1) * hd)
        xh = xc[:, cs]
        pre = _bdot(xh.astype(BF16), wax_ref[hh])
        rg = _sigmoid(pre[:, :hd] + ba_ref[:, cs])
        ig = _sigmoid(pre[:, hd:] + bx_ref[:, cs])
        a = jnp.exp(-LRU_C * rg * sp[:, cs])
        a_scr[:, cs] = a
        b_scr[:, cs] = jnp.sqrt(1.0 - a * a) * (ig * xh)

    first = (tile * tm) % seq_len == 0
    last = ((tile + 1) * tm) % seq_len == 0

    @pl.when(last if reverse else first)
    def _():
        carry[...] = jnp.broadcast_to(h0_ref[0], carry.shape)

    row = lax.broadcasted_iota(jnp.int32, (SUBLANES, width), 0)
    nblk = tm // SUBLANES

    def body(k, c):
        blk = (nblk - 1 - k) if reverse else k
        r0 = pl.multiple_of(blk * SUBLANES, SUBLANES)
        a = a_scr[pl.ds(r0, SUBLANES), :]
        b = b_scr[pl.ds(r0, SUBLANES), :]
        for s in (1, 2, 4):
            shift = (SUBLANES - s) if reverse else s
            a_sh = pltpu.roll(a, shift, 0)
            b_sh = pltpu.roll(b, shift, 0)
            m = (row < SUBLANES - s) if reverse else (row >= s)
            b = jnp.where(m, a * b_sh + b, b)
            a = jnp.where(m, a * a_sh, a)
        h = a * c + b
        o_ref[pl.ds(r0, SUBLANES), :] = h
        edge = h[0:1, :] if reverse else h[SUBLANES - 1:SUBLANES, :]
        return jnp.broadcast_to(edge, c.shape)

    carry[...] = lax.fori_loop(0, nblk, body, carry[...])


def lru_scan(z, col_block, width, conv_w, conv_b, w_a, b_a, w_x, b_x, lam, h0, seq_len, reverse, tm=512):
    r = z.shape[0]
    tm = _row_tile(seq_len, tm)
    n_tiles = r // tm
    nb = r // seq_len
    tile_of = (lambda i: n_tiles - 1 - i) if reverse else (lambda i: i)
    wax = jnp.concatenate([w_a, w_x], axis=-1).astype(BF16)
    kk = conv_w.shape[0]
    hd = width // LRU_HEADS
    const2 = lambda i: (0, 0)
    return pl.pallas_call(
        functools.partial(_lru_kernel, tm=tm, seq_len=seq_len, n_tiles=n_tiles, reverse=reverse),
        grid=(n_tiles,),
        in_specs=_halo_specs(tm, width, col_block, tile_of, r // HALO) + [
            pl.BlockSpec((kk, width), const2),
            pl.BlockSpec((1, width), const2),
            pl.BlockSpec((LRU_HEADS, hd, 2 * hd), lambda i: (0, 0, 0)),
            pl.BlockSpec((1, width), const2),
            pl.BlockSpec((1, width), const2),
            pl.BlockSpec((1, width), const2),
            pl.BlockSpec((1, 1, width), lambda i: ((tile_of(i) * tm) // seq_len, 0, 0)),
        ],
        out_specs=pl.BlockSpec((tm, width), lambda i: (tile_of(i), 0)),
        out_shape=jax.ShapeDtypeStruct((r, width), F32),
        scratch_shapes=[
            pltpu.VMEM((tm + 2 * HALO, width), F32),
            pltpu.VMEM((tm, width), F32),
            pltpu.VMEM((tm, width), F32),
            pltpu.VMEM((SUBLANES, width), F32),
        ],
        compiler_params=_cparams("arbitrary"),
        name="lru_scan_bwd" if reverse else "lru_scan_fwd",
    )(z, z, z, conv_w, conv_b.reshape(1, width), wax, b_a.reshape(1, width), b_x.reshape(1, width),
      lam.reshape(1, width), h0.reshape(nb, 1, width))


def _lru_out_prologue(gate, hf, hb):
    return (_gelu(gate) * (hf + hb)).astype(BF16)


def _pool_kernel(p_ref, pp_ref, pn_ref, wg_ref, bg_ref, sc_ref, o_ref, ext, *, tm, seq_len):
    i = pl.program_id(0)
    _fill_ext(ext, p_ref, pp_ref, pn_ref, i, tm, seq_len)
    width = o_ref.shape[1]
    gw = width // len(POOL_WINDOWS)
    t = ((i * tm) % seq_len + lax.broadcasted_iota(jnp.int32, (tm, 1), 0))
    for g, win in enumerate(POOL_WINDOWS):
        half = win // 2
        cs = slice(g * gw, (g + 1) * gw)
        s = ext[pl.ds(HALO - half, tm), cs]
        for k in range(1 - half, half):
            s = s + ext[pl.ds(HALO + k, tm), cs]
        cnt = (jnp.minimum(t + half, seq_len) - jnp.maximum(t - half, 0)).astype(F32)
        q = s / cnt - ext[pl.ds(HALO, tm), cs]
        y = _bdot(q.astype(BF16), wg_ref[g]) + bg_ref[:, cs]
        o_ref[:, cs] = (y * sc_ref[:, cs]).astype(BF16)


def pool_mix(p, w_g, b_g, scale, seq_len, tm=512):
    r, width = p.shape
    assert max(POOL_WINDOWS) // 2 <= HALO
    tm = _row_tile(seq_len, tm)
    ng, gw, _ = w_g.shape
    const2 = lambda i: (0, 0)
    return pl.pallas_call(
        functools.partial(_pool_kernel, tm=tm, seq_len=seq_len),
        grid=(r // tm,),
        in_specs=_halo_specs(tm, width, 0, lambda i: i, r // HALO) + [
            pl.BlockSpec((ng, gw, gw), lambda i: (0, 0, 0)),
            pl.BlockSpec((1, width), const2),
            pl.BlockSpec((1, width), const2),
        ],
        out_specs=pl.BlockSpec((tm, width), lambda i: (i, 0)),
        out_shape=jax.ShapeDtypeStruct((r, width), BF16),
        scratch_shapes=[pltpu.VMEM((tm + 2 * HALO, width), F32)],
        compiler_params=_cparams("parallel"),
        name="pool_mix",
    )(p, p, p, w_g.astype(BF16), b_g.reshape(1, width), scale.reshape(1, width))


def _hyfilter_kernel(z_ref, w1_ref, b1_ref, w2_ref, b2_ref, w3_ref, b3_ref, fr_ref, wo_ref, dl_ref,
                     o_ref, *, n_out, seq_len):
    z = z_ref[...]
    tm = z.shape[0]
    t = pl.program_id(0) * tm + lax.broadcasted_iota(jnp.int32, (tm, 1), 0)
    keep = (t != seq_len).astype(F32)
    h = jnp.sin(fr_ref[0:1, :] * (_hdot(z, w1_ref[...]) + b1_ref[...]))
    h = jnp.sin(fr_ref[1:2, :] * (_hdot(h, w2_ref[...]) + b2_ref[...]))
    h = jnp.sin(fr_ref[2:3, :] * (_hdot(h, w3_ref[...]) + b3_ref[...]))
    window = jnp.exp(-z[:, 0:1] * dl_ref[...]) * keep
    width = dl_ref.shape[1]
    hb = h.astype(BF16)
    for q in range(n_out):
        cs = slice(q * width, (q + 1) * width)
        o_ref[:, cs] = _bdot(hb, wo_ref[:, cs]) * window


def hyena_filters(seq_len, width, f_w1, f_b1, f_w2, f_b2, f_w3, f_b3, f_freq, f_wout, tm=512):
    n = 2 * seq_len
    rows = np.arange(n)
    lag = np.where(rows < seq_len, rows, np.minimum(n - rows, seq_len - 1))
    t = (jnp.asarray(lag, dtype=F32) / (seq_len - 1))[:, None]
    w = 2.0 * math.pi * jnp.asarray(lag, dtype=F32)[:, None] / seq_len
    bands = jnp.linspace(1e-4, HYENA_BANDS - 1, HYENA_BANDS, dtype=F32)[None, :]
    z = jnp.concatenate([t, jnp.cos(bands * w), jnp.sin(-bands * w)], axis=-1)
    emb = z.shape[1]
    hid = f_w1.shape[1]
    zp = jnp.zeros((n, LANES), F32).at[:, :emb].set(z)
    order = f_wout.shape[1] // (2 * width)
    wo = f_wout.reshape(hid, order, 2, width).transpose(2, 0, 1, 3).reshape(2, hid, order * width)
    w1p = jnp.zeros((LANES, hid), F32).at[:emb].set(f_w1)
    max_decay = math.log(HYENA_DECAY_TARGET) / HYENA_FAST_DECAY
    min_decay = math.log(HYENA_DECAY_TARGET) / HYENA_SLOW_DECAY
    deltas = jnp.abs(jnp.linspace(min_decay, max_decay, width, dtype=F32)).reshape(1, width)
    n_tot = order * width
    tm = _row_tile(seq_len, tm)
    c2 = lambda i: (0, 0)
    return pl.pallas_call(
        functools.partial(_hyfilter_kernel, n_out=order, seq_len=seq_len),
        grid=(n // tm,),
        in_specs=[
            pl.BlockSpec((tm, LANES), lambda i: (i, 0)),
            pl.BlockSpec((LANES, hid), c2), pl.BlockSpec((1, hid), c2),
            pl.BlockSpec((hid, hid), c2), pl.BlockSpec((1, hid), c2),
            pl.BlockSpec((hid, hid), c2), pl.BlockSpec((1, hid), c2),
            pl.BlockSpec((3, hid), c2),
            pl.BlockSpec((None, hid, n_tot), lambda i: ((i * tm) // seq_len, 0, 0)),
            pl.BlockSpec((1, width), c2),
        ],
        out_specs=pl.BlockSpec((tm, n_tot), lambda i: (i, 0)),
        out_shape=jax.ShapeDtypeStruct((n, n_tot), F32),
        compiler_params=_cparams("parallel"),
        name="hyena_filters",
    )(zp, w1p, f_b1.reshape(1, hid), f_w2, f_b2.reshape(1, hid), f_w3, f_b3.reshape(1, hid),
      f_freq, wo.astype(BF16), deltas)


def _dft_tables(seq_len):
    n = 2 * seq_len
    p_ = FFT_P
    n1 = n // p_
    t1n = n1 // 2
    f1 = np.arange(n1)[:, None]
    t1 = np.arange(n1)[None, :]
    ang1 = 2.0 * np.pi * ((f1 * t1) % n1) / n1
    c1, s1 = np.cos(ang1), np.sin(ang1)
    w1 = np.zeros((2 * n1, 2 * t1n))
    w1[:n1, :t1n] = c1[:, :t1n]
    w1[:n1, t1n:] = s1[:, :t1n]
    w1[n1:, :t1n] = -s1[:, :t1n]
    w1[n1:, t1n:] = c1[:, :t1n]
    nh = n1 // 2 + 1
    nhp = -(-nh // SUBLANES) * SUBLANES
    w1k = np.zeros((2 * nhp, n1))
    w1k[:nh] = c1[:nh]
    w1k[nhp:nhp + nh] = -s1[:nh]
    w1i = np.zeros((2 * t1n, 2 * n1))
    ct, st = c1.T[:t1n] / n, s1.T[:t1n] / n
    w1i[:t1n, :n1] = ct
    w1i[:t1n, n1:] = -st
    w1i[t1n:, :n1] = st
    w1i[t1n:, n1:] = ct
    f2 = np.arange(p_)[:, None]
    pp = np.arange(p_)[None, :]
    ang2 = 2.0 * np.pi * ((f2 * pp) % p_) / p_
    c2, s2 = np.cos(ang2), np.sin(ang2)
    fb = np.block([[c2, s2], [-s2, c2]])
    fbi = np.block([[c2, -s2], [s2, c2]])
    fb = np.stack([fb, np.concatenate([fb[:p_][::-1], fb[p_:][::-1]], axis=0)])
    fbi = np.stack([fbi, np.concatenate([fbi[:, :p_][:, ::-1], fbi[:, p_:][:, ::-1]], axis=1)])
    angt = 2.0 * np.pi * ((np.arange(n1)[:, None] * np.arange(p_)[None, :]) % n) / n
    lane_bcast = lambda a: jnp.broadcast_to(jnp.asarray(a, dtype=F32)[:, :, None], (n1, p_, LANES))
    as_bf = lambda a: jnp.asarray(a, dtype=F32).astype(BF16)
    return dict(n1=n1, t1n=t1n, nh=nh, w1=as_bf(w1), w1k=as_bf(w1k), w1i=as_bf(w1i), fb=as_bf(fb), fbi=as_bf(fbi),
                twc=lane_bcast(np.cos(angt)), tws=lane_bcast(-np.sin(angt)))


DFT_PB = 16


def _rows_at(ref, pp):
    m, pb, _ = ref.shape
    return ref.reshape(m * pb, LANES)[pl.ds(pp, m, stride=pb), :]


def _set_rows_at(ref, pp, val):
    m, pb, _ = ref.shape
    ref.reshape(m * pb, LANES)[pl.ds(pp, m, stride=pb), :] = val


def _pack_complex(re, im):
    r = lax.bitcast_convert_type(re.astype(BF16).astype(F32), jnp.uint32)
    i = lax.bitcast_convert_type(im.astype(BF16).astype(F32), jnp.uint32)
    return r | (i >> 16)


def _unpack_complex(w):
    re = lax.bitcast_convert_type(w & jnp.uint32(0xFFFF0000), F32)
    im = lax.bitcast_convert_type(w << 16, F32)
    return re, im


def _dft1_kernel(w_ref, x_ref, o_ref):
    half = w_ref.shape[0] // 2
    for pp in range(x_ref.shape[1]):
        y = _bdot(w_ref[...], _rows_at(x_ref, pp).astype(BF16))
        _set_rows_at(o_ref, pp, _pack_complex(y[:half], y[half:]))


def dft_stage1(w, x4, idx):
    m, k = w.shape
    m = m // 2
    _, _, p_, c = x4.shape
    pb = min(DFT_PB, p_)
    return pl.pallas_call(
        _dft1_kernel,
        grid=(c // LANES, p_ // pb),
        in_specs=[pl.BlockSpec((2 * m, k), lambda j, q: (0, 0)),
                  pl.BlockSpec((None, k, pb, LANES), lambda j, q: (idx, 0, q, j))],
        out_specs=pl.BlockSpec((m, pb, LANES), lambda j, q: (0, q, j)),
        out_shape=jax.ShapeDtypeStruct((m, p_, c), jnp.uint32),
        compiler_params=_cparams("parallel", "parallel"),
        name="dft_stage1",
    )(w, x4)


def _dft3_kernel(w_ref, b_ref, g_ref, v_ref, bias_ref, o_ref):
    for pp in range(b_ref.shape[1]):
        re, im = _unpack_complex(_rows_at(b_ref, pp))
        y = _bdot(w_ref[...], jnp.concatenate([re, im], axis=0).astype(BF16))
        _set_rows_at(o_ref, pp, _rows_at(g_ref, pp) * (y + bias_ref[...] * _rows_at(v_ref, pp)))


def dft_inverse_stage1_gate(w, b3, gate4, gate_idx, v4, v_idx, bias):
    m, k = w.shape
    k = k // 2
    _, p_, c = b3.shape
    pb = min(DFT_PB, p_)
    lead = lambda idx: (lambda j, q: (idx, 0, q, j))
    return pl.pallas_call(
        _dft3_kernel,
        grid=(c // LANES, p_ // pb),
        in_specs=[pl.BlockSpec((m, 2 * k), lambda j, q: (0, 0)),
                  pl.BlockSpec((k, pb, LANES), lambda j, q: (0, q, j)),
                  pl.BlockSpec((None, m, pb, LANES), lead(gate_idx)),
                  pl.BlockSpec((None, m, pb, LANES), lead(v_idx)),
                  pl.BlockSpec((1, LANES), lambda j, q: (0, j))],
        out_specs=pl.BlockSpec((m, pb, LANES), lambda j, q: (0, q, j)),
        out_shape=jax.ShapeDtypeStruct((m, p_, c), F32),
        compiler_params=_cparams("parallel", "parallel"),
        name="dft_inverse_stage1_gate",
    )(w, b3, gate4, v4, bias.reshape(1, c).astype(F32))


def _twiddle(re, im, tc, ts, reps):
    tc = jnp.tile(tc, (1, reps))
    ts = jnp.tile(ts, (1, reps))
    return re * tc - im * ts, re * ts + im * tc


def _spec_fwd_kernel(a_ref, tc_ref, ts_ref, fb_ref, o_ref):
    reps = a_ref.shape[-1] // LANES
    re, im = _unpack_complex(a_ref[0])
    tr, ti = _twiddle(re, im, tc_ref[0], ts_ref[0], reps)
    rhs = jnp.concatenate([tr, ti], axis=0).astype(BF16)
    o_ref[0] = _bdot(fb_ref[...], rhs).astype(o_ref.dtype)


def _spec_conv_kernel(a_ref, tc_ref, ts_ref, fb_ref, fbi_ref, k_ref, o_ref, *, n1):
    p_ = FFT_P
    reps = a_ref.shape[-1] // LANES
    conj = jnp.where(pl.program_id(0) > n1 // 2, -1.0, 1.0).astype(F32)
    tc, ts = tc_ref[0], ts_ref[0]
    re, im = _unpack_complex(a_ref[0])
    tr, ti = _twiddle(re, im, tc, ts, reps)
    x = _bdot(fb_ref[...], jnp.concatenate([tr, ti], axis=0).astype(BF16))
    xr, xi = x[:p_], x[p_:]
    kr, ki = k_ref[0, :p_].astype(F32), conj * k_ref[0, p_:].astype(F32)
    yr = xr * kr - xi * ki
    yi = xr * ki + xi * kr
    bv = _bdot(fbi_ref[...], jnp.concatenate([yr, yi], axis=0).astype(BF16))
    orr, oi = _twiddle(bv[:p_], bv[p_:], tc, -ts, reps)
    o_ref[0] = _pack_complex(orr, oi)


def spectrum_forward(a3, tab, ct=1024):
    _, p_, c = a3.shape
    nh = tab["nh"]
    ct = min(ct, c)
    return pl.pallas_call(
        _spec_fwd_kernel,
        grid=(c // ct, nh),
        in_specs=[
            pl.BlockSpec((1, p_, ct), lambda j, f: (f, 0, j)),
            pl.BlockSpec((1, p_, LANES), lambda j, f: (f, 0, 0)),
            pl.BlockSpec((1, p_, LANES), lambda j, f: (f, 0, 0)),
            pl.BlockSpec((None, 2 * p_, 2 * p_), lambda j, f: (0, 0, 0)),
        ],
        out_specs=pl.BlockSpec((1, 2 * p_, ct), lambda j, f: (f, 0, j)),
        out_shape=jax.ShapeDtypeStruct((nh, 2 * p_, c), BF16),
        compiler_params=_cparams("parallel", "parallel"),
        name="dft_filter_stage2",
    )(a3, tab["twc"], tab["tws"], tab["fb"])


def spectrum_conv(a3, kspec, k_col_block, tab):
    n1, p_, c = a3.shape
    half = n1 // 2
    mirrored = lambda f: (f > half).astype(jnp.int32)
    return pl.pallas_call(
        functools.partial(_spec_conv_kernel, n1=n1),
        grid=(n1,),
        in_specs=[
            pl.BlockSpec((1, p_, c), lambda f: (f, 0, 0)),
            pl.BlockSpec((1, p_, LANES), lambda f: (f, 0, 0)),
            pl.BlockSpec((1, p_, LANES), lambda f: (f, 0, 0)),
            pl.BlockSpec((None, 2 * p_, 2 * p_), lambda f: (mirrored(f), 0, 0)),
            pl.BlockSpec((None, 2 * p_, 2 * p_), lambda f: (mirrored(f), 0, 0)),
            pl.BlockSpec((1, 2 * p_, c), lambda f: (jnp.where(f > half, n1 - f, f), 0, k_col_block)),
        ],
        out_specs=pl.BlockSpec((1, p_, c), lambda f: (f, 0, 0)),
        out_shape=jax.ShapeDtypeStruct(a3.shape, jnp.uint32),
        compiler_params=_cparams("arbitrary"),
        name="dft_stage2_conv",
    )(a3, tab["twc"], tab["tws"], tab["fb"], tab["fbi"], kspec)


def hyena_long_convs(xv, filt, bias, batch, seq_len):
    assert batch == 2, "the two sequences of the batch are packed as one complex sequence"
    c = xv.shape[-1]
    tab = _dft_tables(seq_len)
    n1, t1n, p_ = tab["n1"], tab["t1n"], FFT_P
    order = bias.shape[0]
    ak = dft_stage1(tab["w1k"], filt.reshape(1, n1, p_, order * c), 0)
    kspec = spectrum_forward(ak, tab)
    xv4 = xv.reshape(3, batch * t1n, p_, c)
    y4, y_idx = xv4, 2
    for o in range(order):
        a = dft_stage1(tab["w1"], y4, y_idx)
        b3 = spectrum_conv(a, kspec, o, tab)
        y = dft_inverse_stage1_gate(tab["w1i"], b3, xv4, o, y4, y_idx, bias[o])
        y4, y_idx = y[None], 0
    return y.reshape(batch * seq_len, c)


def kernel(x, c, ctx, c_ctx, ada_w, ada_b, norm_g, gmlp_w_in, gmlp_g_v, gmlp_w_s, gmlp_b_s, gmlp_w_out, lru_w_in, lru_conv_w, lru_conv_b, lru_w_a, lru_b_a, lru_w_x, lru_b_x, lru_lam, lru_w_out, hyena_w_in, hyena_conv_w, hyena_conv_b, hyena_f_w1, hyena_f_b1, hyena_f_w2, hyena_f_b2, hyena_f_w3, hyena_f_b3, hyena_f_freq, hyena_f_wout, hyena_bias, hyena_w_out, pool_w_in, pool_w_g, pool_b_g, pool_scale, pool_w_out, ffn_w_gate, ffn_w_up, ffn_w_down, moe_w_router, moe_w_gate, moe_w_up, moe_w_down):
    B, L, D = x.shape
    Lc = ctx.shape[1]
    depth = ada_w.shape[0]
    n_mixers = 4
    assert B + 1 <= SUBLANES

    cc = jnp.zeros((SUBLANES, D), F32).at[:B].set(c).at[B].set(c_ctx)
    ada = ada_all(cc, ada_w, ada_b).reshape(depth, SUBLANES, 6, D)

    xs = add_pos(x)
    xc = ctx.reshape(B * Lc, D)
    last_ctx = max([i for i in range(depth) if i % n_mixers == 1], default=-1)

    for i in range(depth):
        kind, j = i % n_mixers, i // n_mixers
        update_ctx = i < last_ctx
        read_ctx = i <= last_ctx
        lat = [ada[i, :B, q].reshape(B, 1, D) for q in range(6)]
        cx = [jnp.broadcast_to(ada[i, B:B + 1, q].reshape(1, 1, D), (B, 1, D)) for q in range(6)]
        g = norm_g[i]
        streams = [(xs, lat, L)]
        if read_ctx:
            streams.append((xc, cx, Lc))

        if kind == 0:
            outs = []
            for (s, m, sl) in streams[:1 + int(update_ctx)]:
                hw = norm_mod_matmul(s, g[0], m[0], m[1], gmlp_w_in[j], sl, BF16, act="gelu")
                a = gmlp_spatial(hw, gmlp_g_v[j], gmlp_w_s[j], gmlp_b_s[j])
                outs.append(matmul_norm_res([(a, a.shape[1], 0)], _identity_bf16, gmlp_w_out[j],
                                            g[1], m[2], s, sl))
        elif kind == 1:
            W = lru_w_in.shape[2] // 2
            sc_args = lambda d: (lru_conv_w[j], lru_conv_b[j], lru_w_a[j, d], lru_b_a[j, d],
                                 lru_w_x[j, d], lru_b_x[j, d], lru_lam[j, d])
            zc = norm_mod_matmul(xc, g[0], cx[0], cx[1], lru_w_in[j], Lc, F32)
            zero = jnp.zeros((B, W), F32)
            hf_c = lru_scan(zc, 1, W, *sc_args(0), zero, Lc, False)
            hb_c = lru_scan(zc, 1, W, *sc_args(1), zero, Lc, True)
            zl = norm_mod_matmul(xs, g[0], lat[0], lat[1], lru_w_in[j], L, F32)
            hf = lru_scan(zl, 1, W, *sc_args(0), hf_c.reshape(B, Lc, W)[:, -1], L, False)
            hb = lru_scan(zl, 1, W, *sc_args(1), hb_c.reshape(B, Lc, W)[:, 0], L, True)
            outs = [matmul_norm_res([(zl, W, 0), (hf, W, 0), (hb, W, 0)], _lru_out_prologue,
                                    lru_w_out[j], g[1], lat[2], xs, L)]
            if update_ctx:
                outs.append(matmul_norm_res([(zc, W, 0), (hf_c, W, 0), (hb_c, W, 0)], _lru_out_prologue,
                                            lru_w_out[j], g[1], cx[2], xc, Lc))
        elif kind == 2:
            W = hyena_w_out.shape[1]
            filt = hyena_filters(L, W, hyena_f_w1[j], hyena_f_b1[j], hyena_f_w2[j], hyena_f_b2[j],
                                 hyena_f_w3[j], hyena_f_b3[j], hyena_f_freq[j], hyena_f_wout[j])
            outs = []
            for (s, m, sl) in streams[:1 + int(update_ctx)]:
                nb = s.shape[0] // sl
                xv = norm_mod_matmul_conv(s, g[0], m[0], m[1], hyena_w_in[j], hyena_conv_w[j],
                                          hyena_conv_b[j], W, sl)
                fl = filt if sl == L else hyena_filters(
                    sl, W, hyena_f_w1[j], hyena_f_b1[j], hyena_f_w2[j], hyena_f_b2[j],
                    hyena_f_w3[j], hyena_f_b3[j], hyena_f_freq[j], hyena_f_wout[j])
                y = hyena_long_convs(xv, fl, hyena_bias[j], nb, sl)
                outs.append(matmul_norm_res([(y, W, 0)], _identity_bf16, hyena_w_out[j], g[1], m[2], s, sl))
        else:
            outs = []
            for (s, m, sl) in streams[:1 + int(update_ctx)]:
                p = norm_mod_matmul(s, g[0], m[0], m[1], pool_w_in[j], sl, F32)
                a = pool_mix(p, pool_w_g[j], pool_b_g[j], pool_scale[j], sl)
                outs.append(matmul_norm_res([(a, a.shape[1], 0)], _identity_bf16, pool_w_out[j],
                                            g[1], m[2], s, sl))
        xs = outs[0]
        if update_ctx:
            xc = outs[1]

        k = i // 2
        todo = [(xs, lat, L)] + ([(xc, cx, Lc)] if update_ctx else [])
        res = []
        for (s, m, sl) in todo:
            if i % 2 == 0:
                res.append(ffn_dense(s, g[2], m[3], m[4], ffn_w_gate[k], ffn_w_up[k], ffn_w_down[k],
                                     g[3], m[5], sl))
            else:
                res.append(moe_block(s, g[2], m[3], m[4], moe_w_router[k], moe_w_gate, moe_w_up,
                                     moe_w_down, k, g[3], m[5], sl))
        xs = res[0]
        if update_ctx:
            xc = res[1]
    return xs.reshape(B, L, D)
```

```python
import functools
import math

import jax
import jax.numpy as jnp
import numpy as np
from jax import lax
from jax.experimental import pallas as pl
from jax.experimental.pallas import tpu as pltpu

F32 = jnp.float32
BF16 = jnp.bfloat16
EPS = 1e-6

VMEM_LIMIT_BYTES = 52 * 1024 * 1024
LANES = 128
SUBLANES = 8

GRID_W = 64
GMLP_CHUNK = 128
GMLP_GROUPS = 8
LRU_HEADS = 8
LRU_C = 8.0
POOL_WINDOWS = (2, 4, 8, 16)
HYENA_BANDS = 16
HYENA_FAST_DECAY = 0.3
HYENA_SLOW_DECAY = 1.5
HYENA_DECAY_TARGET = 1e-2
N_EXPERTS = 8
TOP_K = 2
FFT_P = 128
HALO = SUBLANES


def _cparams(*sem):
    return pltpu.CompilerParams(dimension_semantics=sem, vmem_limit_bytes=VMEM_LIMIT_BYTES)


def _rms(x, g):
    return x * lax.rsqrt(jnp.mean(x * x, axis=-1, keepdims=True) + EPS) * g


def _gelu(x):
    return 0.5 * x * (1.0 + jnp.tanh(math.sqrt(2.0 / math.pi) * (x + 0.044715 * (x * x * x))))


def _silu(x):
    return x * (1.0 / (1.0 + jnp.exp(-x)))


def _sigmoid(x):
    return 1.0 / (1.0 + jnp.exp(-x))


def _bdot(a, b):
    return jnp.dot(a, b, preferred_element_type=F32)


def _hdot(a, b):
    return jnp.dot(a, b, preferred_element_type=F32, precision=lax.Precision.HIGHEST)


def _to_row_tiles(val, ref):
    rows = val.shape[0]
    nq = val.shape[1] // LANES
    for q in range(nq):
        ref[pl.ds(q, rows, stride=nq), :] = val[:, q * LANES:(q + 1) * LANES].astype(ref.dtype)


def _from_row_tiles(ref, dst_ref):
    rows = dst_ref.shape[0]
    nq = dst_ref.shape[1] // LANES
    for q in range(nq):
        dst_ref[:, q * LANES:(q + 1) * LANES] = ref[pl.ds(q, rows, stride=nq), :].astype(dst_ref.dtype)


def _row_tile(rows, want):
    t = min(rows, want)
    assert rows % t == 0, (rows, t)
    return t


def _ada_kernel(c_ref, w_ref, b_ref, o_ref):
    o_ref[0] = _hdot(_silu(c_ref[...]), w_ref[0]) + b_ref[0]


def ada_all(cc, ada_w, ada_b):
    depth, d, d6 = ada_w.shape
    nchunk = d6 // d
    return pl.pallas_call(
        _ada_kernel,
        grid=(depth, nchunk),
        in_specs=[
            pl.BlockSpec((SUBLANES, d), lambda i, j: (0, 0)),
            pl.BlockSpec((1, d, d), lambda i, j: (i, 0, j)),
            pl.BlockSpec((1, 1, d), lambda i, j: (i, 0, j)),
        ],
        out_specs=pl.BlockSpec((1, SUBLANES, d), lambda i, j: (i, 0, j)),
        out_shape=jax.ShapeDtypeStruct((depth, SUBLANES, d6), F32),
        compiler_params=_cparams("parallel", "parallel"),
        name="ada",
    )(cc, ada_w, ada_b.reshape(depth, 1, d6))


def _pos_kernel(x_ref, rt_ref, ct_ref, o_ref):
    half = rt_ref.shape[-1]
    x = x_ref[...]
    o_ref[:, :, :half] = x[:, :, :half] + rt_ref[...]
    o_ref[:, :, half:] = x[:, :, half:] + ct_ref[...][None]


def add_pos(x):
    b, l, d = x.shape
    rows = l // GRID_W
    quarter = d // 4
    omega = 1.0 / (10000.0 ** (jnp.arange(quarter, dtype=F32) / quarter))

    def sincos(p):
        ang = p.reshape(-1, 1) * omega[None, :]
        return jnp.concatenate([jnp.sin(ang), jnp.cos(ang)], axis=-1)

    rtab = sincos(jnp.arange(rows, dtype=F32)).reshape(rows, 1, 2 * quarter)
    ctab = sincos(jnp.arange(GRID_W, dtype=F32))
    x3 = x.reshape(b * rows, GRID_W, d)
    tr = _row_tile(rows, 16)
    nrt = rows // tr
    out = pl.pallas_call(
        _pos_kernel,
        grid=(b * nrt,),
        in_specs=[
            pl.BlockSpec((tr, GRID_W, d), lambda i: (i, 0, 0)),
            pl.BlockSpec((tr, 1, 2 * quarter), lambda i: (i % nrt, 0, 0)),
            pl.BlockSpec((GRID_W, 2 * quarter), lambda i: (0, 0)),
        ],
        out_specs=pl.BlockSpec((tr, GRID_W, d), lambda i: (i, 0, 0)),
        out_shape=jax.ShapeDtypeStruct(x3.shape, F32),
        compiler_params=_cparams("parallel"),
        name="add_pos",
    )(x3, rtab, ctab)
    return out.reshape(b * l, d)


def _nmm_kernel(x_ref, g_ref, sh_ref, sc_ref, w_ref, o_ref, h_scr, *, act):
    @pl.when(pl.program_id(1) == 0)
    def _():
        h = _rms(x_ref[...], g_ref[...]) * (1.0 + sc_ref[0]) + sh_ref[0]
        h_scr[...] = h.astype(BF16)

    y = _bdot(h_scr[...], w_ref[...])
    if act == "gelu":
        y = _gelu(y)
    o_ref[...] = y.astype(o_ref.dtype)


def norm_mod_matmul(x, g, shift, scale, w, rows_per_mod, out_dtype, act=None, tm=512, tn=2048):
    r, d = x.shape
    n = w.shape[1]
    tm = _row_tile(min(r, rows_per_mod), tm)
    tn = min(tn, n)
    assert n % tn == 0
    mod_map = lambda i, j: ((i * tm) // rows_per_mod, 0, 0)
    return pl.pallas_call(
        functools.partial(_nmm_kernel, act=act),
        grid=(r // tm, n // tn),
        in_specs=[
            pl.BlockSpec((tm, d), lambda i, j: (i, 0)),
            pl.BlockSpec((1, d), lambda i, j: (0, 0)),
            pl.BlockSpec((1, 1, d), mod_map),
            pl.BlockSpec((1, 1, d), mod_map),
            pl.BlockSpec((d, tn), lambda i, j: (0, j)),
        ],
        out_specs=pl.BlockSpec((tm, tn), lambda i, j: (i, j)),
        out_shape=jax.ShapeDtypeStruct((r, n), out_dtype),
        scratch_shapes=[pltpu.VMEM((tm, d), BF16)],
        compiler_params=_cparams("parallel", "arbitrary"),
        name="norm_mod_matmul",
    )(x, g.reshape(1, d), shift, scale, w.astype(BF16))


HALO_BF16 = 2 * SUBLANES


def _nmm_conv_kernel(x_ref, xp_ref, xn_ref, g_ref, sh_ref, sc_ref, w_ref, cw_ref, cb_ref, o_ref,
                     h_scr, z_scr, *, tm, seq_len):
    i = pl.program_id(0)
    hl = HALO_BF16

    @pl.when(pl.program_id(1) == 0)
    def _():
        def nm(x):
            return (_rms(x, g_ref[...]) * (1.0 + sc_ref[0]) + sh_ref[0]).astype(BF16)
        h_scr[0:hl, :] = nm(xp_ref[...])
        h_scr[hl:hl + tm, :] = nm(x_ref[...])
        h_scr[hl + tm:hl + tm + hl, :] = nm(xn_ref[...])

    z = _bdot(h_scr[...], w_ref[...])
    first = (i * tm) % seq_len == 0
    last = ((i + 1) * tm) % seq_len == 0
    row = lax.broadcasted_iota(jnp.int32, (tm + 2 * hl, 1), 0)
    outside = jnp.logical_or(jnp.logical_and(first, row < hl), jnp.logical_and(last, row >= hl + tm))
    z_scr[...] = jnp.where(outside, 0.0, z)
    kk = cw_ref.shape[0]
    left = kk // 2
    y = cb_ref[...] + cw_ref[0:1, :] * z_scr[pl.ds(hl - left, tm), :]
    for k in range(1, kk):
        y = y + cw_ref[k:k + 1, :] * z_scr[pl.ds(hl - left + k, tm), :]
    o_ref[0] = y


def norm_mod_matmul_conv(x, g, shift, scale, w, conv_w, conv_b, width, seq_len, tm=512):
    r, d = x.shape
    n = w.shape[1]
    nsplit = n // width
    tm = _row_tile(seq_len, tm)
    hl = HALO_BF16
    per = tm // hl
    n_row_blocks = r // hl
    kk = conv_w.shape[0]
    mod_map = lambda i, j: ((i * tm) // seq_len, 0, 0)
    return pl.pallas_call(
        functools.partial(_nmm_conv_kernel, tm=tm, seq_len=seq_len),
        grid=(r // tm, nsplit),
        in_specs=[
            pl.BlockSpec((tm, d), lambda i, j: (i, 0)),
            pl.BlockSpec((hl, d), lambda i, j: (jnp.maximum(i * per - 1, 0), 0)),
            pl.BlockSpec((hl, d), lambda i, j: (jnp.minimum((i + 1) * per, n_row_blocks - 1), 0)),
            pl.BlockSpec((1, d), lambda i, j: (0, 0)),
            pl.BlockSpec((1, 1, d), mod_map),
            pl.BlockSpec((1, 1, d), mod_map),
            pl.BlockSpec((d, width), lambda i, j: (0, j)),
            pl.BlockSpec((kk, width), lambda i, j: (0, j)),
            pl.BlockSpec((1, width), lambda i, j: (0, j)),
        ],
        out_specs=pl.BlockSpec((1, tm, width), lambda i, j: (j, i, 0)),
        out_shape=jax.ShapeDtypeStruct((nsplit, r, width), F32),
        scratch_shapes=[pltpu.VMEM((tm + 2 * hl, d), BF16), pltpu.VMEM((tm + 2 * hl, width), F32)],
        compiler_params=_cparams("parallel", "arbitrary"),
        name="norm_mod_matmul_conv",
    )(x, x, x, g.reshape(1, d), shift, scale, w.astype(BF16), conv_w, conv_b.reshape(1, n))


def _mnr_kernel(*refs, n_in, prologue):
    in_refs = refs[:n_in]
    w_ref, g_ref, gate_ref, x_ref, o_ref = refs[n_in:]
    a = prologue(*[r[...] for r in in_refs])
    y = _bdot(a, w_ref[...])
    o_ref[...] = x_ref[...] + gate_ref[0] * _rms(y, g_ref[...])


def matmul_norm_res(inputs, prologue, w, g, gate, x, rows_per_mod, tm=512):
    r, d = x.shape
    k = w.shape[0]
    tm = _row_tile(min(r, rows_per_mod), tm)
    mod_map = lambda i: ((i * tm) // rows_per_mod, 0, 0)
    in_specs = [pl.BlockSpec((tm, wd), functools.partial(lambda i, cb: (i, cb), cb=cb))
                for (_, wd, cb) in inputs]
    in_specs += [
        pl.BlockSpec((k, d), lambda i: (0, 0)),
        pl.BlockSpec((1, d), lambda i: (0, 0)),
        pl.BlockSpec((1, 1, d), mod_map),
        pl.BlockSpec((tm, d), lambda i: (i, 0)),
    ]
    return pl.pallas_call(
        functools.partial(_mnr_kernel, n_in=len(inputs), prologue=prologue),
        grid=(r // tm,),
        in_specs=in_specs,
        out_specs=pl.BlockSpec((tm, d), lambda i: (i, 0)),
        out_shape=jax.ShapeDtypeStruct((r, d), F32),
        compiler_params=_cparams("parallel"),
        name="matmul_norm_res",
    )(*[a for (a, _, _) in inputs], w.astype(BF16), g.reshape(1, d), gate, x)


def _identity_bf16(a):
    return a.astype(BF16)


def _swiglu_step(h_scr, wg_ref, wu_ref, wd_ref, acc_scr, j):
    h = h_scr[...]
    t = (_silu(_bdot(h, wg_ref[...].astype(BF16))) * _bdot(h, wu_ref[...].astype(BF16))).astype(BF16)
    part = _bdot(t, wd_ref[...].astype(BF16))

    @pl.when(j == 0)
    def _():
        acc_scr[...] = part

    @pl.when(j > 0)
    def _():
        acc_scr[...] += part


def _ffn_dense_kernel(x_ref, g1_ref, sh_ref, sc_ref, wg_ref, wu_ref, wd_ref, g2_ref, gate_ref,
                      o_ref, h_scr, acc_scr):
    j = pl.program_id(1)

    @pl.when(j == 0)
    def _():
        h = _rms(x_ref[...], g1_ref[...]) * (1.0 + sc_ref[0]) + sh_ref[0]
        h_scr[...] = h.astype(BF16)

    _swiglu_step(h_scr, wg_ref, wu_ref, wd_ref, acc_scr, j)

    @pl.when(j == pl.num_programs(1) - 1)
    def _():
        o_ref[...] = x_ref[...] + gate_ref[0] * _rms(acc_scr[...], g2_ref[...])


def ffn_dense(x, g1, shift, scale, wg, wu, wd, g2, gate, rows_per_mod, tm=512, tf=1408):
    r, d = x.shape
    ff = wg.shape[1]
    tm = _row_tile(min(r, rows_per_mod), tm)
    assert ff % tf == 0
    mod_map = lambda i, j: ((i * tm) // rows_per_mod, 0, 0)
    return pl.pallas_call(
        _ffn_dense_kernel,
        grid=(r // tm, ff // tf),
        in_specs=[
            pl.BlockSpec((tm, d), lambda i, j: (i, 0)),
            pl.BlockSpec((1, d), lambda i, j: (0, 0)),
            pl.BlockSpec((1, 1, d), mod_map),
            pl.BlockSpec((1, 1, d), mod_map),
            pl.BlockSpec((d, tf), lambda i, j: (0, j)),
            pl.BlockSpec((d, tf), lambda i, j: (0, j)),
            pl.BlockSpec((tf, d), lambda i, j: (j, 0)),
            pl.BlockSpec((1, d), lambda i, j: (0, 0)),
            pl.BlockSpec((1, 1, d), mod_map),
        ],
        out_specs=pl.BlockSpec((tm, d), lambda i, j: (i, 0)),
        out_shape=jax.ShapeDtypeStruct((r, d), F32),
        scratch_shapes=[pltpu.VMEM((tm, d), BF16), pltpu.VMEM((tm, d), F32)],
        compiler_params=_cparams("parallel", "arbitrary"),
        name="ffn_dense",
    )(x, g1.reshape(1, d), shift, scale, wg.astype(BF16), wu.astype(BF16), wd.astype(BF16),
      g2.reshape(1, d), gate)


def _ffn_routed_kernel(te_ref, nu_ref, idx0_ref, idxn_ref, h_hbm, wg_ref, wu_ref, wd_ref, o_ref,
                       xbuf, sems, h_scr, acc_scr, *, tm, nq, issue_steps):
    i = pl.program_id(0)
    j = pl.program_id(1)
    last = pl.num_programs(1) - 1
    used = i < nu_ref[0]
    slot = i % 2
    per = tm // issue_steps

    def row_copy(idx_ref, r, s):
        s0 = pl.multiple_of(idx_ref[r] * nq, nq)
        d0 = pl.multiple_of(r * nq, nq)
        return pltpu.make_async_copy(h_hbm.at[pl.ds(s0, nq)], xbuf.at[s, pl.ds(d0, nq)], sems.at[s])

    @pl.when(used)
    def _():
        @pl.when(jnp.logical_and(i == 0, j == 0))
        def _():
            def issue(r, carry):
                row_copy(idx0_ref, r, 0).start()
                return carry
            lax.fori_loop(0, tm, issue, 0)

        @pl.when(j == 0)
        def _():
            pltpu.make_async_copy(h_hbm.at[pl.ds(0, tm * nq)], xbuf.at[slot], sems.at[slot]).wait()
            _from_row_tiles(xbuf.at[slot], h_scr)

        @pl.when(jnp.logical_and(i + 1 < nu_ref[0], j < issue_steps))
        def _():
            for k in range(per):
                row_copy(idxn_ref, j * per + k, 1 - slot).start(priority=k % 2)

        _swiglu_step(h_scr, wg_ref.at[0], wu_ref.at[0], wd_ref.at[0], acc_scr, j)

        @pl.when(j == last)
        def _():
            _to_row_tiles(acc_scr[...], o_ref)

    @pl.when(jnp.logical_and(jnp.logical_not(used), j == last))
    def _():
        o_ref[...] = jnp.zeros_like(o_ref)


def ffn_routed(h_rt, src, tile_expert, n_used, wg, wu, wd, tm, tf=512):
    dq = wg.shape[1] // LANES
    d = dq * LANES
    p = src.shape[0]
    ff = wg.shape[2]
    assert ff % tf == 0 and p % tm == 0
    n_tiles = p // tm
    issue_steps = min(4, ff // tf)
    assert tm % issue_steps == 0
    grid_spec = pltpu.PrefetchScalarGridSpec(
        num_scalar_prefetch=2,
        grid=(n_tiles, ff // tf),
        in_specs=[
            pl.BlockSpec((tm,), lambda i, j, te, nu: (0,), memory_space=pltpu.SMEM),
            pl.BlockSpec((tm,), lambda i, j, te, nu: (jnp.minimum(i + 1, n_tiles - 1),),
                         memory_space=pltpu.SMEM),
            pl.BlockSpec(memory_space=pl.ANY),
            pl.BlockSpec((1, d, tf), lambda i, j, te, nu: (te[i], 0, j)),
            pl.BlockSpec((1, d, tf), lambda i, j, te, nu: (te[i], 0, j)),
            pl.BlockSpec((1, tf, d), lambda i, j, te, nu: (te[i], j, 0)),
        ],
        out_specs=pl.BlockSpec((tm * dq, LANES), lambda i, j, te, nu: (i, 0)),
        scratch_shapes=[pltpu.VMEM((2, tm * dq, LANES), F32), pltpu.SemaphoreType.DMA((2,)),
                        pltpu.VMEM((tm, d), BF16), pltpu.VMEM((tm, d), F32)],
    )
    return pl.pallas_call(
        functools.partial(_ffn_routed_kernel, tm=tm, nq=dq, issue_steps=issue_steps),
        grid_spec=grid_spec,
        out_shape=jax.ShapeDtypeStruct((p * dq, LANES), F32),
        compiler_params=_cparams("arbitrary", "arbitrary"),
        name="ffn_routed",
    )(tile_expert, n_used, src, src, h_rt, wg, wu, wd)


ROUTE_IDX0 = N_EXPERTS
ROUTE_P0 = N_EXPERTS + TOP_K


def _router_kernel(x_ref, g_ref, sh_ref, sc_ref, wr_ref, h_ref, r_ref):
    h = _rms(x_ref[...], g_ref[...]) * (1.0 + sc_ref[0]) + sh_ref[0]
    _to_row_tiles(h, h_ref)
    logits = _hdot(h, wr_ref[...])
    lane = lax.broadcasted_iota(jnp.int32, logits.shape, 1)
    neg = jnp.float32(-jnp.inf)
    big = jnp.int32(LANES)
    lg = jnp.where(lane < N_EXPERTS, logits, neg)
    m1 = jnp.max(lg, axis=-1, keepdims=True)
    i1 = jnp.min(jnp.where(lg == m1, lane, big), axis=-1, keepdims=True)
    lg2 = jnp.where(lane == i1, neg, lg)
    m2 = jnp.max(lg2, axis=-1, keepdims=True)
    i2 = jnp.min(jnp.where(lg2 == m2, lane, big), axis=-1, keepdims=True)
    e2 = jnp.exp(m2 - m1)
    p1 = 1.0 / (1.0 + e2)
    p2 = e2 / (1.0 + e2)
    out = jnp.where(lane == ROUTE_IDX0, i1.astype(F32), 0.0)
    out = jnp.where(lane == ROUTE_IDX0 + 1, i2.astype(F32), out)
    out = jnp.where(lane == ROUTE_P0, p1, out)
    out = jnp.where(lane == ROUTE_P0 + 1, p2, out)
    r_ref[...] = out


def router(x, g, shift, scale, w_router, rows_per_mod, tm=512):
    r, d = x.shape
    tm = _row_tile(min(r, rows_per_mod), tm)
    wr = jnp.zeros((d, LANES), F32).at[:, :N_EXPERTS].set(w_router)
    mod_map = lambda i: ((i * tm) // rows_per_mod, 0, 0)
    return pl.pallas_call(
        _router_kernel,
        grid=(r // tm,),
        in_specs=[
            pl.BlockSpec((tm, d), lambda i: (i, 0)),
            pl.BlockSpec((1, d), lambda i: (0, 0)),
            pl.BlockSpec((1, 1, d), mod_map),
            pl.BlockSpec((1, 1, d), mod_map),
            pl.BlockSpec((d, LANES), lambda i: (0, 0)),
        ],
        out_specs=[pl.BlockSpec((tm * (d // LANES), LANES), lambda i: (i, 0)),
                   pl.BlockSpec((tm, LANES), lambda i: (i, 0))],
        out_shape=[jax.ShapeDtypeStruct((r * (d // LANES), LANES), F32), jax.ShapeDtypeStruct((r, LANES), F32)],
        compiler_params=_cparams("parallel"),
        name="router",
    )(x, g.reshape(1, d), shift, scale, wr)


GATHER_UNROLL = 8


def _combine_kernel(f0_ref, f1_ref, n0_ref, n1_ref, ys_hbm, r_ref, g_ref, gate_ref, x_ref, o_ref,
                    ybuf, sems, y_scr, *, tm, nq):
    i = pl.program_id(0)
    nt = pl.num_programs(0)
    slot = i % 2

    def issue_tile(idx_refs, s):
        def issue(rb, carry):
            for k in range(GATHER_UNROLL):
                r = GATHER_UNROLL * rb + k
                d0 = pl.multiple_of(r * nq, nq)
                for c, idx_ref in enumerate(idx_refs):
                    s0 = pl.multiple_of(idx_ref[r] * nq, nq)
                    pltpu.make_async_copy(ys_hbm.at[pl.ds(s0, nq)], ybuf.at[s, c, pl.ds(d0, nq)],
                                          sems.at[s]).start(priority=c)
            return carry
        lax.fori_loop(0, tm // GATHER_UNROLL, issue, 0)

    @pl.when(i == 0)
    def _():
        issue_tile((f0_ref, f1_ref), 0)

    @pl.when(i + 1 < nt)
    def _():
        issue_tile((n0_ref, n1_ref), 1 - slot)

    for c in range(TOP_K):
        pltpu.make_async_copy(ys_hbm.at[pl.ds(0, tm * nq)], ybuf.at[slot, c], sems.at[slot]).wait()
    rt = r_ref[...]
    p0 = rt[:, ROUTE_P0:ROUTE_P0 + 1]
    p1 = rt[:, ROUTE_P0 + 1:ROUTE_P0 + 2]
    _from_row_tiles(ybuf.at[slot, 0], y_scr)
    y = p0 * y_scr[...]
    _from_row_tiles(ybuf.at[slot, 1], y_scr)
    y = y + p1 * y_scr[...]
    o_ref[...] = x_ref[...] + gate_ref[0] * _rms(y, g_ref[...])


def combine(ys, slot0, slot1, route, g, gate, x, rows_per_mod, tm=512):
    r, d = x.shape
    nq = d // LANES
    tm = _row_tile(min(r, rows_per_mod), tm)
    nt = r // tm
    mod_map = lambda i: ((i * tm) // rows_per_mod, 0, 0)
    first = pl.BlockSpec((tm,), lambda i: (0,), memory_space=pltpu.SMEM)
    nxt = pl.BlockSpec((tm,), lambda i: (jnp.minimum(i + 1, nt - 1),), memory_space=pltpu.SMEM)
    return pl.pallas_call(
        functools.partial(_combine_kernel, tm=tm, nq=nq),
        grid=(nt,),
        in_specs=[
            first, first, nxt, nxt,
            pl.BlockSpec(memory_space=pl.ANY),
            pl.BlockSpec((tm, LANES), lambda i: (i, 0)),
            pl.BlockSpec((1, d), lambda i: (0, 0)),
            pl.BlockSpec((1, 1, d), mod_map),
            pl.BlockSpec((tm, d), lambda i: (i, 0)),
        ],
        out_specs=pl.BlockSpec((tm, d), lambda i: (i, 0)),
        out_shape=jax.ShapeDtypeStruct((r, d), F32),
        scratch_shapes=[pltpu.VMEM((2, 2, tm * nq, LANES), F32), pltpu.SemaphoreType.DMA((2,)),
                        pltpu.VMEM((tm, d), F32)],
        compiler_params=_cparams("arbitrary"),
        name="moe_combine",
    )(slot0, slot1, slot0, slot1, ys, route, g.reshape(1, d), gate, x)


def moe_block(x, g1, shift, scale, w_router, wg_all, wu_all, wd_all, layer, g2, gate, rows_per_mod, tm=1024):
    n, d = x.shape
    ff = wg_all.shape[-1]
    wg = wg_all.reshape(-1, d, ff)
    wu = wu_all.reshape(-1, d, ff)
    wd = wd_all.reshape(-1, ff, d)
    h, route = router(x, g1, shift, scale, w_router, rows_per_mod)
    eidx = route[:, ROUTE_IDX0:ROUTE_IDX0 + TOP_K].astype(jnp.int32)
    flat_e = eidx.T.reshape(-1)
    onehot = (flat_e[:, None] == jnp.arange(N_EXPERTS, dtype=jnp.int32)[None, :]).astype(jnp.int32)
    csum = jnp.cumsum(onehot, axis=0)
    counts = csum[-1]
    rank = jnp.take_along_axis(csum, flat_e[:, None], axis=1)[:, 0] - 1
    tiles_per_e = (counts + tm - 1) // tm
    tile_end = jnp.cumsum(tiles_per_e)
    tile_start = tile_end - tiles_per_e
    slot = tile_start[flat_e] * tm + rank
    n_tiles = (TOP_K * n) // tm + N_EXPERTS
    tok = jnp.tile(jnp.arange(n, dtype=jnp.int32), TOP_K)
    src = jnp.zeros((n_tiles * tm,), jnp.int32).at[slot].set(tok)
    tile_ids = jnp.arange(n_tiles, dtype=jnp.int32)
    tile_expert = jnp.minimum(jnp.sum((tile_ids[:, None] >= tile_end[None, :]).astype(jnp.int32), axis=1),
                              N_EXPERTS - 1).astype(jnp.int32)
    n_used = tile_end[-1:].astype(jnp.int32)

    ys = ffn_routed(h, src, tile_expert + layer * N_EXPERTS, n_used, wg, wu, wd, tm)
    slot = slot.astype(jnp.int32)
    return combine(ys, slot[:n], slot[n:], route, g2, gate, x, rows_per_mod)


def _gmlp_kernel(u_ref, v_ref, gv_ref, ws_ref, bs_ref, o_ref, vn_scr):
    v = v_ref[...].astype(F32)
    mu = jnp.mean(v, axis=-1, keepdims=True)
    vc = v - mu
    vn = vc * lax.rsqrt(jnp.mean(vc * vc, axis=-1, keepdims=True) + EPS) * gv_ref[...]
    vn_scr[...] = vn.astype(BF16)
    tm, width = vn_scr.shape
    gw = width // GMLP_GROUPS
    for n in range(tm // GMLP_CHUNK):
        rs = slice(n * GMLP_CHUNK, (n + 1) * GMLP_CHUNK)
        for g in range(GMLP_GROUPS):
            cs = slice(g * gw, (g + 1) * gw)
            m = _bdot(ws_ref[g], vn_scr[rs, cs]) + bs_ref[g]
            o_ref[rs, cs] = (u_ref[rs, cs].astype(F32) * m).astype(BF16)


def gmlp_spatial(hw, g_v, w_s, b_s, tm=512):
    r, w2 = hw.shape
    width = w2 // 2
    gw = width // GMLP_GROUPS
    tm = _row_tile(r, tm)
    bsb = jnp.broadcast_to(b_s[:, :, None], (GMLP_GROUPS, GMLP_CHUNK, gw)).astype(F32)
    return pl.pallas_call(
        _gmlp_kernel,
        grid=(r // tm,),
        in_specs=[
            pl.BlockSpec((tm, width), lambda i: (i, 0)),
            pl.BlockSpec((tm, width), lambda i: (i, 1)),
            pl.BlockSpec((1, width), lambda i: (0, 0)),
            pl.BlockSpec((GMLP_GROUPS, GMLP_CHUNK, GMLP_CHUNK), lambda i: (0, 0, 0)),
            pl.BlockSpec((GMLP_GROUPS, GMLP_CHUNK, gw), lambda i: (0, 0, 0)),
        ],
        out_specs=pl.BlockSpec((tm, width), lambda i: (i, 0)),
        out_shape=jax.ShapeDtypeStruct((r, width), BF16),
        scratch_shapes=[pltpu.VMEM((tm, width), BF16)],
        compiler_params=_cparams("parallel"),
        name="gmlp_spatial",
    )(hw, hw, g_v.reshape(1, width), w_s.astype(BF16), bsb)


def _halo_specs(tm, width, col_block, tile_of, n_row_blocks):
    per = tm // HALO
    cur = pl.BlockSpec((tm, width), lambda i: (tile_of(i), col_block))
    prev = pl.BlockSpec((HALO, width), lambda i: (jnp.maximum(tile_of(i) * per - 1, 0), col_block))
    nxt = pl.BlockSpec((HALO, width),
                       lambda i: (jnp.minimum((tile_of(i) + 1) * per, n_row_blocks - 1), col_block))
    return [cur, prev, nxt]


def _fill_ext(ext, cur_ref, prev_ref, next_ref, tile, tm, seq_len):
    first = (tile * tm) % seq_len == 0
    last = ((tile + 1) * tm) % seq_len == 0
    ext[0:HALO, :] = jnp.where(first, 0.0, prev_ref[...])
    ext[HALO:HALO + tm, :] = cur_ref[...]
    ext[HALO + tm:HALO + tm + HALO, :] = jnp.where(last, 0.0, next_ref[...])


def _lru_kernel(xb_ref, xp_ref, xn_ref, cw_ref, cb_ref, wax_ref, ba_ref, bx_ref, lam_ref, h0_ref,
                o_ref, ext, a_scr, b_scr, carry, *, tm, seq_len, n_tiles, reverse):
    i = pl.program_id(0)
    tile = (n_tiles - 1 - i) if reverse else i
    _fill_ext(ext, xb_ref, xp_ref, xn_ref, tile, tm, seq_len)
    kk = cw_ref.shape[0]
    left = kk // 2
    xc = cb_ref[...] + cw_ref[0:1, :] * ext[pl.ds(HALO - left, tm), :]
    for k in range(1, kk):
        xc = xc + cw_ref[k:k + 1, :] * ext[pl.ds(HALO - left + k, tm), :]

    width = xc.shape[1]
    hd = width // LRU_HEADS
    lam = lam_ref[...]
    sp = jnp.maximum(-lam, 0.0) + jnp.log(1.0 + jnp.exp(-jnp.abs(lam)))
    for hh in range(LRU_HEADS):
        cs = slice(hh * hd, (hh + 1) * hd)
        xh = xc[:, cs]
        pre = _bdot(xh.astype(BF16), wax_ref[hh])
        rg = _sigmoid(pre[:, :hd] + ba_ref[:, cs])
        ig = _sigmoid(pre[:, hd:] + bx_ref[:, cs])
        a = jnp.exp(-LRU_C * rg * sp[:, cs])
        a_scr[:, cs] = a
        b_scr[:, cs] = jnp.sqrt(1.0 - a * a) * (ig * xh)

    first = (tile * tm) % seq_len == 0
    last = ((tile + 1) * tm) % seq_len == 0

    @pl.when(last if reverse else first)
    def _():
        carry[...] = jnp.broadcast_to(h0_ref[0], carry.shape)

    row = lax.broadcasted_iota(jnp.int32, (SUBLANES, width), 0)
    nblk = tm // SUBLANES

    def body(k, c):
        blk = (nblk - 1 - k) if reverse else k
        r0 = pl.multiple_of(blk * SUBLANES, SUBLANES)
        a = a_scr[pl.ds(r0, SUBLANES), :]
        b = b_scr[pl.ds(r0, SUBLANES), :]
        for s in (1, 2, 4):
            shift = (SUBLANES - s) if reverse else s
            a_sh = pltpu.roll(a, shift, 0)
            b_sh = pltpu.roll(b, shift, 0)
            m = (row < SUBLANES - s) if reverse else (row >= s)
            b = jnp.where(m, a * b_sh + b, b)
            a = jnp.where(m, a * a_sh, a)
        h = a * c + b
        b_scr[pl.ds(r0, SUBLANES), :] = h
        edge = h[0:1, :] if reverse else h[SUBLANES - 1:SUBLANES, :]
        return jnp.broadcast_to(edge, c.shape)

    carry[...] = lax.fori_loop(0, nblk, body, carry[...])
    o_ref[...] = b_scr[...].astype(o_ref.dtype)


def lru_scan(z, col_block, width, conv_w, conv_b, w_a, b_a, w_x, b_x, lam, h0, seq_len, reverse,
             out_dtype=F32, tm=512):
    r = z.shape[0]
    tm = _row_tile(seq_len, tm)
    n_tiles = r // tm
    nb = r // seq_len
    tile_of = (lambda i: n_tiles - 1 - i) if reverse else (lambda i: i)
    wax = jnp.concatenate([w_a, w_x], axis=-1).astype(BF16)
    kk = conv_w.shape[0]
    hd = width // LRU_HEADS
    const2 = lambda i: (0, 0)
    return pl.pallas_call(
        functools.partial(_lru_kernel, tm=tm, seq_len=seq_len, n_tiles=n_tiles, reverse=reverse),
        grid=(n_tiles,),
        in_specs=_halo_specs(tm, width, col_block, tile_of, r // HALO) + [
            pl.BlockSpec((kk, width), const2),
            pl.BlockSpec((1, width), const2),
            pl.BlockSpec((LRU_HEADS, hd, 2 * hd), lambda i: (0, 0, 0)),
            pl.BlockSpec((1, width), const2),
            pl.BlockSpec((1, width), const2),
            pl.BlockSpec((1, width), const2),
            pl.BlockSpec((1, 1, width), lambda i: ((tile_of(i) * tm) // seq_len, 0, 0)),
        ],
        out_specs=pl.BlockSpec((tm, width), lambda i: (tile_of(i), 0)),
        out_shape=jax.ShapeDtypeStruct((r, width), out_dtype),
        scratch_shapes=[
            pltpu.VMEM((tm + 2 * HALO, width), F32),
            pltpu.VMEM((tm, width), F32),
            pltpu.VMEM((tm, width), F32),
            pltpu.VMEM((SUBLANES, width), F32),
        ],
        compiler_params=_cparams("arbitrary"),
        name="lru_scan_bwd" if reverse else "lru_scan_fwd",
    )(z, z, z, conv_w, conv_b.reshape(1, width), wax, b_a.reshape(1, width), b_x.reshape(1, width),
      lam.reshape(1, width), h0.reshape(nb, 1, width))


def _lru_out_prologue(gate, hf, hb):
    return (_gelu(gate) * (hf.astype(F32) + hb.astype(F32))).astype(BF16)


def _pool_kernel(p_ref, pp_ref, pn_ref, wg_ref, bg_ref, sc_ref, o_ref, ext, *, tm, seq_len):
    i = pl.program_id(0)
    _fill_ext(ext, p_ref, pp_ref, pn_ref, i, tm, seq_len)
    width = o_ref.shape[1]
    gw = width // len(POOL_WINDOWS)
    t = ((i * tm) % seq_len + lax.broadcasted_iota(jnp.int32, (tm, 1), 0))
    for g, win in enumerate(POOL_WINDOWS):
        half = win // 2
        cs = slice(g * gw, (g + 1) * gw)
        s = ext[pl.ds(HALO - half, tm), cs]
        for k in range(1 - half, half):
            s = s + ext[pl.ds(HALO + k, tm), cs]
        cnt = (jnp.minimum(t + half, seq_len) - jnp.maximum(t - half, 0)).astype(F32)
        q = s / cnt - ext[pl.ds(HALO, tm), cs]
        y = _bdot(q.astype(BF16), wg_ref[g]) + bg_ref[:, cs]
        o_ref[:, cs] = (y * sc_ref[:, cs]).astype(BF16)


def pool_mix(p, w_g, b_g, scale, seq_len, tm=512):
    r, width = p.shape
    assert max(POOL_WINDOWS) // 2 <= HALO
    tm = _row_tile(seq_len, tm)
    ng, gw, _ = w_g.shape
    const2 = lambda i: (0, 0)
    return pl.pallas_call(
        functools.partial(_pool_kernel, tm=tm, seq_len=seq_len),
        grid=(r // tm,),
        in_specs=_halo_specs(tm, width, 0, lambda i: i, r // HALO) + [
            pl.BlockSpec((ng, gw, gw), lambda i: (0, 0, 0)),
            pl.BlockSpec((1, width), const2),
            pl.BlockSpec((1, width), const2),
        ],
        out_specs=pl.BlockSpec((tm, width), lambda i: (i, 0)),
        out_shape=jax.ShapeDtypeStruct((r, width), BF16),
        scratch_shapes=[pltpu.VMEM((tm + 2 * HALO, width), F32)],
        compiler_params=_cparams("parallel"),
        name="pool_mix",
    )(p, p, p, w_g.astype(BF16), b_g.reshape(1, width), scale.reshape(1, width))


def _hyfilter_kernel(z_ref, w1_ref, b1_ref, w2_ref, b2_ref, w3_ref, b3_ref, fr_ref, wo_ref, dl_ref,
                     o_ref, *, n_out, seq_len):
    z = z_ref[...]
    tm = z.shape[0]
    t = pl.program_id(0) * tm + lax.broadcasted_iota(jnp.int32, (tm, 1), 0)
    keep = (t != seq_len).astype(F32)
    h = jnp.sin(fr_ref[0:1, :] * (_hdot(z, w1_ref[...]) + b1_ref[...]))
    h = jnp.sin(fr_ref[1:2, :] * (_hdot(h, w2_ref[...]) + b2_ref[...]))
    h = jnp.sin(fr_ref[2:3, :] * (_hdot(h, w3_ref[...]) + b3_ref[...]))
    window = jnp.exp(-z[:, 0:1] * dl_ref[...]) * keep
    width = dl_ref.shape[1]
    hb = h.astype(BF16)
    for q in range(n_out):
        cs = slice(q * width, (q + 1) * width)
        o_ref[:, cs] = _bdot(hb, wo_ref[:, cs]) * window


def hyena_filters(seq_len, width, f_w1, f_b1, f_w2, f_b2, f_w3, f_b3, f_freq, f_wout, tm=512):
    n = 2 * seq_len
    rows = np.arange(n)
    lag = np.where(rows < seq_len, rows, np.minimum(n - rows, seq_len - 1))
    t = (jnp.asarray(lag, dtype=F32) / (seq_len - 1))[:, None]
    w = 2.0 * math.pi * jnp.asarray(lag, dtype=F32)[:, None] / seq_len
    bands = jnp.linspace(1e-4, HYENA_BANDS - 1, HYENA_BANDS, dtype=F32)[None, :]
    z = jnp.concatenate([t, jnp.cos(bands * w), jnp.sin(-bands * w)], axis=-1)
    emb = z.shape[1]
    hid = f_w1.shape[1]
    zp = jnp.zeros((n, LANES), F32).at[:, :emb].set(z)
    order = f_wout.shape[1] // (2 * width)
    wo = f_wout.reshape(hid, order, 2, width).transpose(2, 0, 1, 3).reshape(2, hid, order * width)
    w1p = jnp.zeros((LANES, hid), F32).at[:emb].set(f_w1)
    max_decay = math.log(HYENA_DECAY_TARGET) / HYENA_FAST_DECAY
    min_decay = math.log(HYENA_DECAY_TARGET) / HYENA_SLOW_DECAY
    deltas = jnp.abs(jnp.linspace(min_decay, max_decay, width, dtype=F32)).reshape(1, width)
    n_tot = order * width
    tm = _row_tile(seq_len, tm)
    c2 = lambda i: (0, 0)
    return pl.pallas_call(
        functools.partial(_hyfilter_kernel, n_out=order, seq_len=seq_len),
        grid=(n // tm,),
        in_specs=[
            pl.BlockSpec((tm, LANES), lambda i: (i, 0)),
            pl.BlockSpec((LANES, hid), c2), pl.BlockSpec((1, hid), c2),
            pl.BlockSpec((hid, hid), c2), pl.BlockSpec((1, hid), c2),
            pl.BlockSpec((hid, hid), c2), pl.BlockSpec((1, hid), c2),
            pl.BlockSpec((3, hid), c2),
            pl.BlockSpec((None, hid, n_tot), lambda i: ((i * tm) // seq_len, 0, 0)),
            pl.BlockSpec((1, width), c2),
        ],
        out_specs=pl.BlockSpec((tm, n_tot), lambda i: (i, 0)),
        out_shape=jax.ShapeDtypeStruct((n, n_tot), F32),
        compiler_params=_cparams("parallel"),
        name="hyena_filters",
    )(zp, w1p, f_b1.reshape(1, hid), f_w2, f_b2.reshape(1, hid), f_w3, f_b3.reshape(1, hid),
      f_freq, wo.astype(BF16), deltas)


def _dft_tables(seq_len):
    n = 2 * seq_len
    p_ = FFT_P
    n1 = n // p_
    t1n = n1 // 2
    f1 = np.arange(n1)[:, None]
    t1 = np.arange(n1)[None, :]
    ang1 = 2.0 * np.pi * ((f1 * t1) % n1) / n1
    c1, s1 = np.cos(ang1), np.sin(ang1)
    w1 = np.zeros((2 * n1, 2 * t1n))
    w1[:n1, :t1n] = c1[:, :t1n]
    w1[:n1, t1n:] = s1[:, :t1n]
    w1[n1:, :t1n] = -s1[:, :t1n]
    w1[n1:, t1n:] = c1[:, :t1n]
    nh = n1 // 2 + 1
    nhp = -(-nh // SUBLANES) * SUBLANES
    w1k = np.zeros((2 * nhp, n1))
    w1k[:nh] = c1[:nh]
    w1k[nhp:nhp + nh] = -s1[:nh]
    w1i = np.zeros((2 * t1n, 2 * n1))
    ct, st = c1.T[:t1n] / n, s1.T[:t1n] / n
    w1i[:t1n, :n1] = ct
    w1i[:t1n, n1:] = -st
    w1i[t1n:, :n1] = st
    w1i[t1n:, n1:] = ct
    f2 = np.arange(p_)[:, None]
    pp = np.arange(p_)[None, :]
    ang2 = 2.0 * np.pi * ((f2 * pp) % p_) / p_
    c2, s2 = np.cos(ang2), np.sin(ang2)
    fb = np.block([[c2, s2], [-s2, c2]])
    fbi = np.block([[c2, -s2], [s2, c2]])
    fb = np.stack([fb, np.concatenate([fb[:p_][::-1], fb[p_:][::-1]], axis=0)])
    fbi = np.stack([fbi, np.concatenate([fbi[:, :p_][:, ::-1], fbi[:, p_:][:, ::-1]], axis=1)])
    angt = 2.0 * np.pi * ((np.arange(n1)[:, None] * np.arange(p_)[None, :]) % n) / n
    lane_bcast = lambda a: jnp.broadcast_to(jnp.asarray(a, dtype=F32)[:, :, None], (n1, p_, LANES))
    as_bf = lambda a: jnp.asarray(a, dtype=F32).astype(BF16)
    return dict(n1=n1, t1n=t1n, nh=nh, w1=as_bf(w1), w1k=as_bf(w1k), w1i=as_bf(w1i), fb=as_bf(fb), fbi=as_bf(fbi),
                twc=lane_bcast(np.cos(angt)), tws=lane_bcast(-np.sin(angt)))


DFT_PB = 16


def _rows_at(ref, pp):
    m, pb, _ = ref.shape
    return ref.reshape(m * pb, LANES)[pl.ds(pp, m, stride=pb), :]


def _set_rows_at(ref, pp, val):
    m, pb, _ = ref.shape
    ref.reshape(m * pb, LANES)[pl.ds(pp, m, stride=pb), :] = val


def _pack_complex(re, im):
    r = lax.bitcast_convert_type(re.astype(BF16).astype(F32), jnp.uint32)
    i = lax.bitcast_convert_type(im.astype(BF16).astype(F32), jnp.uint32)
    return r | (i >> 16)


def _unpack_complex(w):
    re = lax.bitcast_convert_type(w & jnp.uint32(0xFFFF0000), F32)
    im = lax.bitcast_convert_type(w << 16, F32)
    return re, im


def _dft1_kernel(w_ref, x_ref, o_ref):
    half = w_ref.shape[0] // 2
    for pp in range(x_ref.shape[1]):
        y = _bdot(w_ref[...], _rows_at(x_ref, pp).astype(BF16))
        _set_rows_at(o_ref, pp, _pack_complex(y[:half], y[half:]))


def dft_stage1(w, x4, idx):
    m, k = w.shape
    m = m // 2
    _, _, p_, c = x4.shape
    pb = min(DFT_PB, p_)
    return pl.pallas_call(
        _dft1_kernel,
        grid=(c // LANES, p_ // pb),
        in_specs=[pl.BlockSpec((2 * m, k), lambda j, q: (0, 0)),
                  pl.BlockSpec((None, k, pb, LANES), lambda j, q: (idx, 0, q, j))],
        out_specs=pl.BlockSpec((m, pb, LANES), lambda j, q: (0, q, j)),
        out_shape=jax.ShapeDtypeStruct((m, p_, c), jnp.uint32),
        compiler_params=_cparams("parallel", "parallel"),
        name="dft_stage1",
    )(w, x4)


def _dft3_kernel(w_ref, b_ref, g_ref, v_ref, bias_ref, o_ref):
    for pp in range(b_ref.shape[1]):
        re, im = _unpack_complex(_rows_at(b_ref, pp))
        y = _bdot(w_ref[...], jnp.concatenate([re, im], axis=0).astype(BF16))
        _set_rows_at(o_ref, pp, _rows_at(g_ref, pp) * (y + bias_ref[...] * _rows_at(v_ref, pp)))


def dft_inverse_stage1_gate(w, b3, gate4, gate_idx, v4, v_idx, bias):
    m, k = w.shape
    k = k // 2
    _, p_, c = b3.shape
    pb = min(DFT_PB, p_)
    lead = lambda idx: (lambda j, q: (idx, 0, q, j))
    return pl.pallas_call(
        _dft3_kernel,
        grid=(c // LANES, p_ // pb),
        in_specs=[pl.BlockSpec((m, 2 * k), lambda j, q: (0, 0)),
                  pl.BlockSpec((k, pb, LANES), lambda j, q: (0, q, j)),
                  pl.BlockSpec((None, m, pb, LANES), lead(gate_idx)),
                  pl.BlockSpec((None, m, pb, LANES), lead(v_idx)),
                  pl.BlockSpec((1, LANES), lambda j, q: (0, j))],
        out_specs=pl.BlockSpec((m, pb, LANES), lambda j, q: (0, q, j)),
        out_shape=jax.ShapeDtypeStruct((m, p_, c), F32),
        compiler_params=_cparams("parallel", "parallel"),
        name="dft_inverse_stage1_gate",
    )(w, b3, gate4, v4, bias.reshape(1, c).astype(F32))


def _twiddle(re, im, tc, ts, reps):
    tc = jnp.tile(tc, (1, reps))
    ts = jnp.tile(ts, (1, reps))
    return re * tc - im * ts, re * ts + im * tc


def _spec_fwd_kernel(a_ref, tc_ref, ts_ref, fb_ref, o_ref):
    reps = a_ref.shape[-1] // LANES
    re, im = _unpack_complex(a_ref[0])
    tr, ti = _twiddle(re, im, tc_ref[0], ts_ref[0], reps)
    rhs = jnp.concatenate([tr, ti], axis=0).astype(BF16)
    o_ref[0] = _bdot(fb_ref[...], rhs).astype(o_ref.dtype)


def _spec_conv_kernel(a_ref, tc_ref, ts_ref, fb_ref, fbi_ref, k_ref, o_ref, *, n1):
    p_ = FFT_P
    reps = a_ref.shape[-1] // LANES
    conj = jnp.where(pl.program_id(0) > n1 // 2, -1.0, 1.0).astype(F32)
    tc, ts = tc_ref[0], ts_ref[0]
    re, im = _unpack_complex(a_ref[0])
    tr, ti = _twiddle(re, im, tc, ts, reps)
    x = _bdot(fb_ref[...], jnp.concatenate([tr, ti], axis=0).astype(BF16))
    xr, xi = x[:p_], x[p_:]
    kr, ki = k_ref[0, :p_].astype(F32), conj * k_ref[0, p_:].astype(F32)
    yr = xr * kr - xi * ki
    yi = xr * ki + xi * kr
    bv = _bdot(fbi_ref[...], jnp.concatenate([yr, yi], axis=0).astype(BF16))
    orr, oi = _twiddle(bv[:p_], bv[p_:], tc, -ts, reps)
    o_ref[0] = _pack_complex(orr, oi)


def spectrum_forward(a3, tab, ct=1024):
    _, p_, c = a3.shape
    nh = tab["nh"]
    ct = min(ct, c)
    return pl.pallas_call(
        _spec_fwd_kernel,
        grid=(c // ct, nh),
        in_specs=[
            pl.BlockSpec((1, p_, ct), lambda j, f: (f, 0, j)),
            pl.BlockSpec((1, p_, LANES), lambda j, f: (f, 0, 0)),
            pl.BlockSpec((1, p_, LANES), lambda j, f: (f, 0, 0)),
            pl.BlockSpec((None, 2 * p_, 2 * p_), lambda j, f: (0, 0, 0)),
        ],
        out_specs=pl.BlockSpec((1, 2 * p_, ct), lambda j, f: (f, 0, j)),
        out_shape=jax.ShapeDtypeStruct((nh, 2 * p_, c), BF16),
        compiler_params=_cparams("parallel", "parallel"),
        name="dft_filter_stage2",
    )(a3, tab["twc"], tab["tws"], tab["fb"])


def spectrum_conv(a3, kspec, k_col_block, tab):
    n1, p_, c = a3.shape
    half = n1 // 2
    mirrored = lambda f: (f > half).astype(jnp.int32)
    return pl.pallas_call(
        functools.partial(_spec_conv_kernel, n1=n1),
        grid=(n1,),
        in_specs=[
            pl.BlockSpec((1, p_, c), lambda f: (f, 0, 0)),
            pl.BlockSpec((1, p_, LANES), lambda f: (f, 0, 0)),
            pl.BlockSpec((1, p_, LANES), lambda f: (f, 0, 0)),
            pl.BlockSpec((None, 2 * p_, 2 * p_), lambda f: (mirrored(f), 0, 0)),
            pl.BlockSpec((None, 2 * p_, 2 * p_), lambda f: (mirrored(f), 0, 0)),
            pl.BlockSpec((1, 2 * p_, c), lambda f: (jnp.where(f > half, n1 - f, f), 0, k_col_block)),
        ],
        out_specs=pl.BlockSpec((1, p_, c), lambda f: (f, 0, 0)),
        out_shape=jax.ShapeDtypeStruct(a3.shape, jnp.uint32),
        compiler_params=_cparams("arbitrary"),
        name="dft_stage2_conv",
    )(a3, tab["twc"], tab["tws"], tab["fb"], tab["fbi"], kspec)


def hyena_long_convs(xv, filt, bias, batch, seq_len):
    assert batch == 2, "the two sequences of the batch are packed as one complex sequence"
    c = xv.shape[-1]
    tab = _dft_tables(seq_len)
    n1, t1n, p_ = tab["n1"], tab["t1n"], FFT_P
    order = bias.shape[0]
    ak = dft_stage1(tab["w1k"], filt.reshape(1, n1, p_, order * c), 0)
    kspec = spectrum_forward(ak, tab)
    xv4 = xv.reshape(3, batch * t1n, p_, c)
    y4, y_idx = xv4, 2
    for o in range(order):
        a = dft_stage1(tab["w1"], y4, y_idx)
        b3 = spectrum_conv(a, kspec, o, tab)
        y = dft_inverse_stage1_gate(tab["w1i"], b3, xv4, o, y4, y_idx, bias[o])
        y4, y_idx = y[None], 0
    return y.reshape(batch * seq_len, c)


def kernel(x, c, ctx, c_ctx, ada_w, ada_b, norm_g, gmlp_w_in, gmlp_g_v, gmlp_w_s, gmlp_b_s, gmlp_w_out, lru_w_in, lru_conv_w, lru_conv_b, lru_w_a, lru_b_a, lru_w_x, lru_b_x, lru_lam, lru_w_out, hyena_w_in, hyena_conv_w, hyena_conv_b, hyena_f_w1, hyena_f_b1, hyena_f_w2, hyena_f_b2, hyena_f_w3, hyena_f_b3, hyena_f_freq, hyena_f_wout, hyena_bias, hyena_w_out, pool_w_in, pool_w_g, pool_b_g, pool_scale, pool_w_out, ffn_w_gate, ffn_w_up, ffn_w_down, moe_w_router, moe_w_gate, moe_w_up, moe_w_down):
    B, L, D = x.shape
    Lc = ctx.shape[1]
    depth = ada_w.shape[0]
    n_mixers = 4
    assert B + 1 <= SUBLANES

    cc = jnp.zeros((SUBLANES, D), F32).at[:B].set(c).at[B].set(c_ctx)
    ada = ada_all(cc, ada_w, ada_b).reshape(depth, SUBLANES, 6, D)

    xs = add_pos(x)
    xc = ctx.reshape(B * Lc, D)
    last_ctx = max([i for i in range(depth) if i % n_mixers == 1], default=-1)

    for i in range(depth):
        kind, j = i % n_mixers, i // n_mixers
        update_ctx = i < last_ctx
        read_ctx = i <= last_ctx
        lat = [ada[i, :B, q].reshape(B, 1, D) for q in range(6)]
        cx = [jnp.broadcast_to(ada[i, B:B + 1, q].reshape(1, 1, D), (B, 1, D)) for q in range(6)]
        g = norm_g[i]
        streams = [(xs, lat, L)]
        if read_ctx:
            streams.append((xc, cx, Lc))

        if kind == 0:
            outs = []
            for (s, m, sl) in streams[:1 + int(update_ctx)]:
                hw = norm_mod_matmul(s, g[0], m[0], m[1], gmlp_w_in[j], sl, BF16, act="gelu")
                a = gmlp_spatial(hw, gmlp_g_v[j], gmlp_w_s[j], gmlp_b_s[j])
                outs.append(matmul_norm_res([(a, a.shape[1], 0)], _identity_bf16, gmlp_w_out[j],
                                            g[1], m[2], s, sl))
        elif kind == 1:
            W = lru_w_in.shape[2] // 2
            sc_args = lambda d: (lru_conv_w[j], lru_conv_b[j], lru_w_a[j, d], lru_b_a[j, d],
                                 lru_w_x[j, d], lru_b_x[j, d], lru_lam[j, d])
            zc = norm_mod_matmul(xc, g[0], cx[0], cx[1], lru_w_in[j], Lc, F32)
            zero = jnp.zeros((B, W), F32)
            hf_c = lru_scan(zc, 1, W, *sc_args(0), zero, Lc, False)
            hb_c = lru_scan(zc, 1, W, *sc_args(1), zero, Lc, True)
            zl = norm_mod_matmul(xs, g[0], lat[0], lat[1], lru_w_in[j], L, F32)
            hf = lru_scan(zl, 1, W, *sc_args(0), hf_c.reshape(B, Lc, W)[:, -1], L, False, out_dtype=BF16)
            hb = lru_scan(zl, 1, W, *sc_args(1), hb_c.reshape(B, Lc, W)[:, 0], L, True, out_dtype=BF16)
            outs = [matmul_norm_res([(zl, W, 0), (hf, W, 0), (hb, W, 0)], _lru_out_prologue,
                                    lru_w_out[j], g[1], lat[2], xs, L)]
            if update_ctx:
                outs.append(matmul_norm_res([(zc, W, 0), (hf_c, W, 0), (hb_c, W, 0)], _lru_out_prologue,
                                            lru_w_out[j], g[1], cx[2], xc, Lc))
        elif kind == 2:
            W = hyena_w_out.shape[1]
            filt = hyena_filters(L, W, hyena_f_w1[j], hyena_f_b1[j], hyena_f_w2[j], hyena_f_b2[j],
                                 hyena_f_w3[j], hyena_f_b3[j], hyena_f_freq[j], hyena_f_wout[j])
            outs = []
            for (s, m, sl) in streams[:1 + int(update_ctx)]:
                nb = s.shape[0] // sl
                xv = norm_mod_matmul_conv(s, g[0], m[0], m[1], hyena_w_in[j], hyena_conv_w[j],
                                          hyena_conv_b[j], W, sl)
                fl = filt if sl == L else hyena_filters(
                    sl, W, hyena_f_w1[j], hyena_f_b1[j], hyena_f_w2[j], hyena_f_b2[j],
                    hyena_f_w3[j], hyena_f_b3[j], hyena_f_freq[j], hyena_f_wout[j])
                y = hyena_long_convs(xv, fl, hyena_bias[j], nb, sl)
                outs.append(matmul_norm_res([(y, W, 0)], _identity_bf16, hyena_w_out[j], g[1], m[2], s, sl))
        else:
            outs = []
            for (s, m, sl) in streams[:1 + int(update_ctx)]:
                p = norm_mod_matmul(s, g[0], m[0], m[1], pool_w_in[j], sl, F32)
                a = pool_mix(p, pool_w_g[j], pool_b_g[j], pool_scale[j], sl)
                outs.append(matmul_norm_res([(a, a.shape[1], 0)], _identity_bf16, pool_w_out[j],
                                            g[1], m[2], s, sl))
        xs = outs[0]
        if update_ctx:
            xc = outs[1]

        k = i // 2
        todo = [(xs, lat, L)] + ([(xc, cx, Lc)] if update_ctx else [])
        res = []
        for (s, m, sl) in todo:
            if i % 2 == 0:
                res.append(ffn_dense(s, g[2], m[3], m[4], ffn_w_gate[k], ffn_w_up[k], ffn_w_down[k],
                                     g[3], m[5], sl))
            else:
                res.append(moe_block(s, g[2], m[3], m[4], moe_w_router[k], moe_w_gate, moe_w_up,
                                     moe_w_down, k, g[3], m[5], sl))
        xs = res[0]
        if update_ctx:
            xc = res[1]
    return xs.reshape(B, L, D)
```

```python
import functools
import math

import jax
import jax.numpy as jnp
import numpy as np
from jax import lax
from jax.experimental import pallas as pl
from jax.experimental.pallas import tpu as pltpu

F32 = jnp.float32
BF16 = jnp.bfloat16
EPS = 1e-6

VMEM_LIMIT_BYTES = 52 * 1024 * 1024
LANES = 128
SUBLANES = 8

GRID_W = 64
GMLP_CHUNK = 128
GMLP_GROUPS = 8
LRU_HEADS = 8
LRU_C = 8.0
POOL_WINDOWS = (2, 4, 8, 16)
HYENA_BANDS = 16
HYENA_FAST_DECAY = 0.3
HYENA_SLOW_DECAY = 1.5
HYENA_DECAY_TARGET = 1e-2
N_EXPERTS = 8
TOP_K = 2
FFT_P = 128
HALO = SUBLANES


def _cparams(*sem):
    return pltpu.CompilerParams(dimension_semantics=sem, vmem_limit_bytes=VMEM_LIMIT_BYTES)


def _rms(x, g):
    return x * lax.rsqrt(jnp.mean(x * x, axis=-1, keepdims=True) + EPS) * g


def _gelu(x):
    return 0.5 * x * (1.0 + jnp.tanh(math.sqrt(2.0 / math.pi) * (x + 0.044715 * (x * x * x))))


def _silu(x):
    return x * (1.0 / (1.0 + jnp.exp(-x)))


def _sigmoid(x):
    return 1.0 / (1.0 + jnp.exp(-x))


def _bdot(a, b):
    return jnp.dot(a, b, preferred_element_type=F32)


def _hdot(a, b):
    return jnp.dot(a, b, preferred_element_type=F32, precision=lax.Precision.HIGHEST)


def _to_row_tiles(val, ref):
    rows = val.shape[0]
    nq = val.shape[1] // LANES
    for q in range(nq):
        ref[pl.ds(q, rows, stride=nq), :] = val[:, q * LANES:(q + 1) * LANES].astype(ref.dtype)


def _from_row_tiles(ref, dst_ref):
    rows = dst_ref.shape[0]
    nq = dst_ref.shape[1] // LANES
    for q in range(nq):
        dst_ref[:, q * LANES:(q + 1) * LANES] = ref[pl.ds(q, rows, stride=nq), :].astype(dst_ref.dtype)


def _row_tile(rows, want):
    t = min(rows, want)
    assert rows % t == 0, (rows, t)
    return t


def _ada_kernel(c_ref, w_ref, b_ref, o_ref):
    o_ref[0] = _hdot(_silu(c_ref[...]), w_ref[0]) + b_ref[0]


def ada_all(cc, ada_w, ada_b):
    depth, d, d6 = ada_w.shape
    nchunk = d6 // d
    return pl.pallas_call(
        _ada_kernel,
        grid=(depth, nchunk),
        in_specs=[
            pl.BlockSpec((SUBLANES, d), lambda i, j: (0, 0)),
            pl.BlockSpec((1, d, d), lambda i, j: (i, 0, j)),
            pl.BlockSpec((1, 1, d), lambda i, j: (i, 0, j)),
        ],
        out_specs=pl.BlockSpec((1, SUBLANES, d), lambda i, j: (i, 0, j)),
        out_shape=jax.ShapeDtypeStruct((depth, SUBLANES, d6), F32),
        compiler_params=_cparams("parallel", "parallel"),
        name="ada",
    )(cc, ada_w, ada_b.reshape(depth, 1, d6))


def _pos_kernel(x_ref, rt_ref, ct_ref, o_ref):
    half = rt_ref.shape[-1]
    x = x_ref[...]
    o_ref[:, :, :half] = x[:, :, :half] + rt_ref[...]
    o_ref[:, :, half:] = x[:, :, half:] + ct_ref[...][None]


def add_pos(x):
    b, l, d = x.shape
    rows = l // GRID_W
    quarter = d // 4
    omega = 1.0 / (10000.0 ** (jnp.arange(quarter, dtype=F32) / quarter))

    def sincos(p):
        ang = p.reshape(-1, 1) * omega[None, :]
        return jnp.concatenate([jnp.sin(ang), jnp.cos(ang)], axis=-1)

    rtab = sincos(jnp.arange(rows, dtype=F32)).reshape(rows, 1, 2 * quarter)
    ctab = sincos(jnp.arange(GRID_W, dtype=F32))
    x3 = x.reshape(b * rows, GRID_W, d)
    tr = _row_tile(rows, 16)
    nrt = rows // tr
    out = pl.pallas_call(
        _pos_kernel,
        grid=(b * nrt,),
        in_specs=[
            pl.BlockSpec((tr, GRID_W, d), lambda i: (i, 0, 0)),
            pl.BlockSpec((tr, 1, 2 * quarter), lambda i: (i % nrt, 0, 0)),
            pl.BlockSpec((GRID_W, 2 * quarter), lambda i: (0, 0)),
        ],
        out_specs=pl.BlockSpec((tr, GRID_W, d), lambda i: (i, 0, 0)),
        out_shape=jax.ShapeDtypeStruct(x3.shape, F32),
        compiler_params=_cparams("parallel"),
        name="add_pos",
    )(x3, rtab, ctab)
    return out.reshape(b * l, d)


def _nmm_kernel(x_ref, g_ref, sh_ref, sc_ref, w_ref, o_ref, h_scr, *, act):
    @pl.when(pl.program_id(1) == 0)
    def _():
        h = _rms(x_ref[...], g_ref[...]) * (1.0 + sc_ref[0]) + sh_ref[0]
        h_scr[...] = h.astype(BF16)

    y = _bdot(h_scr[...], w_ref[...])
    if act == "gelu":
        y = _gelu(y)
    o_ref[...] = y.astype(o_ref.dtype)


def norm_mod_matmul(x, g, shift, scale, w, rows_per_mod, out_dtype, act=None, tm=512, tn=2048):
    r, d = x.shape
    n = w.shape[1]
    tm = _row_tile(min(r, rows_per_mod), tm)
    tn = min(tn, n)
    assert n % tn == 0
    mod_map = lambda i, j: ((i * tm) // rows_per_mod, 0, 0)
    return pl.pallas_call(
        functools.partial(_nmm_kernel, act=act),
        grid=(r // tm, n // tn),
        in_specs=[
            pl.BlockSpec((tm, d), lambda i, j: (i, 0)),
            pl.BlockSpec((1, d), lambda i, j: (0, 0)),
            pl.BlockSpec((1, 1, d), mod_map),
            pl.BlockSpec((1, 1, d), mod_map),
            pl.BlockSpec((d, tn), lambda i, j: (0, j)),
        ],
        out_specs=pl.BlockSpec((tm, tn), lambda i, j: (i, j)),
        out_shape=jax.ShapeDtypeStruct((r, n), out_dtype),
        scratch_shapes=[pltpu.VMEM((tm, d), BF16)],
        compiler_params=_cparams("parallel", "arbitrary"),
        name="norm_mod_matmul",
    )(x, g.reshape(1, d), shift, scale, w.astype(BF16))


HALO_BF16 = 2 * SUBLANES


def _nmm_conv_kernel(x_ref, xp_ref, xn_ref, g_ref, sh_ref, sc_ref, w_ref, cw_ref, cb_ref, o_ref,
                     h_scr, z_scr, *, tm, seq_len):
    i = pl.program_id(0)
    hl = HALO_BF16

    @pl.when(pl.program_id(1) == 0)
    def _():
        def nm(x):
            return (_rms(x, g_ref[...]) * (1.0 + sc_ref[0]) + sh_ref[0]).astype(BF16)
        h_scr[0:hl, :] = nm(xp_ref[...])
        h_scr[hl:hl + tm, :] = nm(x_ref[...])
        h_scr[hl + tm:hl + tm + hl, :] = nm(xn_ref[...])

    z = _bdot(h_scr[...], w_ref[...])
    first = (i * tm) % seq_len == 0
    last = ((i + 1) * tm) % seq_len == 0
    row = lax.broadcasted_iota(jnp.int32, (tm + 2 * hl, 1), 0)
    outside = jnp.logical_or(jnp.logical_and(first, row < hl), jnp.logical_and(last, row >= hl + tm))
    z_scr[...] = jnp.where(outside, 0.0, z)
    kk = cw_ref.shape[0]
    left = kk // 2
    y = cb_ref[...] + cw_ref[0:1, :] * z_scr[pl.ds(hl - left, tm), :]
    for k in range(1, kk):
        y = y + cw_ref[k:k + 1, :] * z_scr[pl.ds(hl - left + k, tm), :]
    o_ref[0] = y


def norm_mod_matmul_conv(x, g, shift, scale, w, conv_w, conv_b, width, seq_len, tm=512):
    r, d = x.shape
    n = w.shape[1]
    nsplit = n // width
    tm = _row_tile(seq_len, tm)
    hl = HALO_BF16
    per = tm // hl
    n_row_blocks = r // hl
    kk = conv_w.shape[0]
    mod_map = lambda i, j: ((i * tm) // seq_len, 0, 0)
    return pl.pallas_call(
        functools.partial(_nmm_conv_kernel, tm=tm, seq_len=seq_len),
        grid=(r // tm, nsplit),
        in_specs=[
            pl.BlockSpec((tm, d), lambda i, j: (i, 0)),
            pl.BlockSpec((hl, d), lambda i, j: (jnp.maximum(i * per - 1, 0), 0)),
            pl.BlockSpec((hl, d), lambda i, j: (jnp.minimum((i + 1) * per, n_row_blocks - 1), 0)),
            pl.BlockSpec((1, d), lambda i, j: (0, 0)),
            pl.BlockSpec((1, 1, d), mod_map),
            pl.BlockSpec((1, 1, d), mod_map),
            pl.BlockSpec((d, width), lambda i, j: (0, j)),
            pl.BlockSpec((kk, width), lambda i, j: (0, j)),
            pl.BlockSpec((1, width), lambda i, j: (0, j)),
        ],
        out_specs=pl.BlockSpec((1, tm, width), lambda i, j: (j, i, 0)),
        out_shape=jax.ShapeDtypeStruct((nsplit, r, width), F32),
        scratch_shapes=[pltpu.VMEM((tm + 2 * hl, d), BF16), pltpu.VMEM((tm + 2 * hl, width), F32)],
        compiler_params=_cparams("parallel", "arbitrary"),
        name="norm_mod_matmul_conv",
    )(x, x, x, g.reshape(1, d), shift, scale, w.astype(BF16), conv_w, conv_b.reshape(1, n))


def _mnr_kernel(*refs, n_in, prologue):
    in_refs = refs[:n_in]
    w_ref, g_ref, gate_ref, x_ref, o_ref = refs[n_in:]
    a = prologue(*[r[...] for r in in_refs])
    y = _bdot(a, w_ref[...])
    o_ref[...] = x_ref[...] + gate_ref[0] * _rms(y, g_ref[...])


def matmul_norm_res(inputs, prologue, w, g, gate, x, rows_per_mod, tm=512):
    r, d = x.shape
    k = w.shape[0]
    tm = _row_tile(min(r, rows_per_mod), tm)
    mod_map = lambda i: ((i * tm) // rows_per_mod, 0, 0)
    in_specs = [pl.BlockSpec((tm, wd), functools.partial(lambda i, cb: (i, cb), cb=cb))
                for (_, wd, cb) in inputs]
    in_specs += [
        pl.BlockSpec((k, d), lambda i: (0, 0)),
        pl.BlockSpec((1, d), lambda i: (0, 0)),
        pl.BlockSpec((1, 1, d), mod_map),
        pl.BlockSpec((tm, d), lambda i: (i, 0)),
    ]
    return pl.pallas_call(
        functools.partial(_mnr_kernel, n_in=len(inputs), prologue=prologue),
        grid=(r // tm,),
        in_specs=in_specs,
        out_specs=pl.BlockSpec((tm, d), lambda i: (i, 0)),
        out_shape=jax.ShapeDtypeStruct((r, d), F32),
        compiler_params=_cparams("parallel"),
        name="matmul_norm_res",
    )(*[a for (a, _, _) in inputs], w.astype(BF16), g.reshape(1, d), gate, x)


def _identity_bf16(a):
    return a.astype(BF16)


def _swiglu_step(h_scr, wg_ref, wu_ref, wd_ref, acc_scr, j):
    h = h_scr[...]
    t = (_silu(_bdot(h, wg_ref[...].astype(BF16))) * _bdot(h, wu_ref[...].astype(BF16))).astype(BF16)
    part = _bdot(t, wd_ref[...].astype(BF16))

    @pl.when(j == 0)
    def _():
        acc_scr[...] = part

    @pl.when(j > 0)
    def _():
        acc_scr[...] += part


def _ffn_dense_kernel(x_ref, g1_ref, sh_ref, sc_ref, wg_ref, wu_ref, wd_ref, g2_ref, gate_ref,
                      o_ref, h_scr, acc_scr):
    j = pl.program_id(1)

    @pl.when(j == 0)
    def _():
        h = _rms(x_ref[...], g1_ref[...]) * (1.0 + sc_ref[0]) + sh_ref[0]
        h_scr[...] = h.astype(BF16)

    _swiglu_step(h_scr, wg_ref, wu_ref, wd_ref, acc_scr, j)

    @pl.when(j == pl.num_programs(1) - 1)
    def _():
        o_ref[...] = x_ref[...] + gate_ref[0] * _rms(acc_scr[...], g2_ref[...])


def ffn_dense(x, g1, shift, scale, wg, wu, wd, g2, gate, rows_per_mod, tm=512, tf=1408):
    r, d = x.shape
    ff = wg.shape[1]
    tm = _row_tile(min(r, rows_per_mod), tm)
    assert ff % tf == 0
    mod_map = lambda i, j: ((i * tm) // rows_per_mod, 0, 0)
    return pl.pallas_call(
        _ffn_dense_kernel,
        grid=(r // tm, ff // tf),
        in_specs=[
            pl.BlockSpec((tm, d), lambda i, j: (i, 0)),
            pl.BlockSpec((1, d), lambda i, j: (0, 0)),
            pl.BlockSpec((1, 1, d), mod_map),
            pl.BlockSpec((1, 1, d), mod_map),
            pl.BlockSpec((d, tf), lambda i, j: (0, j)),
            pl.BlockSpec((d, tf), lambda i, j: (0, j)),
            pl.BlockSpec((tf, d), lambda i, j: (j, 0)),
            pl.BlockSpec((1, d), lambda i, j: (0, 0)),
            pl.BlockSpec((1, 1, d), mod_map),
        ],
        out_specs=pl.BlockSpec((tm, d), lambda i, j: (i, 0)),
        out_shape=jax.ShapeDtypeStruct((r, d), F32),
        scratch_shapes=[pltpu.VMEM((tm, d), BF16), pltpu.VMEM((tm, d), F32)],
        compiler_params=_cparams("parallel", "arbitrary"),
        name="ffn_dense",
    )(x, g1.reshape(1, d), shift, scale, wg.astype(BF16), wu.astype(BF16), wd.astype(BF16),
      g2.reshape(1, d), gate)


def _ffn_routed_kernel(te_ref, nu_ref, idx0_ref, idxn_ref, h_hbm, wg_ref, wu_ref, wd_ref, o_ref,
                       xbuf, sems, h_scr, acc_scr, *, tm, nq, issue_steps):
    i = pl.program_id(0)
    j = pl.program_id(1)
    last = pl.num_programs(1) - 1
    used = i < nu_ref[0]
    slot = i % 2
    per = tm // issue_steps

    def row_copy(idx_ref, r, s):
        s0 = pl.multiple_of(idx_ref[r] * nq, nq)
        d0 = pl.multiple_of(r * nq, nq)
        return pltpu.make_async_copy(h_hbm.at[pl.ds(s0, nq)], xbuf.at[s, pl.ds(d0, nq)], sems.at[s])

    @pl.when(used)
    def _():
        @pl.when(jnp.logical_and(i == 0, j == 0))
        def _():
            def issue(r, carry):
                row_copy(idx0_ref, r, 0).start()
                return carry
            lax.fori_loop(0, tm, issue, 0)

        @pl.when(j == 0)
        def _():
            pltpu.make_async_copy(h_hbm.at[pl.ds(0, tm * nq)], xbuf.at[slot], sems.at[slot]).wait()
            _from_row_tiles(xbuf.at[slot], h_scr)

        @pl.when(jnp.logical_and(i + 1 < nu_ref[0], j < issue_steps))
        def _():
            for k in range(per):
                row_copy(idxn_ref, j * per + k, 1 - slot).start(priority=k % 2)

        _swiglu_step(h_scr, wg_ref.at[0], wu_ref.at[0], wd_ref.at[0], acc_scr, j)

        @pl.when(j == last)
        def _():
            _to_row_tiles(acc_scr[...], o_ref)

    @pl.when(jnp.logical_and(jnp.logical_not(used), j == last))
    def _():
        o_ref[...] = jnp.zeros_like(o_ref)


def ffn_routed(h_rt, src, tile_expert, n_used, wg, wu, wd, tm, tf=512):
    dq = wg.shape[1] // LANES
    d = dq * LANES
    p = src.shape[0]
    ff = wg.shape[2]
    assert ff % tf == 0 and p % tm == 0
    n_tiles = p // tm
    issue_steps = min(4, ff // tf)
    assert tm % issue_steps == 0
    grid_spec = pltpu.PrefetchScalarGridSpec(
        num_scalar_prefetch=2,
        grid=(n_tiles, ff // tf),
        in_specs=[
            pl.BlockSpec((tm,), lambda i, j, te, nu: (0,), memory_space=pltpu.SMEM),
            pl.BlockSpec((tm,), lambda i, j, te, nu: (jnp.minimum(i + 1, n_tiles - 1),),
                         memory_space=pltpu.SMEM),
            pl.BlockSpec(memory_space=pl.ANY),
            pl.BlockSpec((1, d, tf), lambda i, j, te, nu: (te[i], 0, j)),
            pl.BlockSpec((1, d, tf), lambda i, j, te, nu: (te[i], 0, j)),
            pl.BlockSpec((1, tf, d), lambda i, j, te, nu: (te[i], j, 0)),
        ],
        out_specs=pl.BlockSpec((tm * dq, LANES), lambda i, j, te, nu: (i, 0)),
        scratch_shapes=[pltpu.VMEM((2, tm * dq, LANES), F32), pltpu.SemaphoreType.DMA((2,)),
                        pltpu.VMEM((tm, d), BF16), pltpu.VMEM((tm, d), F32)],
    )
    return pl.pallas_call(
        functools.partial(_ffn_routed_kernel, tm=tm, nq=dq, issue_steps=issue_steps),
        grid_spec=grid_spec,
        out_shape=jax.ShapeDtypeStruct((p * dq, LANES), F32),
        compiler_params=_cparams("arbitrary", "arbitrary"),
        name="ffn_routed",
    )(tile_expert, n_used, src, src, h_rt, wg, wu, wd)


ROUTE_IDX0 = N_EXPERTS
ROUTE_P0 = N_EXPERTS + TOP_K


def _router_kernel(x_ref, g_ref, sh_ref, sc_ref, wr_ref, h_ref, r_ref):
    h = _rms(x_ref[...], g_ref[...]) * (1.0 + sc_ref[0]) + sh_ref[0]
    _to_row_tiles(h, h_ref)
    h_hi = h.astype(BF16)
    h_lo = (h - h_hi.astype(F32)).astype(BF16)
    logits = _bdot(h_hi, wr_ref[0]) + (_bdot(h_lo, wr_ref[0]) + _bdot(h_hi, wr_ref[1]))
    lane = lax.broadcasted_iota(jnp.int32, logits.shape, 1)
    neg = jnp.float32(-jnp.inf)
    big = jnp.int32(LANES)
    lg = jnp.where(lane < N_EXPERTS, logits, neg)
    m1 = jnp.max(lg, axis=-1, keepdims=True)
    i1 = jnp.min(jnp.where(lg == m1, lane, big), axis=-1, keepdims=True)
    lg2 = jnp.where(lane == i1, neg, lg)
    m2 = jnp.max(lg2, axis=-1, keepdims=True)
    i2 = jnp.min(jnp.where(lg2 == m2, lane, big), axis=-1, keepdims=True)
    e2 = jnp.exp(m2 - m1)
    p1 = 1.0 / (1.0 + e2)
    p2 = e2 / (1.0 + e2)
    out = jnp.where(lane == ROUTE_IDX0, i1.astype(F32), 0.0)
    out = jnp.where(lane == ROUTE_IDX0 + 1, i2.astype(F32), out)
    out = jnp.where(lane == ROUTE_P0, p1, out)
    out = jnp.where(lane == ROUTE_P0 + 1, p2, out)
    r_ref[...] = out


def router(x, g, shift, scale, w_router, rows_per_mod, tm=512):
    r, d = x.shape
    tm = _row_tile(min(r, rows_per_mod), tm)
    wr = jnp.zeros((d, LANES), F32).at[:, :N_EXPERTS].set(w_router)
    wr_hi = wr.astype(BF16)
    wr = jnp.stack([wr_hi, (wr - wr_hi.astype(F32)).astype(BF16)])
    mod_map = lambda i: ((i * tm) // rows_per_mod, 0, 0)
    return pl.pallas_call(
        _router_kernel,
        grid=(r // tm,),
        in_specs=[
            pl.BlockSpec((tm, d), lambda i: (i, 0)),
            pl.BlockSpec((1, d), lambda i: (0, 0)),
            pl.BlockSpec((1, 1, d), mod_map),
            pl.BlockSpec((1, 1, d), mod_map),
            pl.BlockSpec((2, d, LANES), lambda i: (0, 0, 0)),
        ],
        out_specs=[pl.BlockSpec((tm * (d // LANES), LANES), lambda i: (i, 0)),
                   pl.BlockSpec((tm, LANES), lambda i: (i, 0))],
        out_shape=[jax.ShapeDtypeStruct((r * (d // LANES), LANES), F32), jax.ShapeDtypeStruct((r, LANES), F32)],
        compiler_params=_cparams("parallel"),
        name="router",
    )(x, g.reshape(1, d), shift, scale, wr)


GATHER_UNROLL = 8


def _combine_kernel(f0_ref, f1_ref, n0_ref, n1_ref, ys_hbm, r_ref, g_ref, gate_ref, x_ref, o_ref,
                    ybuf, sems, y_scr, *, tm, nq):
    i = pl.program_id(0)
    nt = pl.num_programs(0)
    slot = i % 2

    def issue_tile(idx_refs, s):
        def issue(rb, carry):
            for k in range(GATHER_UNROLL):
                r = GATHER_UNROLL * rb + k
                d0 = pl.multiple_of(r * nq, nq)
                for c, idx_ref in enumerate(idx_refs):
                    s0 = pl.multiple_of(idx_ref[r] * nq, nq)
                    pltpu.make_async_copy(ys_hbm.at[pl.ds(s0, nq)], ybuf.at[s, c, pl.ds(d0, nq)],
                                          sems.at[s]).start(priority=c)
            return carry
        lax.fori_loop(0, tm // GATHER_UNROLL, issue, 0)

    @pl.when(i == 0)
    def _():
        issue_tile((f0_ref, f1_ref), 0)

    @pl.when(i + 1 < nt)
    def _():
        issue_tile((n0_ref, n1_ref), 1 - slot)

    for c in range(TOP_K):
        pltpu.make_async_copy(ys_hbm.at[pl.ds(0, tm * nq)], ybuf.at[slot, c], sems.at[slot]).wait()
    rt = r_ref[...]
    p0 = rt[:, ROUTE_P0:ROUTE_P0 + 1]
    p1 = rt[:, ROUTE_P0 + 1:ROUTE_P0 + 2]
    _from_row_tiles(ybuf.at[slot, 0], y_scr)
    y = p0 * y_scr[...]
    _from_row_tiles(ybuf.at[slot, 1], y_scr)
    y = y + p1 * y_scr[...]
    o_ref[...] = x_ref[...] + gate_ref[0] * _rms(y, g_ref[...])


def combine(ys, slot0, slot1, route, g, gate, x, rows_per_mod, tm=512):
    r, d = x.shape
    nq = d // LANES
    tm = _row_tile(min(r, rows_per_mod), tm)
    nt = r // tm
    mod_map = lambda i: ((i * tm) // rows_per_mod, 0, 0)
    first = pl.BlockSpec((tm,), lambda i: (0,), memory_space=pltpu.SMEM)
    nxt = pl.BlockSpec((tm,), lambda i: (jnp.minimum(i + 1, nt - 1),), memory_space=pltpu.SMEM)
    return pl.pallas_call(
        functools.partial(_combine_kernel, tm=tm, nq=nq),
        grid=(nt,),
        in_specs=[
            first, first, nxt, nxt,
            pl.BlockSpec(memory_space=pl.ANY),
            pl.BlockSpec((tm, LANES), lambda i: (i, 0)),
            pl.BlockSpec((1, d), lambda i: (0, 0)),
            pl.BlockSpec((1, 1, d), mod_map),
            pl.BlockSpec((tm, d), lambda i: (i, 0)),
        ],
        out_specs=pl.BlockSpec((tm, d), lambda i: (i, 0)),
        out_shape=jax.ShapeDtypeStruct((r, d), F32),
        scratch_shapes=[pltpu.VMEM((2, 2, tm * nq, LANES), F32), pltpu.SemaphoreType.DMA((2,)),
                        pltpu.VMEM((tm, d), F32)],
        compiler_params=_cparams("arbitrary"),
        name="moe_combine",
    )(slot0, slot1, slot0, slot1, ys, route, g.reshape(1, d), gate, x)


def moe_block(x, g1, shift, scale, w_router, wg_all, wu_all, wd_all, layer, g2, gate, rows_per_mod, tm=1024):
    n, d = x.shape
    ff = wg_all.shape[-1]
    wg = wg_all.reshape(-1, d, ff)
    wu = wu_all.reshape(-1, d, ff)
    wd = wd_all.reshape(-1, ff, d)
    h, route = router(x, g1, shift, scale, w_router, rows_per_mod)
    eidx = route[:, ROUTE_IDX0:ROUTE_IDX0 + TOP_K].astype(jnp.int32)
    flat_e = eidx.T.reshape(-1)
    onehot = (flat_e[:, None] == jnp.arange(N_EXPERTS, dtype=jnp.int32)[None, :]).astype(jnp.int32)
    csum = jnp.cumsum(onehot, axis=0)
    counts = csum[-1]
    rank = jnp.take_along_axis(csum, flat_e[:, None], axis=1)[:, 0] - 1
    tiles_per_e = (counts + tm - 1) // tm
    tile_end = jnp.cumsum(tiles_per_e)
    tile_start = tile_end - tiles_per_e
    slot = tile_start[flat_e] * tm + rank
    n_tiles = (TOP_K * n) // tm + N_EXPERTS
    tok = jnp.tile(jnp.arange(n, dtype=jnp.int32), TOP_K)
    src = jnp.zeros((n_tiles * tm,), jnp.int32).at[slot].set(tok)
    tile_ids = jnp.arange(n_tiles, dtype=jnp.int32)
    tile_expert = jnp.minimum(jnp.sum((tile_ids[:, None] >= tile_end[None, :]).astype(jnp.int32), axis=1),
                              N_EXPERTS - 1).astype(jnp.int32)
    n_used = tile_end[-1:].astype(jnp.int32)

    ys = ffn_routed(h, src, tile_expert + layer * N_EXPERTS, n_used, wg, wu, wd, tm)
    slot = slot.astype(jnp.int32)
    return combine(ys, slot[:n], slot[n:], route, g2, gate, x, rows_per_mod)


def _gmlp_kernel(u_ref, v_ref, gv_ref, ws_ref, bs_ref, o_ref, vn_scr):
    v = v_ref[...].astype(F32)
    mu = jnp.mean(v, axis=-1, keepdims=True)
    vc = v - mu
    vn = vc * lax.rsqrt(jnp.mean(vc * vc, axis=-1, keepdims=True) + EPS) * gv_ref[...]
    vn_scr[...] = vn.astype(BF16)
    tm, width = vn_scr.shape
    gw = width // GMLP_GROUPS
    for n in range(tm // GMLP_CHUNK):
        rs = slice(n * GMLP_CHUNK, (n + 1) * GMLP_CHUNK)
        for g in range(GMLP_GROUPS):
            cs = slice(g * gw, (g + 1) * gw)
            m = _bdot(ws_ref[g], vn_scr[rs, cs]) + bs_ref[g]
            o_ref[rs, cs] = (u_ref[rs, cs].astype(F32) * m).astype(BF16)


def gmlp_spatial(hw, g_v, w_s, b_s, tm=512):
    r, w2 = hw.shape
    width = w2 // 2
    gw = width // GMLP_GROUPS
    tm = _row_tile(r, tm)
    bsb = jnp.broadcast_to(b_s[:, :, None], (GMLP_GROUPS, GMLP_CHUNK, gw)).astype(F32)
    return pl.pallas_call(
        _gmlp_kernel,
        grid=(r // tm,),
        in_specs=[
            pl.BlockSpec((tm, width), lambda i: (i, 0)),
            pl.BlockSpec((tm, width), lambda i: (i, 1)),
            pl.BlockSpec((1, width), lambda i: (0, 0)),
            pl.BlockSpec((GMLP_GROUPS, GMLP_CHUNK, GMLP_CHUNK), lambda i: (0, 0, 0)),
            pl.BlockSpec((GMLP_GROUPS, GMLP_CHUNK, gw), lambda i: (0, 0, 0)),
        ],
        out_specs=pl.BlockSpec((tm, width), lambda i: (i, 0)),
        out_shape=jax.ShapeDtypeStruct((r, width), BF16),
        scratch_shapes=[pltpu.VMEM((tm, width), BF16)],
        compiler_params=_cparams("parallel"),
        name="gmlp_spatial",
    )(hw, hw, g_v.reshape(1, width), w_s.astype(BF16), bsb)


def _halo_specs(tm, width, col_block, tile_of, n_row_blocks):
    per = tm // HALO
    cur = pl.BlockSpec((tm, width), lambda i: (tile_of(i), col_block))
    prev = pl.BlockSpec((HALO, width), lambda i: (jnp.maximum(tile_of(i) * per - 1, 0), col_block))
    nxt = pl.BlockSpec((HALO, width),
                       lambda i: (jnp.minimum((tile_of(i) + 1) * per, n_row_blocks - 1), col_block))
    return [cur, prev, nxt]


def _fill_ext(ext, cur_ref, prev_ref, next_ref, tile, tm, seq_len):
    first = (tile * tm) % seq_len == 0
    last = ((tile + 1) * tm) % seq_len == 0
    ext[0:HALO, :] = jnp.where(first, 0.0, prev_ref[...])
    ext[HALO:HALO + tm, :] = cur_ref[...]
    ext[HALO + tm:HALO + tm + HALO, :] = jnp.where(last, 0.0, next_ref[...])


def _lru_kernel(xb_ref, xp_ref, xn_ref, cw_ref, cb_ref, wax_ref, ba_ref, bx_ref, lam_ref, h0_ref,
                o_ref, ext, a_scr, b_scr, carry, *, tm, seq_len, n_tiles, reverse):
    i = pl.program_id(0)
    tile = (n_tiles - 1 - i) if reverse else i
    _fill_ext(ext, xb_ref, xp_ref, xn_ref, tile, tm, seq_len)
    kk = cw_ref.shape[0]
    left = kk // 2
    xc = cb_ref[...] + cw_ref[0:1, :] * ext[pl.ds(HALO - left, tm), :]
    for k in range(1, kk):
        xc = xc + cw_ref[k:k + 1, :] * ext[pl.ds(HALO - left + k, tm), :]

    width = xc.shape[1]
    hd = width // LRU_HEADS
    lam = lam_ref[...]
    sp = jnp.maximum(-lam, 0.0) + jnp.log(1.0 + jnp.exp(-jnp.abs(lam)))
    for hh in range(LRU_HEADS):
        cs = slice(hh * hd, (hh + 1) * hd)
        xh = xc[:, cs]
        pre = _bdot(xh.astype(BF16), wax_ref[hh])
        rg = _sigmoid(pre[:, :hd] + ba_ref[:, cs])
        ig = _sigmoid(pre[:, hd:] + bx_ref[:, cs])
        a = jnp.exp(-LRU_C * rg * sp[:, cs])
        a_scr[:, cs] = a
        b_scr[:, cs] = jnp.sqrt(1.0 - a * a) * (ig * xh)

    first = (tile * tm) % seq_len == 0
    last = ((tile + 1) * tm) % seq_len == 0

    @pl.when(last if reverse else first)
    def _():
        carry[...] = jnp.broadcast_to(h0_ref[0], carry.shape)

    row = lax.broadcasted_iota(jnp.int32, (SUBLANES, width), 0)
    nblk = tm // SUBLANES

    def body(k, c):
        blk = (nblk - 1 - k) if reverse else k
        r0 = pl.multiple_of(blk * SUBLANES, SUBLANES)
        a = a_scr[pl.ds(r0, SUBLANES), :]
        b = b_scr[pl.ds(r0, SUBLANES), :]
        for s in (1, 2, 4):
            shift = (SUBLANES - s) if reverse else s
            a_sh = pltpu.roll(a, shift, 0)
            b_sh = pltpu.roll(b, shift, 0)
            m = (row < SUBLANES - s) if reverse else (row >= s)
            b = jnp.where(m, a * b_sh + b, b)
            a = jnp.where(m, a * a_sh, a)
        h = a * c + b
        b_scr[pl.ds(r0, SUBLANES), :] = h
        edge = h[0:1, :] if reverse else h[SUBLANES - 1:SUBLANES, :]
        return jnp.broadcast_to(edge, c.shape)

    carry[...] = lax.fori_loop(0, nblk, body, carry[...])
    o_ref[...] = b_scr[...].astype(o_ref.dtype)


def lru_scan(z, col_block, width, conv_w, conv_b, w_a, b_a, w_x, b_x, lam, h0, seq_len, reverse,
             out_dtype=F32, tm=512):
    r = z.shape[0]
    tm = _row_tile(seq_len, tm)
    n_tiles = r // tm
    nb = r // seq_len
    tile_of = (lambda i: n_tiles - 1 - i) if reverse else (lambda i: i)
    wax = jnp.concatenate([w_a, w_x], axis=-1).astype(BF16)
    kk = conv_w.shape[0]
    hd = width // LRU_HEADS
    const2 = lambda i: (0, 0)
    return pl.pallas_call(
        functools.partial(_lru_kernel, tm=tm, seq_len=seq_len, n_tiles=n_tiles, reverse=reverse),
        grid=(n_tiles,),
        in_specs=_halo_specs(tm, width, col_block, tile_of, r // HALO) + [
            pl.BlockSpec((kk, width), const2),
            pl.BlockSpec((1, width), const2),
            pl.BlockSpec((LRU_HEADS, hd, 2 * hd), lambda i: (0, 0, 0)),
            pl.BlockSpec((1, width), const2),
            pl.BlockSpec((1, width), const2),
            pl.BlockSpec((1, width), const2),
            pl.BlockSpec((1, 1, width), lambda i: ((tile_of(i) * tm) // seq_len, 0, 0)),
        ],
        out_specs=pl.BlockSpec((tm, width), lambda i: (tile_of(i), 0)),
        out_shape=jax.ShapeDtypeStruct((r, width), out_dtype),
        scratch_shapes=[
            pltpu.VMEM((tm + 2 * HALO, width), F32),
            pltpu.VMEM((tm, width), F32),
            pltpu.VMEM((tm, width), F32),
            pltpu.VMEM((SUBLANES, width), F32),
        ],
        compiler_params=_cparams("arbitrary"),
        name="lru_scan_bwd" if reverse else "lru_scan_fwd",
    )(z, z, z, conv_w, conv_b.reshape(1, width), wax, b_a.reshape(1, width), b_x.reshape(1, width),
      lam.reshape(1, width), h0.reshape(nb, 1, width))


def _lru_out_prologue(gate, hf, hb):
    return (_gelu(gate) * (hf.astype(F32) + hb.astype(F32))).astype(BF16)


def _pool_kernel(p_ref, pp_ref, pn_ref, wg_ref, bg_ref, sc_ref, o_ref, ext, *, tm, seq_len):
    i = pl.program_id(0)
    _fill_ext(ext, p_ref, pp_ref, pn_ref, i, tm, seq_len)
    width = o_ref.shape[1]
    gw = width // len(POOL_WINDOWS)
    t = ((i * tm) % seq_len + lax.broadcasted_iota(jnp.int32, (tm, 1), 0))
    for g, win in enumerate(POOL_WINDOWS):
        half = win // 2
        cs = slice(g * gw, (g + 1) * gw)
        s = ext[pl.ds(HALO - half, tm), cs]
        for k in range(1 - half, half):
            s = s + ext[pl.ds(HALO + k, tm), cs]
        cnt = (jnp.minimum(t + half, seq_len) - jnp.maximum(t - half, 0)).astype(F32)
        q = s / cnt - ext[pl.ds(HALO, tm), cs]
        y = _bdot(q.astype(BF16), wg_ref[g]) + bg_ref[:, cs]
        o_ref[:, cs] = (y * sc_ref[:, cs]).astype(BF16)


def pool_mix(p, w_g, b_g, scale, seq_len, tm=512):
    r, width = p.shape
    assert max(POOL_WINDOWS) // 2 <= HALO
    tm = _row_tile(seq_len, tm)
    ng, gw, _ = w_g.shape
    const2 = lambda i: (0, 0)
    return pl.pallas_call(
        functools.partial(_pool_kernel, tm=tm, seq_len=seq_len),
        grid=(r // tm,),
        in_specs=_halo_specs(tm, width, 0, lambda i: i, r // HALO) + [
            pl.BlockSpec((ng, gw, gw), lambda i: (0, 0, 0)),
            pl.BlockSpec((1, width), const2),
            pl.BlockSpec((1, width), const2),
        ],
        out_specs=pl.BlockSpec((tm, width), lambda i: (i, 0)),
        out_shape=jax.ShapeDtypeStruct((r, width), BF16),
        scratch_shapes=[pltpu.VMEM((tm + 2 * HALO, width), F32)],
        compiler_params=_cparams("parallel"),
        name="pool_mix",
    )(p, p, p, w_g.astype(BF16), b_g.reshape(1, width), scale.reshape(1, width))


def _hyfilter_kernel(z_ref, zt_ref, w1_ref, b1_ref, w2_ref, b2_ref, w3_ref, b3_ref, fr_ref, wo_ref, dl_ref,
                     o_ref, *, n_out, seq_len):
    z = z_ref[...]
    tm = z.shape[0]
    t = pl.program_id(0) * tm + lax.broadcasted_iota(jnp.int32, (tm, 1), 0)
    keep = (t != seq_len).astype(F32)
    h = jnp.sin(fr_ref[:, 0:1] * (_hdot(w1_ref[...], zt_ref[...]) + b1_ref[...]))
    h = jnp.sin(fr_ref[:, 1:2] * (_hdot(w2_ref[...], h) + b2_ref[...]))
    h = jnp.sin(fr_ref[:, 2:3] * (_hdot(w3_ref[...], h) + b3_ref[...]))
    window = jnp.exp(-z[:, 0:1] * dl_ref[...]) * keep
    width = dl_ref.shape[1]
    hb = h.astype(BF16)
    for q in range(n_out):
        cs = slice(q * width, (q + 1) * width)
        y = lax.dot_general(hb, wo_ref[:, cs], (((0,), (0,)), ((), ())), preferred_element_type=F32)
        o_ref[:, cs] = y * window


def hyena_filters(seq_len, width, f_w1, f_b1, f_w2, f_b2, f_w3, f_b3, f_freq, f_wout, tm=512):
    n = 2 * seq_len
    rows = np.arange(n)
    lag = np.where(rows < seq_len, rows, np.minimum(n - rows, seq_len - 1))
    t = (jnp.asarray(lag, dtype=F32) / (seq_len - 1))[:, None]
    w = 2.0 * math.pi * jnp.asarray(lag, dtype=F32)[:, None] / seq_len
    bands = jnp.linspace(1e-4, HYENA_BANDS - 1, HYENA_BANDS, dtype=F32)[None, :]
    z = jnp.concatenate([t, jnp.cos(bands * w), jnp.sin(-bands * w)], axis=-1)
    emb = z.shape[1]
    hid = f_w1.shape[1]
    zp = jnp.zeros((n, LANES), F32).at[:, :emb].set(z)
    order = f_wout.shape[1] // (2 * width)
    wo = f_wout.reshape(hid, order, 2, width).transpose(2, 0, 1, 3).reshape(2, hid, order * width)
    w1p = jnp.zeros((LANES, hid), F32).at[:emb].set(f_w1)
    max_decay = math.log(HYENA_DECAY_TARGET) / HYENA_FAST_DECAY
    min_decay = math.log(HYENA_DECAY_TARGET) / HYENA_SLOW_DECAY
    deltas = jnp.abs(jnp.linspace(min_decay, max_decay, width, dtype=F32)).reshape(1, width)
    n_tot = order * width
    tm = _row_tile(seq_len, tm)
    c2 = lambda i: (0, 0)
    return pl.pallas_call(
        functools.partial(_hyfilter_kernel, n_out=order, seq_len=seq_len),
        grid=(n // tm,),
        in_specs=[
            pl.BlockSpec((tm, LANES), lambda i: (i, 0)),
            pl.BlockSpec((LANES, tm), lambda i: (0, i)),
            pl.BlockSpec((hid, LANES), c2), pl.BlockSpec((hid, 1), c2),
            pl.BlockSpec((hid, hid), c2), pl.BlockSpec((hid, 1), c2),
            pl.BlockSpec((hid, hid), c2), pl.BlockSpec((hid, 1), c2),
            pl.BlockSpec((hid, 3), c2),
            pl.BlockSpec((None, hid, n_tot), lambda i: ((i * tm) // seq_len, 0, 0)),
            pl.BlockSpec((1, width), c2),
        ],
        out_specs=pl.BlockSpec((tm, n_tot), lambda i: (i, 0)),
        out_shape=jax.ShapeDtypeStruct((n, n_tot), F32),
        compiler_params=_cparams("parallel"),
        name="hyena_filters",
    )(zp, zp.T, w1p.T, f_b1.reshape(hid, 1), f_w2.T, f_b2.reshape(hid, 1), f_w3.T, f_b3.reshape(hid, 1),
      f_freq.T, wo.astype(BF16), deltas)


def _dft_tables(seq_len):
    n = 2 * seq_len
    p_ = FFT_P
    n1 = n // p_
    t1n = n1 // 2
    f1 = np.arange(n1)[:, None]
    t1 = np.arange(n1)[None, :]
    ang1 = 2.0 * np.pi * ((f1 * t1) % n1) / n1
    c1, s1 = np.cos(ang1), np.sin(ang1)
    w1 = np.zeros((2 * n1, 2 * t1n))
    w1[:n1, :t1n] = c1[:, :t1n]
    w1[:n1, t1n:] = s1[:, :t1n]
    w1[n1:, :t1n] = -s1[:, :t1n]
    w1[n1:, t1n:] = c1[:, :t1n]
    nh = n1 // 2 + 1
    nhp = -(-nh // SUBLANES) * SUBLANES
    w1k = np.zeros((2 * nhp, n1))
    w1k[:nh] = c1[:nh]
    w1k[nhp:nhp + nh] = -s1[:nh]
    w1i = np.zeros((2 * t1n, 2 * n1))
    ct, st = c1.T[:t1n] / n, s1.T[:t1n] / n
    w1i[:t1n, :n1] = ct
    w1i[:t1n, n1:] = -st
    w1i[t1n:, :n1] = st
    w1i[t1n:, n1:] = ct
    f2 = np.arange(p_)[:, None]
    pp = np.arange(p_)[None, :]
    ang2 = 2.0 * np.pi * ((f2 * pp) % p_) / p_
    c2, s2 = np.cos(ang2), np.sin(ang2)
    fb = np.block([[c2, s2], [-s2, c2]])
    fbi = np.block([[c2, -s2], [s2, c2]])
    fb = np.stack([fb, np.concatenate([fb[:p_][::-1], fb[p_:][::-1]], axis=0)])
    fbi = np.stack([fbi, np.concatenate([fbi[:, :p_][:, ::-1], fbi[:, p_:][:, ::-1]], axis=1)])
    angt = 2.0 * np.pi * ((np.arange(n1)[:, None] * np.arange(p_)[None, :]) % n) / n
    lane_bcast = lambda a: jnp.broadcast_to(jnp.asarray(a, dtype=F32)[:, :, None], (n1, p_, LANES))
    as_bf = lambda a: jnp.asarray(a, dtype=F32).astype(BF16)
    return dict(n1=n1, t1n=t1n, nh=nh, w1=as_bf(w1), w1k=as_bf(w1k), w1i=as_bf(w1i), fb=as_bf(fb), fbi=as_bf(fbi),
                twc=lane_bcast(np.cos(angt)), tws=lane_bcast(-np.sin(angt)))


DFT_PB = 16


def _rows_at(ref, pp):
    m, pb, _ = ref.shape
    return ref.reshape(m * pb, LANES)[pl.ds(pp, m, stride=pb), :]


def _set_rows_at(ref, pp, val):
    m, pb, _ = ref.shape
    ref.reshape(m * pb, LANES)[pl.ds(pp, m, stride=pb), :] = val


def _pack_complex(re, im):
    r = lax.bitcast_convert_type(re.astype(BF16).astype(F32), jnp.uint32)
    i = lax.bitcast_convert_type(im.astype(BF16).astype(F32), jnp.uint32)
    return r | (i >> 16)


def _unpack_complex(w):
    re = lax.bitcast_convert_type(w & jnp.uint32(0xFFFF0000), F32)
    im = lax.bitcast_convert_type(w << 16, F32)
    return re, im


def _dft1_kernel(w_ref, x_ref, o_ref):
    half = w_ref.shape[0] // 2
    for pp in range(x_ref.shape[1]):
        y = _bdot(w_ref[...], _rows_at(x_ref, pp).astype(BF16))
        _set_rows_at(o_ref, pp, _pack_complex(y[:half], y[half:]))


def dft_stage1(w, x4, idx):
    m, k = w.shape
    m = m // 2
    _, _, p_, c = x4.shape
    pb = min(DFT_PB, p_)
    return pl.pallas_call(
        _dft1_kernel,
        grid=(c // LANES, p_ // pb),
        in_specs=[pl.BlockSpec((2 * m, k), lambda j, q: (0, 0)),
                  pl.BlockSpec((None, k, pb, LANES), lambda j, q: (idx, 0, q, j))],
        out_specs=pl.BlockSpec((m, pb, LANES), lambda j, q: (0, q, j)),
        out_shape=jax.ShapeDtypeStruct((m, p_, c), jnp.uint32),
        compiler_params=_cparams("parallel", "parallel"),
        name="dft_stage1",
    )(w, x4)


def _dft3_kernel(w_ref, b_ref, g_ref, v_ref, bias_ref, o_ref):
    for pp in range(b_ref.shape[1]):
        re, im = _unpack_complex(_rows_at(b_ref, pp))
        y = _bdot(w_ref[...], jnp.concatenate([re, im], axis=0).astype(BF16))
        _set_rows_at(o_ref, pp, _rows_at(g_ref, pp) * (y + bias_ref[...] * _rows_at(v_ref, pp)))


def dft_inverse_stage1_gate(w, b3, gate4, gate_idx, v4, v_idx, bias):
    m, k = w.shape
    k = k // 2
    _, p_, c = b3.shape
    pb = min(DFT_PB, p_)
    lead = lambda idx: (lambda j, q: (idx, 0, q, j))
    return pl.pallas_call(
        _dft3_kernel,
        grid=(c // LANES, p_ // pb),
        in_specs=[pl.BlockSpec((m, 2 * k), lambda j, q: (0, 0)),
                  pl.BlockSpec((k, pb, LANES), lambda j, q: (0, q, j)),
                  pl.BlockSpec((None, m, pb, LANES), lead(gate_idx)),
                  pl.BlockSpec((None, m, pb, LANES), lead(v_idx)),
                  pl.BlockSpec((1, LANES), lambda j, q: (0, j))],
        out_specs=pl.BlockSpec((m, pb, LANES), lambda j, q: (0, q, j)),
        out_shape=jax.ShapeDtypeStruct((m, p_, c), F32),
        compiler_params=_cparams("parallel", "parallel"),
        name="dft_inverse_stage1_gate",
    )(w, b3, gate4, v4, bias.reshape(1, c).astype(F32))


def _twiddle(re, im, tc, ts, reps):
    tc = jnp.tile(tc, (1, reps))
    ts = jnp.tile(ts, (1, reps))
    return re * tc - im * ts, re * ts + im * tc


def _spec_fwd_kernel(a_ref, tc_ref, ts_ref, fb_ref, o_ref):
    reps = a_ref.shape[-1] // LANES
    re, im = _unpack_complex(a_ref[0])
    tr, ti = _twiddle(re, im, tc_ref[0], ts_ref[0], reps)
    rhs = jnp.concatenate([tr, ti], axis=0).astype(BF16)
    o_ref[0] = _bdot(fb_ref[...], rhs).astype(o_ref.dtype)


def _spec_conv_kernel(a_ref, tc_ref, ts_ref, fb_ref, fbi_ref, k_ref, o_ref, *, n1):
    p_ = FFT_P
    reps = a_ref.shape[-1] // LANES
    conj = jnp.where(pl.program_id(0) > n1 // 2, -1.0, 1.0).astype(F32)
    tc, ts = tc_ref[0], ts_ref[0]
    re, im = _unpack_complex(a_ref[0])
    tr, ti = _twiddle(re, im, tc, ts, reps)
    x = _bdot(fb_ref[...], jnp.concatenate([tr, ti], axis=0).astype(BF16))
    xr, xi = x[:p_], x[p_:]
    kr, ki = k_ref[0, :p_].astype(F32), conj * k_ref[0, p_:].astype(F32)
    yr = xr * kr - xi * ki
    yi = xr * ki + xi * kr
    bv = _bdot(fbi_ref[...], jnp.concatenate([yr, yi], axis=0).astype(BF16))
    orr, oi = _twiddle(bv[:p_], bv[p_:], tc, -ts, reps)
    o_ref[0] = _pack_complex(orr, oi)


def spectrum_forward(a3, tab, ct=1024):
    _, p_, c = a3.shape
    nh = tab["nh"]
    ct = min(ct, c)
    return pl.pallas_call(
        _spec_fwd_kernel,
        grid=(c // ct, nh),
        in_specs=[
            pl.BlockSpec((1, p_, ct), lambda j, f: (f, 0, j)),
            pl.BlockSpec((1, p_, LANES), lambda j, f: (f, 0, 0)),
            pl.BlockSpec((1, p_, LANES), lambda j, f: (f, 0, 0)),
            pl.BlockSpec((None, 2 * p_, 2 * p_), lambda j, f: (0, 0, 0)),
        ],
        out_specs=pl.BlockSpec((1, 2 * p_, ct), lambda j, f: (f, 0, j)),
        out_shape=jax.ShapeDtypeStruct((nh, 2 * p_, c), BF16),
        compiler_params=_cparams("parallel", "parallel"),
        name="dft_filter_stage2",
    )(a3, tab["twc"], tab["tws"], tab["fb"])


def spectrum_conv(a3, kspec, k_col_block, tab):
    n1, p_, c = a3.shape
    half = n1 // 2
    mirrored = lambda f: (f > half).astype(jnp.int32)
    return pl.pallas_call(
        functools.partial(_spec_conv_kernel, n1=n1),
        grid=(n1,),
        in_specs=[
            pl.BlockSpec((1, p_, c), lambda f: (f, 0, 0)),
            pl.BlockSpec((1, p_, LANES), lambda f: (f, 0, 0)),
            pl.BlockSpec((1, p_, LANES), lambda f: (f, 0, 0)),
            pl.BlockSpec((None, 2 * p_, 2 * p_), lambda f: (mirrored(f), 0, 0)),
            pl.BlockSpec((None, 2 * p_, 2 * p_), lambda f: (mirrored(f), 0, 0)),
            pl.BlockSpec((1, 2 * p_, c), lambda f: (jnp.where(f > half, n1 - f, f), 0, k_col_block)),
        ],
        out_specs=pl.BlockSpec((1, p_, c), lambda f: (f, 0, 0)),
        out_shape=jax.ShapeDtypeStruct(a3.shape, jnp.uint32),
        compiler_params=_cparams("arbitrary"),
        name="dft_stage2_conv",
    )(a3, tab["twc"], tab["tws"], tab["fb"], tab["fbi"], kspec)


def hyena_long_convs(xv, filt, bias, batch, seq_len):
    assert batch == 2, "the two sequences of the batch are packed as one complex sequence"
    c = xv.shape[-1]
    tab = _dft_tables(seq_len)
    n1, t1n, p_ = tab["n1"], tab["t1n"], FFT_P
    order = bias.shape[0]
    ak = dft_stage1(tab["w1k"], filt.reshape(1, n1, p_, order * c), 0)
    kspec = spectrum_forward(ak, tab)
    xv4 = xv.reshape(3, batch * t1n, p_, c)
    y4, y_idx = xv4, 2
    for o in range(order):
        a = dft_stage1(tab["w1"], y4, y_idx)
        b3 = spectrum_conv(a, kspec, o, tab)
        y = dft_inverse_stage1_gate(tab["w1i"], b3, xv4, o, y4, y_idx, bias[o])
        y4, y_idx = y[None], 0
    return y.reshape(batch * seq_len, c)


def kernel(x, c, ctx, c_ctx, ada_w, ada_b, norm_g, gmlp_w_in, gmlp_g_v, gmlp_w_s, gmlp_b_s, gmlp_w_out, lru_w_in, lru_conv_w, lru_conv_b, lru_w_a, lru_b_a, lru_w_x, lru_b_x, lru_lam, lru_w_out, hyena_w_in, hyena_conv_w, hyena_conv_b, hyena_f_w1, hyena_f_b1, hyena_f_w2, hyena_f_b2, hyena_f_w3, hyena_f_b3, hyena_f_freq, hyena_f_wout, hyena_bias, hyena_w_out, pool_w_in, pool_w_g, pool_b_g, pool_scale, pool_w_out, ffn_w_gate, ffn_w_up, ffn_w_down, moe_w_router, moe_w_gate, moe_w_up, moe_w_down):
    B, L, D = x.shape
    Lc = ctx.shape[1]
    depth = ada_w.shape[0]
    n_mixers = 4
    assert B + 1 <= SUBLANES

    cc = jnp.zeros((SUBLANES, D), F32).at[:B].set(c).at[B].set(c_ctx)
    ada = ada_all(cc, ada_w, ada_b).reshape(depth, SUBLANES, 6, D)

    xs = add_pos(x)
    xc = ctx.reshape(B * Lc, D)
    last_ctx = max([i for i in range(depth) if i % n_mixers == 1], default=-1)

    for i in range(depth):
        kind, j = i % n_mixers, i // n_mixers
        update_ctx = i < last_ctx
        read_ctx = i <= last_ctx
        lat = [ada[i, :B, q].reshape(B, 1, D) for q in range(6)]
        cx = [jnp.broadcast_to(ada[i, B:B + 1, q].reshape(1, 1, D), (B, 1, D)) for q in range(6)]
        g = norm_g[i]
        streams = [(xs, lat, L)]
        if read_ctx:
            streams.append((xc, cx, Lc))

        if kind == 0:
            outs = []
            for (s, m, sl) in streams[:1 + int(update_ctx)]:
                hw = norm_mod_matmul(s, g[0], m[0], m[1], gmlp_w_in[j], sl, BF16, act="gelu")
                a = gmlp_spatial(hw, gmlp_g_v[j], gmlp_w_s[j], gmlp_b_s[j])
                outs.append(matmul_norm_res([(a, a.shape[1], 0)], _identity_bf16, gmlp_w_out[j],
                                            g[1], m[2], s, sl))
        elif kind == 1:
            W = lru_w_in.shape[2] // 2
            sc_args = lambda d: (lru_conv_w[j], lru_conv_b[j], lru_w_a[j, d], lru_b_a[j, d],
                                 lru_w_x[j, d], lru_b_x[j, d], lru_lam[j, d])
            zc = norm_mod_matmul(xc, g[0], cx[0], cx[1], lru_w_in[j], Lc, F32)
            zero = jnp.zeros((B, W), F32)
            hf_c = lru_scan(zc, 1, W, *sc_args(0), zero, Lc, False)
            hb_c = lru_scan(zc, 1, W, *sc_args(1), zero, Lc, True)
            zl = norm_mod_matmul(xs, g[0], lat[0], lat[1], lru_w_in[j], L, F32)
            hf = lru_scan(zl, 1, W, *sc_args(0), hf_c.reshape(B, Lc, W)[:, -1], L, False, out_dtype=BF16)
            hb = lru_scan(zl, 1, W, *sc_args(1), hb_c.reshape(B, Lc, W)[:, 0], L, True, out_dtype=BF16)
            outs = [matmul_norm_res([(zl, W, 0), (hf, W, 0), (hb, W, 0)], _lru_out_prologue,
                                    lru_w_out[j], g[1], lat[2], xs, L)]
            if update_ctx:
                outs.append(matmul_norm_res([(zc, W, 0), (hf_c, W, 0), (hb_c, W, 0)], _lru_out_prologue,
                                            lru_w_out[j], g[1], cx[2], xc, Lc))
        elif kind == 2:
            W = hyena_w_out.shape[1]
            filt = hyena_filters(L, W, hyena_f_w1[j], hyena_f_b1[j], hyena_f_w2[j], hyena_f_b2[j],
                                 hyena_f_w3[j], hyena_f_b3[j], hyena_f_freq[j], hyena_f_wout[j])
            outs = []
            for (s, m, sl) in streams[:1 + int(update_ctx)]:
                nb = s.shape[0] // sl
                xv = norm_mod_matmul_conv(s, g[0], m[0], m[1], hyena_w_in[j], hyena_conv_w[j],
                                          hyena_conv_b[j], W, sl)
                fl = filt if sl == L else hyena_filters(
                    sl, W, hyena_f_w1[j], hyena_f_b1[j], hyena_f_w2[j], hyena_f_b2[j],
                    hyena_f_w3[j], hyena_f_b3[j], hyena_f_freq[j], hyena_f_wout[j])
                y = hyena_long_convs(xv, fl, hyena_bias[j], nb, sl)
                outs.append(matmul_norm_res([(y, W, 0)], _identity_bf16, hyena_w_out[j], g[1], m[2], s, sl))
        else:
            outs = []
            for (s, m, sl) in streams[:1 + int(update_ctx)]:
                p = norm_mod_matmul(s, g[0], m[0], m[1], pool_w_in[j], sl, F32)
                a = pool_mix(p, pool_w_g[j], pool_b_g[j], pool_scale[j], sl)
                outs.append(matmul_norm_res([(a, a.shape[1], 0)], _identity_bf16, pool_w_out[j],
                                            g[1], m[2], s, sl))
        xs = outs[0]
        if update_ctx:
            xc = outs[1]

        k = i // 2
        todo = [(xs, lat, L)] + ([(xc, cx, Lc)] if update_ctx else [])
        res = []
        for (s, m, sl) in todo:
            if i % 2 == 0:
                res.append(ffn_dense(s, g[2], m[3], m[4], ffn_w_gate[k], ffn_w_up[k], ffn_w_down[k],
                                     g[3], m[5], sl))
            else:
                res.append(moe_block(s, g[2], m[3], m[4], moe_w_router[k], moe_w_gate, moe_w_up,
                                     moe_w_down, k, g[3], m[5], sl))
        xs = res[0]
        if update_ctx:
            xc = res[1]
    return xs.reshape(B, L, D)
```

```python
import functools
import math

import jax
import jax.numpy as jnp
import numpy as np
from jax import lax
from jax.experimental import pallas as pl
from jax.experimental.pallas import tpu as pltpu

F32 = jnp.float32
BF16 = jnp.bfloat16
EPS = 1e-6

VMEM_LIMIT_BYTES = 52 * 1024 * 1024
LANES = 128
SUBLANES = 8

GRID_W = 64
GMLP_CHUNK = 128
GMLP_GROUPS = 8
LRU_HEADS = 8
LRU_C = 8.0
POOL_WINDOWS = (2, 4, 8, 16)
HYENA_BANDS = 16
HYENA_FAST_DECAY = 0.3
HYENA_SLOW_DECAY = 1.5
HYENA_DECAY_TARGET = 1e-2
N_EXPERTS = 8
TOP_K = 2
FFT_P = 128
HALO = SUBLANES


def _cparams(*sem):
    return pltpu.CompilerParams(dimension_semantics=sem, vmem_limit_bytes=VMEM_LIMIT_BYTES)


def _rms(x, g):
    return x * lax.rsqrt(jnp.mean(x * x, axis=-1, keepdims=True) + EPS) * g


def _gelu(x):
    return 0.5 * x * (1.0 + jnp.tanh(math.sqrt(2.0 / math.pi) * (x + 0.044715 * (x * x * x))))


def _silu(x):
    return x * (1.0 / (1.0 + jnp.exp(-x)))


def _sigmoid(x):
    return 1.0 / (1.0 + jnp.exp(-x))


def _bdot(a, b):
    return jnp.dot(a, b, preferred_element_type=F32)


def _hdot(a, b):
    return jnp.dot(a, b, preferred_element_type=F32, precision=lax.Precision.HIGHEST)


def _to_row_tiles(val, ref):
    rows = val.shape[0]
    nq = val.shape[1] // LANES
    for q in range(nq):
        ref[pl.ds(q, rows, stride=nq), :] = val[:, q * LANES:(q + 1) * LANES].astype(ref.dtype)


def _from_row_tiles(ref, dst_ref):
    rows = dst_ref.shape[0]
    nq = dst_ref.shape[1] // LANES
    for q in range(nq):
        dst_ref[:, q * LANES:(q + 1) * LANES] = ref[pl.ds(q, rows, stride=nq), :].astype(dst_ref.dtype)


def _row_tile(rows, want):
    t = min(rows, want)
    assert rows % t == 0, (rows, t)
    return t


def _ada_kernel(c_ref, w_ref, b_ref, o_ref):
    o_ref[0] = _hdot(_silu(c_ref[...]), w_ref[0]) + b_ref[0]


def ada_all(cc, ada_w, ada_b):
    depth, d, d6 = ada_w.shape
    nchunk = d6 // d
    return pl.pallas_call(
        _ada_kernel,
        grid=(depth, nchunk),
        in_specs=[
            pl.BlockSpec((SUBLANES, d), lambda i, j: (0, 0)),
            pl.BlockSpec((1, d, d), lambda i, j: (i, 0, j)),
            pl.BlockSpec((1, 1, d), lambda i, j: (i, 0, j)),
        ],
        out_specs=pl.BlockSpec((1, SUBLANES, d), lambda i, j: (i, 0, j)),
        out_shape=jax.ShapeDtypeStruct((depth, SUBLANES, d6), F32),
        compiler_params=_cparams("parallel", "parallel"),
        name="ada",
    )(cc, ada_w, ada_b.reshape(depth, 1, d6))


def _pos_kernel(x_ref, rt_ref, ct_ref, o_ref):
    half = rt_ref.shape[-1]
    x = x_ref[...]
    o_ref[:, :, :half] = x[:, :, :half] + rt_ref[...]
    o_ref[:, :, half:] = x[:, :, half:] + ct_ref[...][None]


def add_pos(x):
    b, l, d = x.shape
    rows = l // GRID_W
    quarter = d // 4
    omega = 1.0 / (10000.0 ** (jnp.arange(quarter, dtype=F32) / quarter))

    def sincos(p):
        ang = p.reshape(-1, 1) * omega[None, :]
        return jnp.concatenate([jnp.sin(ang), jnp.cos(ang)], axis=-1)

    rtab = sincos(jnp.arange(rows, dtype=F32)).reshape(rows, 1, 2 * quarter)
    ctab = sincos(jnp.arange(GRID_W, dtype=F32))
    x3 = x.reshape(b * rows, GRID_W, d)
    tr = _row_tile(rows, 16)
    nrt = rows // tr
    out = pl.pallas_call(
        _pos_kernel,
        grid=(b * nrt,),
        in_specs=[
            pl.BlockSpec((tr, GRID_W, d), lambda i: (i, 0, 0)),
            pl.BlockSpec((tr, 1, 2 * quarter), lambda i: (i % nrt, 0, 0)),
            pl.BlockSpec((GRID_W, 2 * quarter), lambda i: (0, 0)),
        ],
        out_specs=pl.BlockSpec((tr, GRID_W, d), lambda i: (i, 0, 0)),
        out_shape=jax.ShapeDtypeStruct(x3.shape, F32),
        compiler_params=_cparams("parallel"),
        name="add_pos",
    )(x3, rtab, ctab)
    return out.reshape(b * l, d)


def _nmm_kernel(x_ref, g_ref, sh_ref, sc_ref, w_ref, o_ref, h_scr, *, act):
    @pl.when(pl.program_id(1) == 0)
    def _():
        h = _rms(x_ref[...], g_ref[...]) * (1.0 + sc_ref[0]) + sh_ref[0]
        h_scr[...] = h.astype(BF16)

    y = _bdot(h_scr[...], w_ref[...])
    if act == "gelu":
        y = _gelu(y)
    o_ref[...] = y.astype(o_ref.dtype)


def norm_mod_matmul(x, g, shift, scale, w, rows_per_mod, out_dtype, act=None, tm=512, tn=2048):
    r, d = x.shape
    n = w.shape[1]
    tm = _row_tile(min(r, rows_per_mod), tm)
    tn = min(tn, n)
    assert n % tn == 0
    mod_map = lambda i, j: ((i * tm) // rows_per_mod, 0, 0)
    return pl.pallas_call(
        functools.partial(_nmm_kernel, act=act),
        grid=(r // tm, n // tn),
        in_specs=[
            pl.BlockSpec((tm, d), lambda i, j: (i, 0)),
            pl.BlockSpec((1, d), lambda i, j: (0, 0)),
            pl.BlockSpec((1, 1, d), mod_map),
            pl.BlockSpec((1, 1, d), mod_map),
            pl.BlockSpec((d, tn), lambda i, j: (0, j)),
        ],
        out_specs=pl.BlockSpec((tm, tn), lambda i, j: (i, j)),
        out_shape=jax.ShapeDtypeStruct((r, n), out_dtype),
        scratch_shapes=[pltpu.VMEM((tm, d), BF16)],
        compiler_params=_cparams("parallel", "arbitrary"),
        name="norm_mod_matmul",
    )(x, g.reshape(1, d), shift, scale, w.astype(BF16))


HALO_BF16 = 2 * SUBLANES


def _nmm_conv_kernel(x_ref, xp_ref, xn_ref, g_ref, sh_ref, sc_ref, w_ref, cw_ref, cb_ref, o_ref,
                     h_scr, z_scr, *, tm, seq_len):
    i = pl.program_id(0)
    hl = HALO_BF16

    @pl.when(pl.program_id(1) == 0)
    def _():
        def nm(x):
            return (_rms(x, g_ref[...]) * (1.0 + sc_ref[0]) + sh_ref[0]).astype(BF16)
        h_scr[0:hl, :] = nm(xp_ref[...])
        h_scr[hl:hl + tm, :] = nm(x_ref[...])
        h_scr[hl + tm:hl + tm + hl, :] = nm(xn_ref[...])

    z = _bdot(h_scr[...], w_ref[...])
    first = (i * tm) % seq_len == 0
    last = ((i + 1) * tm) % seq_len == 0
    row = lax.broadcasted_iota(jnp.int32, (tm + 2 * hl, 1), 0)
    outside = jnp.logical_or(jnp.logical_and(first, row < hl), jnp.logical_and(last, row >= hl + tm))
    z_scr[...] = jnp.where(outside, 0.0, z)
    kk = cw_ref.shape[0]
    left = kk // 2
    y = cb_ref[...] + cw_ref[0:1, :] * z_scr[pl.ds(hl - left, tm), :]
    for k in range(1, kk):
        y = y + cw_ref[k:k + 1, :] * z_scr[pl.ds(hl - left + k, tm), :]
    o_ref[0] = y


def norm_mod_matmul_conv(x, g, shift, scale, w, conv_w, conv_b, width, seq_len, tm=512):
    r, d = x.shape
    n = w.shape[1]
    nsplit = n // width
    tm = _row_tile(seq_len, tm)
    hl = HALO_BF16
    per = tm // hl
    n_row_blocks = r // hl
    kk = conv_w.shape[0]
    mod_map = lambda i, j: ((i * tm) // seq_len, 0, 0)
    return pl.pallas_call(
        functools.partial(_nmm_conv_kernel, tm=tm, seq_len=seq_len),
        grid=(r // tm, nsplit),
        in_specs=[
            pl.BlockSpec((tm, d), lambda i, j: (i, 0)),
            pl.BlockSpec((hl, d), lambda i, j: (jnp.maximum(i * per - 1, 0), 0)),
            pl.BlockSpec((hl, d), lambda i, j: (jnp.minimum((i + 1) * per, n_row_blocks - 1), 0)),
            pl.BlockSpec((1, d), lambda i, j: (0, 0)),
            pl.BlockSpec((1, 1, d), mod_map),
            pl.BlockSpec((1, 1, d), mod_map),
            pl.BlockSpec((d, width), lambda i, j: (0, j)),
            pl.BlockSpec((kk, width), lambda i, j: (0, j)),
            pl.BlockSpec((1, width), lambda i, j: (0, j)),
        ],
        out_specs=pl.BlockSpec((1, tm, width), lambda i, j: (j, i, 0)),
        out_shape=jax.ShapeDtypeStruct((nsplit, r, width), F32),
        scratch_shapes=[pltpu.VMEM((tm + 2 * hl, d), BF16), pltpu.VMEM((tm + 2 * hl, width), F32)],
        compiler_params=_cparams("parallel", "arbitrary"),
        name="norm_mod_matmul_conv",
    )(x, x, x, g.reshape(1, d), shift, scale, w.astype(BF16), conv_w, conv_b.reshape(1, n))


def _mnr_kernel(*refs, n_in, prologue):
    in_refs = refs[:n_in]
    w_ref, g_ref, gate_ref, x_ref, o_ref = refs[n_in:]
    a = prologue(*[r[...] for r in in_refs])
    y = _bdot(a, w_ref[...])
    o_ref[...] = x_ref[...] + gate_ref[0] * _rms(y, g_ref[...])


def matmul_norm_res(inputs, prologue, w, g, gate, x, rows_per_mod, tm=512):
    r, d = x.shape
    k = w.shape[0]
    tm = _row_tile(min(r, rows_per_mod), tm)
    mod_map = lambda i: ((i * tm) // rows_per_mod, 0, 0)
    in_specs = [pl.BlockSpec((tm, wd), functools.partial(lambda i, cb: (i, cb), cb=cb))
                for (_, wd, cb) in inputs]
    in_specs += [
        pl.BlockSpec((k, d), lambda i: (0, 0)),
        pl.BlockSpec((1, d), lambda i: (0, 0)),
        pl.BlockSpec((1, 1, d), mod_map),
        pl.BlockSpec((tm, d), lambda i: (i, 0)),
    ]
    return pl.pallas_call(
        functools.partial(_mnr_kernel, n_in=len(inputs), prologue=prologue),
        grid=(r // tm,),
        in_specs=in_specs,
        out_specs=pl.BlockSpec((tm, d), lambda i: (i, 0)),
        out_shape=jax.ShapeDtypeStruct((r, d), F32),
        compiler_params=_cparams("parallel"),
        name="matmul_norm_res",
    )(*[a for (a, _, _) in inputs], w.astype(BF16), g.reshape(1, d), gate, x)


def _identity_bf16(a):
    return a.astype(BF16)


def _swiglu_step(h_scr, wg_ref, wu_ref, wd_ref, acc_scr, j):
    h = h_scr[...]
    t = (_silu(_bdot(h, wg_ref[...].astype(BF16))) * _bdot(h, wu_ref[...].astype(BF16))).astype(BF16)
    part = _bdot(t, wd_ref[...].astype(BF16))

    @pl.when(j == 0)
    def _():
        acc_scr[...] = part

    @pl.when(j > 0)
    def _():
        acc_scr[...] += part


def _ffn_dense_kernel(x_ref, g1_ref, sh_ref, sc_ref, wg_ref, wu_ref, wd_ref, g2_ref, gate_ref,
                      o_ref, h_scr, acc_scr):
    j = pl.program_id(1)

    @pl.when(j == 0)
    def _():
        h = _rms(x_ref[...], g1_ref[...]) * (1.0 + sc_ref[0]) + sh_ref[0]
        h_scr[...] = h.astype(BF16)

    _swiglu_step(h_scr, wg_ref, wu_ref, wd_ref, acc_scr, j)

    @pl.when(j == pl.num_programs(1) - 1)
    def _():
        o_ref[...] = x_ref[...] + gate_ref[0] * _rms(acc_scr[...], g2_ref[...])


def ffn_dense(x, g1, shift, scale, wg, wu, wd, g2, gate, rows_per_mod, tm=512, tf=1408):
    r, d = x.shape
    ff = wg.shape[1]
    tm = _row_tile(min(r, rows_per_mod), tm)
    assert ff % tf == 0
    mod_map = lambda i, j: ((i * tm) // rows_per_mod, 0, 0)
    return pl.pallas_call(
        _ffn_dense_kernel,
        grid=(r // tm, ff // tf),
        in_specs=[
            pl.BlockSpec((tm, d), lambda i, j: (i, 0)),
            pl.BlockSpec((1, d), lambda i, j: (0, 0)),
            pl.BlockSpec((1, 1, d), mod_map),
            pl.BlockSpec((1, 1, d), mod_map),
            pl.BlockSpec((d, tf), lambda i, j: (0, j)),
            pl.BlockSpec((d, tf), lambda i, j: (0, j)),
            pl.BlockSpec((tf, d), lambda i, j: (j, 0)),
            pl.BlockSpec((1, d), lambda i, j: (0, 0)),
            pl.BlockSpec((1, 1, d), mod_map),
        ],
        out_specs=pl.BlockSpec((tm, d), lambda i, j: (i, 0)),
        out_shape=jax.ShapeDtypeStruct((r, d), F32),
        scratch_shapes=[pltpu.VMEM((tm, d), BF16), pltpu.VMEM((tm, d), F32)],
        compiler_params=_cparams("parallel", "arbitrary"),
        name="ffn_dense",
    )(x, g1.reshape(1, d), shift, scale, wg.astype(BF16), wu.astype(BF16), wd.astype(BF16),
      g2.reshape(1, d), gate)


def _ffn_routed_kernel(te_ref, nu_ref, idx0_ref, idxn_ref, h_hbm, wg_ref, wu_ref, wd_ref, o_ref,
                       xbuf, sems, h_scr, acc_scr, *, tm, nq, issue_steps):
    i = pl.program_id(0)
    j = pl.program_id(1)
    last = pl.num_programs(1) - 1
    used = i < nu_ref[0]
    slot = i % 2
    per = tm // issue_steps

    def row_copy(idx_ref, r, s):
        s0 = pl.multiple_of(idx_ref[r] * nq, nq)
        d0 = pl.multiple_of(r * nq, nq)
        return pltpu.make_async_copy(h_hbm.at[pl.ds(s0, nq)], xbuf.at[s, pl.ds(d0, nq)], sems.at[s])

    @pl.when(used)
    def _():
        @pl.when(jnp.logical_and(i == 0, j == 0))
        def _():
            def issue(r, carry):
                row_copy(idx0_ref, r, 0).start()
                return carry
            lax.fori_loop(0, tm, issue, 0)

        @pl.when(j == 0)
        def _():
            pltpu.make_async_copy(h_hbm.at[pl.ds(0, tm * nq)], xbuf.at[slot], sems.at[slot]).wait()
            _from_row_tiles(xbuf.at[slot], h_scr)

        @pl.when(jnp.logical_and(i + 1 < nu_ref[0], j < issue_steps))
        def _():
            for k in range(per):
                row_copy(idxn_ref, j * per + k, 1 - slot).start(priority=k % 2)

        _swiglu_step(h_scr, wg_ref.at[0], wu_ref.at[0], wd_ref.at[0], acc_scr, j)

        @pl.when(j == last)
        def _():
            _to_row_tiles(acc_scr[...], o_ref)

    @pl.when(jnp.logical_and(jnp.logical_not(used), j == last))
    def _():
        o_ref[...] = jnp.zeros_like(o_ref)


def ffn_routed(h_rt, src, tile_expert, n_used, wg, wu, wd, tm, tf=512):
    dq = wg.shape[1] // LANES
    d = dq * LANES
    p = src.shape[0]
    ff = wg.shape[2]
    assert ff % tf == 0 and p % tm == 0
    n_tiles = p // tm
    issue_steps = min(4, ff // tf)
    assert tm % issue_steps == 0
    grid_spec = pltpu.PrefetchScalarGridSpec(
        num_scalar_prefetch=2,
        grid=(n_tiles, ff // tf),
        in_specs=[
            pl.BlockSpec((tm,), lambda i, j, te, nu: (0,), memory_space=pltpu.SMEM),
            pl.BlockSpec((tm,), lambda i, j, te, nu: (jnp.minimum(i + 1, n_tiles - 1),),
                         memory_space=pltpu.SMEM),
            pl.BlockSpec(memory_space=pl.ANY),
            pl.BlockSpec((1, d, tf), lambda i, j, te, nu: (te[i], 0, j)),
            pl.BlockSpec((1, d, tf), lambda i, j, te, nu: (te[i], 0, j)),
            pl.BlockSpec((1, tf, d), lambda i, j, te, nu: (te[i], j, 0)),
        ],
        out_specs=pl.BlockSpec((tm * dq, LANES), lambda i, j, te, nu: (i, 0)),
        scratch_shapes=[pltpu.VMEM((2, tm * dq, LANES), F32), pltpu.SemaphoreType.DMA((2,)),
                        pltpu.VMEM((tm, d), BF16), pltpu.VMEM((tm, d), F32)],
    )
    return pl.pallas_call(
        functools.partial(_ffn_routed_kernel, tm=tm, nq=dq, issue_steps=issue_steps),
        grid_spec=grid_spec,
        out_shape=jax.ShapeDtypeStruct((p * dq, LANES), F32),
        compiler_params=_cparams("arbitrary", "arbitrary"),
        name="ffn_routed",
    )(tile_expert, n_used, src, src, h_rt, wg, wu, wd)


ROUTE_IDX0 = N_EXPERTS
ROUTE_P0 = N_EXPERTS + TOP_K


def _router_kernel(x_ref, g_ref, sh_ref, sc_ref, wr_ref, h_ref, r_ref):
    h = _rms(x_ref[...], g_ref[...]) * (1.0 + sc_ref[0]) + sh_ref[0]
    _to_row_tiles(h, h_ref)
    h_hi = h.astype(BF16)
    h_lo = (h - h_hi.astype(F32)).astype(BF16)
    logits = _bdot(h_hi, wr_ref[0]) + (_bdot(h_lo, wr_ref[0]) + _bdot(h_hi, wr_ref[1]))
    lane = lax.broadcasted_iota(jnp.int32, logits.shape, 1)
    neg = jnp.float32(-jnp.inf)
    big = jnp.int32(LANES)
    lg = jnp.where(lane < N_EXPERTS, logits, neg)
    m1 = jnp.max(lg, axis=-1, keepdims=True)
    i1 = jnp.min(jnp.where(lg == m1, lane, big), axis=-1, keepdims=True)
    lg2 = jnp.where(lane == i1, neg, lg)
    m2 = jnp.max(lg2, axis=-1, keepdims=True)
    i2 = jnp.min(jnp.where(lg2 == m2, lane, big), axis=-1, keepdims=True)
    e2 = jnp.exp(m2 - m1)
    p1 = 1.0 / (1.0 + e2)
    p2 = e2 / (1.0 + e2)
    out = jnp.where(lane == ROUTE_IDX0, i1.astype(F32), 0.0)
    out = jnp.where(lane == ROUTE_IDX0 + 1, i2.astype(F32), out)
    out = jnp.where(lane == ROUTE_P0, p1, out)
    out = jnp.where(lane == ROUTE_P0 + 1, p2, out)
    r_ref[...] = out


def router(x, g, shift, scale, w_router, rows_per_mod, tm=512):
    r, d = x.shape
    tm = _row_tile(min(r, rows_per_mod), tm)
    wr = jnp.zeros((d, LANES), F32).at[:, :N_EXPERTS].set(w_router)
    wr_hi = wr.astype(BF16)
    wr = jnp.stack([wr_hi, (wr - wr_hi.astype(F32)).astype(BF16)])
    mod_map = lambda i: ((i * tm) // rows_per_mod, 0, 0)
    return pl.pallas_call(
        _router_kernel,
        grid=(r // tm,),
        in_specs=[
            pl.BlockSpec((tm, d), lambda i: (i, 0)),
            pl.BlockSpec((1, d), lambda i: (0, 0)),
            pl.BlockSpec((1, 1, d), mod_map),
            pl.BlockSpec((1, 1, d), mod_map),
            pl.BlockSpec((2, d, LANES), lambda i: (0, 0, 0)),
        ],
        out_specs=[pl.BlockSpec((tm * (d // LANES), LANES), lambda i: (i, 0)),
                   pl.BlockSpec((tm, LANES), lambda i: (i, 0))],
        out_shape=[jax.ShapeDtypeStruct((r * (d // LANES), LANES), F32), jax.ShapeDtypeStruct((r, LANES), F32)],
        compiler_params=_cparams("parallel"),
        name="router",
    )(x, g.reshape(1, d), shift, scale, wr)


GATHER_UNROLL = 8


def _combine_kernel(f0_ref, f1_ref, n0_ref, n1_ref, ys_hbm, r_ref, g_ref, gate_ref, x_ref, o_ref,
                    ybuf, sems, y_scr, *, tm, nq):
    i = pl.program_id(0)
    nt = pl.num_programs(0)
    slot = i % 2

    def issue_tile(idx_refs, s):
        def issue(rb, carry):
            for k in range(GATHER_UNROLL):
                r = GATHER_UNROLL * rb + k
                d0 = pl.multiple_of(r * nq, nq)
                for c, idx_ref in enumerate(idx_refs):
                    s0 = pl.multiple_of(idx_ref[r] * nq, nq)
                    pltpu.make_async_copy(ys_hbm.at[pl.ds(s0, nq)], ybuf.at[s, c, pl.ds(d0, nq)],
                                          sems.at[s]).start(priority=c)
            return carry
        lax.fori_loop(0, tm // GATHER_UNROLL, issue, 0)

    @pl.when(i == 0)
    def _():
        issue_tile((f0_ref, f1_ref), 0)

    @pl.when(i + 1 < nt)
    def _():
        issue_tile((n0_ref, n1_ref), 1 - slot)

    for c in range(TOP_K):
        pltpu.make_async_copy(ys_hbm.at[pl.ds(0, tm * nq)], ybuf.at[slot, c], sems.at[slot]).wait()
    rt = r_ref[...]
    p0 = rt[:, ROUTE_P0:ROUTE_P0 + 1]
    p1 = rt[:, ROUTE_P0 + 1:ROUTE_P0 + 2]
    _from_row_tiles(ybuf.at[slot, 0], y_scr)
    y = p0 * y_scr[...]
    _from_row_tiles(ybuf.at[slot, 1], y_scr)
    y = y + p1 * y_scr[...]
    o_ref[...] = x_ref[...] + gate_ref[0] * _rms(y, g_ref[...])


def combine(ys, slot0, slot1, route, g, gate, x, rows_per_mod, tm=512):
    r, d = x.shape
    nq = d // LANES
    tm = _row_tile(min(r, rows_per_mod), tm)
    nt = r // tm
    mod_map = lambda i: ((i * tm) // rows_per_mod, 0, 0)
    first = pl.BlockSpec((tm,), lambda i: (0,), memory_space=pltpu.SMEM)
    nxt = pl.BlockSpec((tm,), lambda i: (jnp.minimum(i + 1, nt - 1),), memory_space=pltpu.SMEM)
    return pl.pallas_call(
        functools.partial(_combine_kernel, tm=tm, nq=nq),
        grid=(nt,),
        in_specs=[
            first, first, nxt, nxt,
            pl.BlockSpec(memory_space=pl.ANY),
            pl.BlockSpec((tm, LANES), lambda i: (i, 0)),
            pl.BlockSpec((1, d), lambda i: (0, 0)),
            pl.BlockSpec((1, 1, d), mod_map),
            pl.BlockSpec((tm, d), lambda i: (i, 0)),
        ],
        out_specs=pl.BlockSpec((tm, d), lambda i: (i, 0)),
        out_shape=jax.ShapeDtypeStruct((r, d), F32),
        scratch_shapes=[pltpu.VMEM((2, 2, tm * nq, LANES), F32), pltpu.SemaphoreType.DMA((2,)),
                        pltpu.VMEM((tm, d), F32)],
        compiler_params=_cparams("arbitrary"),
        name="moe_combine",
    )(slot0, slot1, slot0, slot1, ys, route, g.reshape(1, d), gate, x)


def moe_block(x, g1, shift, scale, w_router, wg_all, wu_all, wd_all, layer, g2, gate, rows_per_mod, tm=1024):
    n, d = x.shape
    ff = wg_all.shape[-1]
    wg = wg_all.reshape(-1, d, ff)
    wu = wu_all.reshape(-1, d, ff)
    wd = wd_all.reshape(-1, ff, d)
    h, route = router(x, g1, shift, scale, w_router, rows_per_mod)
    eidx = route[:, ROUTE_IDX0:ROUTE_IDX0 + TOP_K].astype(jnp.int32)
    flat_e = eidx.T.reshape(-1)
    onehot = (flat_e[:, None] == jnp.arange(N_EXPERTS, dtype=jnp.int32)[None, :]).astype(jnp.int32)
    csum = jnp.cumsum(onehot, axis=0)
    counts = csum[-1]
    rank = jnp.take_along_axis(csum, flat_e[:, None], axis=1)[:, 0] - 1
    tiles_per_e = (counts + tm - 1) // tm
    tile_end = jnp.cumsum(tiles_per_e)
    tile_start = tile_end - tiles_per_e
    slot = tile_start[flat_e] * tm + rank
    n_tiles = (TOP_K * n) // tm + N_EXPERTS
    tok = jnp.tile(jnp.arange(n, dtype=jnp.int32), TOP_K)
    src = jnp.zeros((n_tiles * tm,), jnp.int32).at[slot].set(tok)
    tile_ids = jnp.arange(n_tiles, dtype=jnp.int32)
    tile_expert = jnp.minimum(jnp.sum((tile_ids[:, None] >= tile_end[None, :]).astype(jnp.int32), axis=1),
                              N_EXPERTS - 1).astype(jnp.int32)
    n_used = tile_end[-1:].astype(jnp.int32)

    ys = ffn_routed(h, src, tile_expert + layer * N_EXPERTS, n_used, wg, wu, wd, tm)
    slot = slot.astype(jnp.int32)
    return combine(ys, slot[:n], slot[n:], route, g2, gate, x, rows_per_mod)


def _gmlp_kernel(u_ref, v_ref, gv_ref, ws_ref, bs_ref, o_ref, vn_scr):
    v = v_ref[...].astype(F32)
    mu = jnp.mean(v, axis=-1, keepdims=True)
    vc = v - mu
    vn = vc * lax.rsqrt(jnp.mean(vc * vc, axis=-1, keepdims=True) + EPS) * gv_ref[...]
    vn_scr[...] = vn.astype(BF16)
    tm, width = vn_scr.shape
    gw = width // GMLP_GROUPS
    for n in range(tm // GMLP_CHUNK):
        rs = slice(n * GMLP_CHUNK, (n + 1) * GMLP_CHUNK)
        for g in range(GMLP_GROUPS):
            cs = slice(g * gw, (g + 1) * gw)
            m = _bdot(ws_ref[g], vn_scr[rs, cs]) + bs_ref[g]
            o_ref[rs, cs] = (u_ref[rs, cs].astype(F32) * m).astype(BF16)


def gmlp_spatial(hw, g_v, w_s, b_s, tm=512):
    r, w2 = hw.shape
    width = w2 // 2
    gw = width // GMLP_GROUPS
    tm = _row_tile(r, tm)
    bsb = jnp.broadcast_to(b_s[:, :, None], (GMLP_GROUPS, GMLP_CHUNK, gw)).astype(F32)
    return pl.pallas_call(
        _gmlp_kernel,
        grid=(r // tm,),
        in_specs=[
            pl.BlockSpec((tm, width), lambda i: (i, 0)),
            pl.BlockSpec((tm, width), lambda i: (i, 1)),
            pl.BlockSpec((1, width), lambda i: (0, 0)),
            pl.BlockSpec((GMLP_GROUPS, GMLP_CHUNK, GMLP_CHUNK), lambda i: (0, 0, 0)),
            pl.BlockSpec((GMLP_GROUPS, GMLP_CHUNK, gw), lambda i: (0, 0, 0)),
        ],
        out_specs=pl.BlockSpec((tm, width), lambda i: (i, 0)),
        out_shape=jax.ShapeDtypeStruct((r, width), BF16),
        scratch_shapes=[pltpu.VMEM((tm, width), BF16)],
        compiler_params=_cparams("parallel"),
        name="gmlp_spatial",
    )(hw, hw, g_v.reshape(1, width), w_s.astype(BF16), bsb)


def _halo_specs(tm, width, col_block, tile_of, n_row_blocks):
    per = tm // HALO
    cur = pl.BlockSpec((tm, width), lambda i: (tile_of(i), col_block))
    prev = pl.BlockSpec((HALO, width), lambda i: (jnp.maximum(tile_of(i) * per - 1, 0), col_block))
    nxt = pl.BlockSpec((HALO, width),
                       lambda i: (jnp.minimum((tile_of(i) + 1) * per, n_row_blocks - 1), col_block))
    return [cur, prev, nxt]


def _fill_ext(ext, cur_ref, prev_ref, next_ref, tile, tm, seq_len):
    first = (tile * tm) % seq_len == 0
    last = ((tile + 1) * tm) % seq_len == 0
    ext[0:HALO, :] = jnp.where(first, 0.0, prev_ref[...])
    ext[HALO:HALO + tm, :] = cur_ref[...]
    ext[HALO + tm:HALO + tm + HALO, :] = jnp.where(last, 0.0, next_ref[...])


def _lru_kernel(xb_ref, xp_ref, xn_ref, cw_ref, cb_ref, wax_ref, ba_ref, bx_ref, lam_ref, h0_ref,
                o_ref, ext, a_scr, b_scr, carry, *, tm, seq_len, n_tiles, reverse):
    i = pl.program_id(0)
    tile = (n_tiles - 1 - i) if reverse else i
    _fill_ext(ext, xb_ref, xp_ref, xn_ref, tile, tm, seq_len)
    kk = cw_ref.shape[0]
    left = kk // 2
    xc = cb_ref[...] + cw_ref[0:1, :] * ext[pl.ds(HALO - left, tm), :]
    for k in range(1, kk):
        xc = xc + cw_ref[k:k + 1, :] * ext[pl.ds(HALO - left + k, tm), :]

    width = xc.shape[1]
    hd = width // LRU_HEADS
    lam = lam_ref[...]
    sp = jnp.maximum(-lam, 0.0) + jnp.log(1.0 + jnp.exp(-jnp.abs(lam)))
    for hh in range(LRU_HEADS):
        cs = slice(hh * hd, (hh + 1) * hd)
        xh = xc[:, cs]
        pre = _bdot(xh.astype(BF16), wax_ref[hh])
        rg = _sigmoid(pre[:, :hd] + ba_ref[:, cs])
        ig = _sigmoid(pre[:, hd:] + bx_ref[:, cs])
        a = jnp.exp(-LRU_C * rg * sp[:, cs])
        a_scr[:, cs] = a
        b_scr[:, cs] = jnp.sqrt(1.0 - a * a) * (ig * xh)

    first = (tile * tm) % seq_len == 0
    last = ((tile + 1) * tm) % seq_len == 0

    @pl.when(last if reverse else first)
    def _():
        carry[...] = jnp.broadcast_to(h0_ref[0], carry.shape)

    row = lax.broadcasted_iota(jnp.int32, (SUBLANES, width), 0)
    nblk = tm // SUBLANES

    def body(k, c):
        blk = (nblk - 1 - k) if reverse else k
        r0 = pl.multiple_of(blk * SUBLANES, SUBLANES)
        a = a_scr[pl.ds(r0, SUBLANES), :]
        b = b_scr[pl.ds(r0, SUBLANES), :]
        for s in (1, 2, 4):
            shift = (SUBLANES - s) if reverse else s
            a_sh = pltpu.roll(a, shift, 0)
            b_sh = pltpu.roll(b, shift, 0)
            m = (row < SUBLANES - s) if reverse else (row >= s)
            b = jnp.where(m, a * b_sh + b, b)
            a = jnp.where(m, a * a_sh, a)
        h = a * c + b
        b_scr[pl.ds(r0, SUBLANES), :] = h
        edge = h[0:1, :] if reverse else h[SUBLANES - 1:SUBLANES, :]
        return jnp.broadcast_to(edge, c.shape)

    carry[...] = lax.fori_loop(0, nblk, body, carry[...])
    o_ref[...] = b_scr[...].astype(o_ref.dtype)


def lru_scan(z, col_block, width, conv_w, conv_b, w_a, b_a, w_x, b_x, lam, h0, seq_len, reverse,
             out_dtype=F32, tm=512):
    r = z.shape[0]
    tm = _row_tile(seq_len, tm)
    n_tiles = r // tm
    nb = r // seq_len
    tile_of = (lambda i: n_tiles - 1 - i) if reverse else (lambda i: i)
    wax = jnp.concatenate([w_a, w_x], axis=-1).astype(BF16)
    kk = conv_w.shape[0]
    hd = width // LRU_HEADS
    const2 = lambda i: (0, 0)
    return pl.pallas_call(
        functools.partial(_lru_kernel, tm=tm, seq_len=seq_len, n_tiles=n_tiles, reverse=reverse),
        grid=(n_tiles,),
        in_specs=_halo_specs(tm, width, col_block, tile_of, r // HALO) + [
            pl.BlockSpec((kk, width), const2),
            pl.BlockSpec((1, width), const2),
            pl.BlockSpec((LRU_HEADS, hd, 2 * hd), lambda i: (0, 0, 0)),
            pl.BlockSpec((1, width), const2),
            pl.BlockSpec((1, width), const2),
            pl.BlockSpec((1, width), const2),
            pl.BlockSpec((1, 1, width), lambda i: ((tile_of(i) * tm) // seq_len, 0, 0)),
        ],
        out_specs=pl.BlockSpec((tm, width), lambda i: (tile_of(i), 0)),
        out_shape=jax.ShapeDtypeStruct((r, width), out_dtype),
        scratch_shapes=[
            pltpu.VMEM((tm + 2 * HALO, width), F32),
            pltpu.VMEM((tm, width), F32),
            pltpu.VMEM((tm, width), F32),
            pltpu.VMEM((SUBLANES, width), F32),
        ],
        compiler_params=_cparams("arbitrary"),
        name="lru_scan_bwd" if reverse else "lru_scan_fwd",
    )(z, z, z, conv_w, conv_b.reshape(1, width), wax, b_a.reshape(1, width), b_x.reshape(1, width),
      lam.reshape(1, width), h0.reshape(nb, 1, width))


def _lru_out_prologue(gate, hf, hb):
    return (_gelu(gate) * (hf.astype(F32) + hb.astype(F32))).astype(BF16)


def _pool_kernel(p_ref, pp_ref, pn_ref, wg_ref, bg_ref, sc_ref, o_ref, ext, *, tm, seq_len):
    i = pl.program_id(0)
    _fill_ext(ext, p_ref, pp_ref, pn_ref, i, tm, seq_len)
    width = o_ref.shape[1]
    gw = width // len(POOL_WINDOWS)
    t = ((i * tm) % seq_len + lax.broadcasted_iota(jnp.int32, (tm, 1), 0))
    for g, win in enumerate(POOL_WINDOWS):
        half = win // 2
        cs = slice(g * gw, (g + 1) * gw)
        s = ext[pl.ds(HALO - half, tm), cs]
        for k in range(1 - half, half):
            s = s + ext[pl.ds(HALO + k, tm), cs]
        cnt = (jnp.minimum(t + half, seq_len) - jnp.maximum(t - half, 0)).astype(F32)
        q = s / cnt - ext[pl.ds(HALO, tm), cs]
        y = _bdot(q.astype(BF16), wg_ref[g]) + bg_ref[:, cs]
        o_ref[:, cs] = (y * sc_ref[:, cs]).astype(BF16)


def pool_mix(p, w_g, b_g, scale, seq_len, tm=512):
    r, width = p.shape
    assert max(POOL_WINDOWS) // 2 <= HALO
    tm = _row_tile(seq_len, tm)
    ng, gw, _ = w_g.shape
    const2 = lambda i: (0, 0)
    return pl.pallas_call(
        functools.partial(_pool_kernel, tm=tm, seq_len=seq_len),
        grid=(r // tm,),
        in_specs=_halo_specs(tm, width, 0, lambda i: i, r // HALO) + [
            pl.BlockSpec((ng, gw, gw), lambda i: (0, 0, 0)),
            pl.BlockSpec((1, width), const2),
            pl.BlockSpec((1, width), const2),
        ],
        out_specs=pl.BlockSpec((tm, width), lambda i: (i, 0)),
        out_shape=jax.ShapeDtypeStruct((r, width), BF16),
        scratch_shapes=[pltpu.VMEM((tm + 2 * HALO, width), F32)],
        compiler_params=_cparams("parallel"),
        name="pool_mix",
    )(p, p, p, w_g.astype(BF16), b_g.reshape(1, width), scale.reshape(1, width))


def _hyfilter_kernel(z_ref, zt_ref, w1_ref, b1_ref, w2_ref, b2_ref, w3_ref, b3_ref, fr_ref, wo_ref, dl_ref,
                     o_ref, *, n_out, seq_len):
    z = z_ref[...]
    tm = z.shape[0]
    t = pl.program_id(0) * tm + lax.broadcasted_iota(jnp.int32, (tm, 1), 0)
    keep = (t != seq_len).astype(F32)
    h = jnp.sin(fr_ref[:, 0:1] * (_hdot(w1_ref[...], zt_ref[...]) + b1_ref[...]))
    h = jnp.sin(fr_ref[:, 1:2] * (_hdot(w2_ref[...], h) + b2_ref[...]))
    h = jnp.sin(fr_ref[:, 2:3] * (_hdot(w3_ref[...], h) + b3_ref[...]))
    window = jnp.exp(-z[:, 0:1] * dl_ref[...]) * keep
    width = dl_ref.shape[1]
    hb = h.astype(BF16)
    for q in range(n_out):
        cs = slice(q * width, (q + 1) * width)
        y = lax.dot_general(hb, wo_ref[:, cs], (((0,), (0,)), ((), ())), preferred_element_type=F32)
        o_ref[:, cs] = y * window


def hyena_filters(seq_len, width, f_w1, f_b1, f_w2, f_b2, f_w3, f_b3, f_freq, f_wout, tm=512):
    n = 2 * seq_len
    rows = np.arange(n)
    lag = np.where(rows < seq_len, rows, np.minimum(n - rows, seq_len - 1))
    t = (jnp.asarray(lag, dtype=F32) / (seq_len - 1))[:, None]
    w = 2.0 * math.pi * jnp.asarray(lag, dtype=F32)[:, None] / seq_len
    bands = jnp.linspace(1e-4, HYENA_BANDS - 1, HYENA_BANDS, dtype=F32)[None, :]
    z = jnp.concatenate([t, jnp.cos(bands * w), jnp.sin(-bands * w)], axis=-1)
    emb = z.shape[1]
    hid = f_w1.shape[1]
    zp = jnp.zeros((n, LANES), F32).at[:, :emb].set(z)
    order = f_wout.shape[1] // (2 * width)
    wo = f_wout.reshape(hid, order, 2, width).transpose(2, 0, 1, 3).reshape(2, hid, order * width)
    w1p = jnp.zeros((LANES, hid), F32).at[:emb].set(f_w1)
    max_decay = math.log(HYENA_DECAY_TARGET) / HYENA_FAST_DECAY
    min_decay = math.log(HYENA_DECAY_TARGET) / HYENA_SLOW_DECAY
    deltas = jnp.abs(jnp.linspace(min_decay, max_decay, width, dtype=F32)).reshape(1, width)
    n_tot = order * width
    tm = _row_tile(seq_len, tm)
    c2 = lambda i: (0, 0)
    return pl.pallas_call(
        functools.partial(_hyfilter_kernel, n_out=order, seq_len=seq_len),
        grid=(n // tm,),
        in_specs=[
            pl.BlockSpec((tm, LANES), lambda i: (i, 0)),
            pl.BlockSpec((LANES, tm), lambda i: (0, i)),
            pl.BlockSpec((hid, LANES), c2), pl.BlockSpec((hid, 1), c2),
            pl.BlockSpec((hid, hid), c2), pl.BlockSpec((hid, 1), c2),
            pl.BlockSpec((hid, hid), c2), pl.BlockSpec((hid, 1), c2),
            pl.BlockSpec((hid, 3), c2),
            pl.BlockSpec((None, hid, n_tot), lambda i: ((i * tm) // seq_len, 0, 0)),
            pl.BlockSpec((1, width), c2),
        ],
        out_specs=pl.BlockSpec((tm, n_tot), lambda i: (i, 0)),
        out_shape=jax.ShapeDtypeStruct((n, n_tot), F32),
        compiler_params=_cparams("parallel"),
        name="hyena_filters",
    )(zp, zp.T, w1p.T, f_b1.reshape(hid, 1), f_w2.T, f_b2.reshape(hid, 1), f_w3.T, f_b3.reshape(hid, 1),
      f_freq.T, wo.astype(BF16), deltas)


def _dft_tables(seq_len):
    n = 2 * seq_len
    p_ = FFT_P
    n1 = n // p_
    t1n = n1 // 2
    f1 = np.arange(n1)[:, None]
    t1 = np.arange(n1)[None, :]
    ang1 = 2.0 * np.pi * ((f1 * t1) % n1) / n1
    c1, s1 = np.cos(ang1), np.sin(ang1)
    w1 = np.zeros((2 * n1, 2 * t1n))
    w1[:n1, :t1n] = c1[:, :t1n]
    w1[:n1, t1n:] = s1[:, :t1n]
    w1[n1:, :t1n] = -s1[:, :t1n]
    w1[n1:, t1n:] = c1[:, :t1n]
    nh = n1 // 2 + 1
    nhp = -(-nh // SUBLANES) * SUBLANES
    w1k = np.zeros((2 * nhp, n1))
    w1k[:nh] = c1[:nh]
    w1k[nhp:nhp + nh] = -s1[:nh]
    w1i = np.zeros((2 * t1n, 2 * n1))
    ct, st = c1.T[:t1n] / n, s1.T[:t1n] / n
    w1i[:t1n, :n1] = ct
    w1i[:t1n, n1:] = -st
    w1i[t1n:, :n1] = st
    w1i[t1n:, n1:] = ct
    f2 = np.arange(p_)[:, None]
    pp = np.arange(p_)[None, :]
    ang2 = 2.0 * np.pi * ((f2 * pp) % p_) / p_
    c2, s2 = np.cos(ang2), np.sin(ang2)
    fb = np.block([[c2, s2], [-s2, c2]])
    fbi = np.block([[c2, -s2], [s2, c2]])
    fb = np.stack([fb, np.concatenate([fb[:p_][::-1], fb[p_:][::-1]], axis=0)])
    fbi = np.stack([fbi, np.concatenate([fbi[:, :p_][:, ::-1], fbi[:, p_:][:, ::-1]], axis=1)])
    angt = 2.0 * np.pi * ((np.arange(n1)[:, None] * np.arange(p_)[None, :]) % n) / n
    lane_bcast = lambda a: jnp.broadcast_to(jnp.asarray(a, dtype=F32)[:, :, None], (n1, p_, LANES))
    as_bf = lambda a: jnp.asarray(a, dtype=F32).astype(BF16)
    return dict(n1=n1, t1n=t1n, nh=nh, w1=as_bf(w1), w1k=as_bf(w1k), w1i=as_bf(w1i), fb=as_bf(fb), fbi=as_bf(fbi),
                twc=lane_bcast(np.cos(angt)), tws=lane_bcast(-np.sin(angt)))


DFT_PB = 16


def _rows_at(ref, pp):
    m, pb, _ = ref.shape
    return ref.reshape(m * pb, LANES)[pl.ds(pp, m, stride=pb), :]


def _set_rows_at(ref, pp, val):
    m, pb, _ = ref.shape
    ref.reshape(m * pb, LANES)[pl.ds(pp, m, stride=pb), :] = val


def _pack_complex(re, im):
    r = lax.bitcast_convert_type(re.astype(BF16).astype(F32), jnp.uint32)
    i = lax.bitcast_convert_type(im.astype(BF16).astype(F32), jnp.uint32)
    return r | (i >> 16)


def _unpack_complex(w):
    re = lax.bitcast_convert_type(w & jnp.uint32(0xFFFF0000), F32)
    im = lax.bitcast_convert_type(w << 16, F32)
    return re, im


def _dft1_kernel(w_ref, x_ref, o_ref):
    half = w_ref.shape[0] // 2
    for pp in range(x_ref.shape[1]):
        y = _bdot(w_ref[...], _rows_at(x_ref, pp).astype(BF16))
        _set_rows_at(o_ref, pp, _pack_complex(y[:half], y[half:]))


def dft_stage1(w, x4, idx):
    m, k = w.shape
    m = m // 2
    _, _, p_, c = x4.shape
    pb = min(DFT_PB, p_)
    return pl.pallas_call(
        _dft1_kernel,
        grid=(c // LANES, p_ // pb),
        in_specs=[pl.BlockSpec((2 * m, k), lambda j, q: (0, 0)),
                  pl.BlockSpec((None, k, pb, LANES), lambda j, q: (idx, 0, q, j))],
        out_specs=pl.BlockSpec((m, pb, LANES), lambda j, q: (0, q, j)),
        out_shape=jax.ShapeDtypeStruct((m, p_, c), jnp.uint32),
        compiler_params=_cparams("parallel", "parallel"),
        name="dft_stage1",
    )(w, x4)


def _dft3_kernel(w_ref, b_ref, g_ref, v_ref, bias_ref, o_ref):
    for pp in range(b_ref.shape[1]):
        re, im = _unpack_complex(_rows_at(b_ref, pp))
        y = _bdot(w_ref[...], jnp.concatenate([re, im], axis=0).astype(BF16))
        _set_rows_at(o_ref, pp, _rows_at(g_ref, pp) * (y + bias_ref[...] * _rows_at(v_ref, pp)))


def _dft3_dft1_kernel(w_ref, wn_ref, b_ref, g_ref, v_ref, bias_ref, o_ref, a_ref):
    half = wn_ref.shape[0] // 2
    for pp in range(b_ref.shape[1]):
        re, im = _unpack_complex(_rows_at(b_ref, pp))
        y = _bdot(w_ref[...], jnp.concatenate([re, im], axis=0).astype(BF16))
        y = _rows_at(g_ref, pp) * (y + bias_ref[...] * _rows_at(v_ref, pp))
        _set_rows_at(o_ref, pp, y)
        a = _bdot(wn_ref[...], y.astype(BF16))
        _set_rows_at(a_ref, pp, _pack_complex(a[:half], a[half:]))


def dft_inverse_stage1_gate_forward(w, w_next, b3, gate4, gate_idx, v4, v_idx, bias):
    m, k = w.shape
    k = k // 2
    mn = w_next.shape[0] // 2
    _, p_, c = b3.shape
    pb = min(DFT_PB, p_)
    lead = lambda idx: (lambda j, q: (idx, 0, q, j))
    blk = lambda rows: pl.BlockSpec((rows, pb, LANES), lambda j, q: (0, q, j))
    return pl.pallas_call(
        _dft3_dft1_kernel,
        grid=(c // LANES, p_ // pb),
        in_specs=[pl.BlockSpec((m, 2 * k), lambda j, q: (0, 0)),
                  pl.BlockSpec((2 * mn, m), lambda j, q: (0, 0)),
                  blk(k),
                  pl.BlockSpec((None, m, pb, LANES), lead(gate_idx)),
                  pl.BlockSpec((None, m, pb, LANES), lead(v_idx)),
                  pl.BlockSpec((1, LANES), lambda j, q: (0, j))],
        out_specs=[blk(m), blk(mn)],
        out_shape=[jax.ShapeDtypeStruct((m, p_, c), F32), jax.ShapeDtypeStruct((mn, p_, c), jnp.uint32)],
        compiler_params=_cparams("parallel", "parallel"),
        name="dft_inverse_stage1_gate_forward",
    )(w, w_next, b3, gate4, v4, bias.reshape(1, c).astype(F32))


def dft_inverse_stage1_gate(w, b3, gate4, gate_idx, v4, v_idx, bias):
    m, k = w.shape
    k = k // 2
    _, p_, c = b3.shape
    pb = min(DFT_PB, p_)
    lead = lambda idx: (lambda j, q: (idx, 0, q, j))
    return pl.pallas_call(
        _dft3_kernel,
        grid=(c // LANES, p_ // pb),
        in_specs=[pl.BlockSpec((m, 2 * k), lambda j, q: (0, 0)),
                  pl.BlockSpec((k, pb, LANES), lambda j, q: (0, q, j)),
                  pl.BlockSpec((None, m, pb, LANES), lead(gate_idx)),
                  pl.BlockSpec((None, m, pb, LANES), lead(v_idx)),
                  pl.BlockSpec((1, LANES), lambda j, q: (0, j))],
        out_specs=pl.BlockSpec((m, pb, LANES), lambda j, q: (0, q, j)),
        out_shape=jax.ShapeDtypeStruct((m, p_, c), F32),
        compiler_params=_cparams("parallel", "parallel"),
        name="dft_inverse_stage1_gate",
    )(w, b3, gate4, v4, bias.reshape(1, c).astype(F32))


def _twiddle(re, im, tc, ts, reps):
    tc = jnp.tile(tc, (1, reps))
    ts = jnp.tile(ts, (1, reps))
    return re * tc - im * ts, re * ts + im * tc


def _spec_fwd_kernel(a_ref, tc_ref, ts_ref, fb_ref, o_ref):
    reps = a_ref.shape[-1] // LANES
    re, im = _unpack_complex(a_ref[0])
    tr, ti = _twiddle(re, im, tc_ref[0], ts_ref[0], reps)
    rhs = jnp.concatenate([tr, ti], axis=0).astype(BF16)
    o_ref[0] = _bdot(fb_ref[...], rhs).astype(o_ref.dtype)


def _spec_conv_kernel(a_ref, tc_ref, ts_ref, fb_ref, fbi_ref, k_ref, o_ref, *, n1):
    p_ = FFT_P
    reps = a_ref.shape[-1] // LANES
    conj = jnp.where(pl.program_id(0) > n1 // 2, -1.0, 1.0).astype(F32)
    tc, ts = tc_ref[0], ts_ref[0]
    re, im = _unpack_complex(a_ref[0])
    tr, ti = _twiddle(re, im, tc, ts, reps)
    x = _bdot(fb_ref[...], jnp.concatenate([tr, ti], axis=0).astype(BF16))
    xr, xi = x[:p_], x[p_:]
    kr, ki = k_ref[0, :p_].astype(F32), conj * k_ref[0, p_:].astype(F32)
    yr = xr * kr - xi * ki
    yi = xr * ki + xi * kr
    bv = _bdot(fbi_ref[...], jnp.concatenate([yr, yi], axis=0).astype(BF16))
    orr, oi = _twiddle(bv[:p_], bv[p_:], tc, -ts, reps)
    o_ref[0] = _pack_complex(orr, oi)


def spectrum_forward(a3, tab, ct=1024):
    _, p_, c = a3.shape
    nh = tab["nh"]
    ct = min(ct, c)
    return pl.pallas_call(
        _spec_fwd_kernel,
        grid=(c // ct, nh),
        in_specs=[
            pl.BlockSpec((1, p_, ct), lambda j, f: (f, 0, j)),
            pl.BlockSpec((1, p_, LANES), lambda j, f: (f, 0, 0)),
            pl.BlockSpec((1, p_, LANES), lambda j, f: (f, 0, 0)),
            pl.BlockSpec((None, 2 * p_, 2 * p_), lambda j, f: (0, 0, 0)),
        ],
        out_specs=pl.BlockSpec((1, 2 * p_, ct), lambda j, f: (f, 0, j)),
        out_shape=jax.ShapeDtypeStruct((nh, 2 * p_, c), BF16),
        compiler_params=_cparams("parallel", "parallel"),
        name="dft_filter_stage2",
    )(a3, tab["twc"], tab["tws"], tab["fb"])


def spectrum_conv(a3, kspec, k_col_block, tab):
    n1, p_, c = a3.shape
    half = n1 // 2
    mirrored = lambda f: (f > half).astype(jnp.int32)
    return pl.pallas_call(
        functools.partial(_spec_conv_kernel, n1=n1),
        grid=(n1,),
        in_specs=[
            pl.BlockSpec((1, p_, c), lambda f: (f, 0, 0)),
            pl.BlockSpec((1, p_, LANES), lambda f: (f, 0, 0)),
            pl.BlockSpec((1, p_, LANES), lambda f: (f, 0, 0)),
            pl.BlockSpec((None, 2 * p_, 2 * p_), lambda f: (mirrored(f), 0, 0)),
            pl.BlockSpec((None, 2 * p_, 2 * p_), lambda f: (mirrored(f), 0, 0)),
            pl.BlockSpec((1, 2 * p_, c), lambda f: (jnp.where(f > half, n1 - f, f), 0, k_col_block)),
        ],
        out_specs=pl.BlockSpec((1, p_, c), lambda f: (f, 0, 0)),
        out_shape=jax.ShapeDtypeStruct(a3.shape, jnp.uint32),
        compiler_params=_cparams("arbitrary"),
        name="dft_stage2_conv",
    )(a3, tab["twc"], tab["tws"], tab["fb"], tab["fbi"], kspec)


def hyena_long_convs(xv, filt, bias, batch, seq_len):
    assert batch == 2, "the two sequences of the batch are packed as one complex sequence"
    c = xv.shape[-1]
    tab = _dft_tables(seq_len)
    n1, t1n, p_ = tab["n1"], tab["t1n"], FFT_P
    order = bias.shape[0]
    ak = dft_stage1(tab["w1k"], filt.reshape(1, n1, p_, order * c), 0)
    kspec = spectrum_forward(ak, tab)
    xv4 = xv.reshape(3, batch * t1n, p_, c)
    y4, y_idx = xv4, 2
    a = dft_stage1(tab["w1"], y4, y_idx)
    for o in range(order):
        b3 = spectrum_conv(a, kspec, o, tab)
        if o + 1 < order:
            y, a = dft_inverse_stage1_gate_forward(tab["w1i"], tab["w1"], b3, xv4, o, y4, y_idx, bias[o])
        else:
            y = dft_inverse_stage1_gate(tab["w1i"], b3, xv4, o, y4, y_idx, bias[o])
        y4, y_idx = y[None], 0
    return y.reshape(batch * seq_len, c)


def kernel(x, c, ctx, c_ctx, ada_w, ada_b, norm_g, gmlp_w_in, gmlp_g_v, gmlp_w_s, gmlp_b_s, gmlp_w_out, lru_w_in, lru_conv_w, lru_conv_b, lru_w_a, lru_b_a, lru_w_x, lru_b_x, lru_lam, lru_w_out, hyena_w_in, hyena_conv_w, hyena_conv_b, hyena_f_w1, hyena_f_b1, hyena_f_w2, hyena_f_b2, hyena_f_w3, hyena_f_b3, hyena_f_freq, hyena_f_wout, hyena_bias, hyena_w_out, pool_w_in, pool_w_g, pool_b_g, pool_scale, pool_w_out, ffn_w_gate, ffn_w_up, ffn_w_down, moe_w_router, moe_w_gate, moe_w_up, moe_w_down):
    B, L, D = x.shape
    Lc = ctx.shape[1]
    depth = ada_w.shape[0]
    n_mixers = 4
    assert B + 1 <= SUBLANES

    cc = jnp.zeros((SUBLANES, D), F32).at[:B].set(c).at[B].set(c_ctx)
    ada = ada_all(cc, ada_w, ada_b).reshape(depth, SUBLANES, 6, D)

    xs = add_pos(x)
    xc = ctx.reshape(B * Lc, D)
    last_ctx = max([i for i in range(depth) if i % n_mixers == 1], default=-1)

    for i in range(depth):
        kind, j = i % n_mixers, i // n_mixers
        update_ctx = i < last_ctx
        read_ctx = i <= last_ctx
        lat = [ada[i, :B, q].reshape(B, 1, D) for q in range(6)]
        cx = [jnp.broadcast_to(ada[i, B:B + 1, q].reshape(1, 1, D), (B, 1, D)) for q in range(6)]
        g = norm_g[i]
        streams = [(xs, lat, L)]
        if read_ctx:
            streams.append((xc, cx, Lc))

        if kind == 0:
            outs = []
            for (s, m, sl) in streams[:1 + int(update_ctx)]:
                hw = norm_mod_matmul(s, g[0], m[0], m[1], gmlp_w_in[j], sl, BF16, act="gelu")
                a = gmlp_spatial(hw, gmlp_g_v[j], gmlp_w_s[j], gmlp_b_s[j])
                outs.append(matmul_norm_res([(a, a.shape[1], 0)], _identity_bf16, gmlp_w_out[j],
                                            g[1], m[2], s, sl))
        elif kind == 1:
            W = lru_w_in.shape[2] // 2
            sc_args = lambda d: (lru_conv_w[j], lru_conv_b[j], lru_w_a[j, d], lru_b_a[j, d],
                                 lru_w_x[j, d], lru_b_x[j, d], lru_lam[j, d])
            zc = norm_mod_matmul(xc, g[0], cx[0], cx[1], lru_w_in[j], Lc, F32)
            zero = jnp.zeros((B, W), F32)
            hf_c = lru_scan(zc, 1, W, *sc_args(0), zero, Lc, False)
            hb_c = lru_scan(zc, 1, W, *sc_args(1), zero, Lc, True)
            zl = norm_mod_matmul(xs, g[0], lat[0], lat[1], lru_w_in[j], L, F32)
            hf = lru_scan(zl, 1, W, *sc_args(0), hf_c.reshape(B, Lc, W)[:, -1], L, False, out_dtype=BF16)
            hb = lru_scan(zl, 1, W, *sc_args(1), hb_c.reshape(B, Lc, W)[:, 0], L, True, out_dtype=BF16)
            outs = [matmul_norm_res([(zl, W, 0), (hf, W, 0), (hb, W, 0)], _lru_out_prologue,
                                    lru_w_out[j], g[1], lat[2], xs, L)]
            if update_ctx:
                outs.append(matmul_norm_res([(zc, W, 0), (hf_c, W, 0), (hb_c, W, 0)], _lru_out_prologue,
                                            lru_w_out[j], g[1], cx[2], xc, Lc))
        elif kind == 2:
            W = hyena_w_out.shape[1]
            filt = hyena_filters(L, W, hyena_f_w1[j], hyena_f_b1[j], hyena_f_w2[j], hyena_f_b2[j],
                                 hyena_f_w3[j], hyena_f_b3[j], hyena_f_freq[j], hyena_f_wout[j])
            outs = []
            for (s, m, sl) in streams[:1 + int(update_ctx)]:
                nb = s.shape[0] // sl
                xv = norm_mod_matmul_conv(s, g[0], m[0], m[1], hyena_w_in[j], hyena_conv_w[j],
                                          hyena_conv_b[j], W, sl)
                fl = filt if sl == L else hyena_filters(
                    sl, W, hyena_f_w1[j], hyena_f_b1[j], hyena_f_w2[j], hyena_f_b2[j],
                    hyena_f_w3[j], hyena_f_b3[j], hyena_f_freq[j], hyena_f_wout[j])
                y = hyena_long_convs(xv, fl, hyena_bias[j], nb, sl)
                outs.append(matmul_norm_res([(y, W, 0)], _identity_bf16, hyena_w_out[j], g[1], m[2], s, sl))
        else:
            outs = []
            for (s, m, sl) in streams[:1 + int(update_ctx)]:
                p = norm_mod_matmul(s, g[0], m[0], m[1], pool_w_in[j], sl, F32)
                a = pool_mix(p, pool_w_g[j], pool_b_g[j], pool_scale[j], sl)
                outs.append(matmul_norm_res([(a, a.shape[1], 0)], _identity_bf16, pool_w_out[j],
                                            g[1], m[2], s, sl))
        xs = outs[0]
        if update_ctx:
            xc = outs[1]

        k = i // 2
        todo = [(xs, lat, L)] + ([(xc, cx, Lc)] if update_ctx else [])
        res = []
        for (s, m, sl) in todo:
            if i % 2 == 0:
                res.append(ffn_dense(s, g[2], m[3], m[4], ffn_w_gate[k], ffn_w_up[k], ffn_w_down[k],
                                     g[3], m[5], sl))
            else:
                res.append(moe_block(s, g[2], m[3], m[4], moe_w_router[k], moe_w_gate, moe_w_up,
                                     moe_w_down, k, g[3], m[5], sl))
        xs = res[0]
        if update_ctx:
            xc = res[1]
    return xs.reshape(B, L, D)
```
